```python
import jax, jax.numpy as jnp
from jax import lax
import numpy as np

D_MODEL = 4096
BATCH = 2
SEQ = 4096
DEPTH = 2

GRID_W = 64
CTX_LEN = 256
HEAD_DIM = 128
N_MOD = 6
EPS = 1e-6
NEG = -1e30
ROPE_BASE = 10000.0

CONV_CH = D_MODEL // 4
CONV_K = 31
WIN_HQ = D_MODEL // 256
WIN_HKV = WIN_HQ // 4
WINDOW = 128
BLK = 128
NA_H = D_MODEL // 512
NA_KH = 8
NA_KW = 16
PEER_HEADS = 8
N_KEYS = 128
N_EXPERTS = N_KEYS * N_KEYS
PEER_QDIM = 256
PK_DIM = PEER_QDIM // 2
PEER_TOPK = 16

W_CONV_IN = 2 * CONV_CH
W_WIN_Q = WIN_HQ * HEAD_DIM
W_NA_Q = NA_H * HEAD_DIM
W_GATE = 3 * D_MODEL
W_WIN_KV = WIN_HKV * HEAD_DIM
W_NA_KV = NA_H * HEAD_DIM
IN_SPLITS = (W_CONV_IN, W_WIN_Q, W_NA_Q, W_GATE, W_WIN_KV, W_WIN_KV, W_NA_KV, W_NA_KV)
KV_OFF = W_CONV_IN + W_WIN_Q + W_NA_Q + W_GATE
IN_COLS = KV_OFF + 2 * W_WIN_KV + 2 * W_NA_KV

kernel_name = 'hybrid_flow_backbone'


def split_cols(a, sizes):
    idx = np.cumsum(np.array(sizes))[:-1].tolist()
    return jnp.split(a, idx, axis=-1)


def heads(t, n):
    return t.reshape(t.shape[:-1] + (n, HEAD_DIM))


def rms_norm(x, g):
    x32 = x.astype(jnp.float32)
    y = x32 * lax.rsqrt(jnp.mean(x32 * x32, axis=-1, keepdims=True) + EPS)
    return (y * g.astype(jnp.float32)).astype(x.dtype)


def layer_norm(x, g, b):
    x32 = x.astype(jnp.float32)
    mu = jnp.mean(x32, axis=-1, keepdims=True)
    xc = x32 - mu
    var = jnp.mean(xc * xc, axis=-1, keepdims=True)
    return (xc * lax.rsqrt(var + EPS) * g.astype(jnp.float32) + b.astype(jnp.float32)).astype(x.dtype)


def axial_rope_tables(n_tok):
    t = jnp.arange(n_tok)
    row = (t // GRID_W).astype(jnp.float32)
    col = (t % GRID_W).astype(jnp.float32)
    axis_dim = HEAD_DIM // 2
    inv = ROPE_BASE ** (-jnp.arange(0, axis_dim, 2, dtype=jnp.float32) / axis_dim)
    ang = jnp.stack([row[:, None] * inv, col[:, None] * inv], axis=1)
    return jnp.cos(ang), jnp.sin(ang)


def apply_rope(x, cos, sin):
    xs = x.astype(jnp.float32).reshape(x.shape[:-1] + (2, 2, HEAD_DIM // 4))
    x1, x2 = xs[..., 0, :], xs[..., 1, :]
    cb, sb = cos[None, :, None], sin[None, :, None]
    out = jnp.stack([x1 * cb - x2 * sb, x1 * sb + x2 * cb], axis=-2)
    return out.reshape(x.shape).astype(x.dtype)


def conv_module(u, conv_w, conv_b, ln_g, ln_b, w_out):
    a, g = jnp.split(u, 2, axis=-1)
    h = a * jax.nn.sigmoid(g)
    h = lax.conv_general_dilated(h, conv_w[:, None, :], window_strides=(1,),
                                 padding=[(CONV_K // 2, CONV_K // 2)],
                                 dimension_numbers=('NWC', 'WIO', 'NWC'),
                                 feature_group_count=CONV_CH) + conv_b
    h = jax.nn.silu(layer_norm(h, ln_g, ln_b))
    return h @ w_out


def context_attention(qc, kc, vc, sink):
    B, L, HQ, d = qc.shape
    HKV = kc.shape[2]
    G = HQ // HKV
    qg = qc.reshape(B, L, HKV, G, d)
    s = jnp.einsum('bqhgd,bkhd->bhgqk', qg, kc).astype(jnp.float32) * (d ** -0.5)
    if sink is not None:
        snk = jnp.broadcast_to(sink.astype(jnp.float32).reshape(HKV, G, 1, 1), s.shape[:-1] + (1,))
        s = jnp.concatenate([s, snk], axis=-1)
    p = jax.nn.softmax(s, axis=-1)[..., :L].astype(vc.dtype)
    o = jnp.einsum('bhgqk,bkhd->bqhgd', p, vc)
    return o.reshape(B, L, HQ * d)


def window_attention(q, k, v, kc, vc, sink):
    B, S = q.shape[:2]
    L = kc.shape[1]
    G = WIN_HQ // WIN_HKV
    nb = S // BLK
    wb = WINDOW // BLK
    nwin = 2 * wb + 1
    nl = nwin * BLK
    scale = HEAD_DIM ** -0.5
    qb = q.reshape(B, nb, BLK, WIN_HKV, G, HEAD_DIM)
    pad = ((0, 0), (wb * BLK, wb * BLK), (0, 0), (0, 0))
    kp = jnp.pad(k, pad).reshape(B, nb + 2 * wb, BLK, WIN_HKV, HEAD_DIM)
    vp = jnp.pad(v, pad).reshape(B, nb + 2 * wb, BLK, WIN_HKV, HEAD_DIM)
    kw = jnp.concatenate([kp[:, i:i + nb] for i in range(nwin)], axis=2)
    vw = jnp.concatenate([vp[:, i:i + nb] for i in range(nwin)], axis=2)
    qpos = jnp.arange(nb)[:, None, None] * BLK + jnp.arange(BLK)[None, :, None]
    kpos = jnp.arange(nb)[:, None, None] * BLK - wb * BLK + jnp.arange(nl)[None, None, :]
    band = (jnp.abs(kpos - qpos) <= WINDOW) & (kpos >= 0) & (kpos < S)
    s_loc = jnp.einsum('bnqhgd,bnkhd->bnhgqk', qb, kw).astype(jnp.float32) * scale
    s_loc = jnp.where(band[None, :, None, None], s_loc, NEG)
    s_ctx = jnp.einsum('bnqhgd,bchd->bnhgqc', qb, kc).astype(jnp.float32) * scale
    snk = jnp.broadcast_to(sink.astype(jnp.float32).reshape(WIN_HKV, G, 1, 1), s_loc.shape[:-1] + (1,))
    p = jax.nn.softmax(jnp.concatenate([s_loc, s_ctx, snk], axis=-1), axis=-1)
    p_loc = p[..., :nl].astype(v.dtype)
    p_ctx = p[..., nl:nl + L].astype(v.dtype)
    o = (jnp.einsum('bnhgqk,bnkhd->bnqhgd', p_loc, vw)
         + jnp.einsum('bnhgqc,bchd->bnqhgd', p_ctx, vc))
    return o.reshape(B, S, WIN_HQ * HEAD_DIM)


def neighborhood_attention(q, k, v, kc, vc, rpb):
    B, S, H, d = q.shape
    rows = S // GRID_W
    kh = min(NA_KH, rows)
    ncb = GRID_W // NA_KW
    kwb = 2 * NA_KW
    nk = kh * kwb
    r = jnp.arange(rows)
    rs = jnp.clip(r - kh // 2, 0, rows - kh)
    row_idx = rs[:, None] + jnp.arange(kh)[None, :]
    c0 = jnp.arange(ncb) * NA_KW
    kb = jnp.clip(c0 - NA_KW // 2, 0, GRID_W - kwb)
    col_idx = kb[:, None] + jnp.arange(kwb)[None, :]
    tok = (row_idx[:, None, :, None] * GRID_W + col_idx[None, :, None, :]).reshape(rows, ncb, nk)
    kg = k[:, tok]
    vg = v[:, tok]
    qg = q.reshape(B, rows, ncb, NA_KW, H, d)
    scale = d ** -0.5
    s_loc = jnp.einsum('brcqhd,brckhd->brchqk', qg, kg).astype(jnp.float32) * scale
    qcol = c0[:, None] + jnp.arange(NA_KW)[None, :]
    cs = jnp.clip(qcol - NA_KW // 2, 0, GRID_W - NA_KW)
    kcol = col_idx[:, None, :]
    colmask = (kcol >= cs[..., None]) & (kcol < cs[..., None] + NA_KW)
    mask = jnp.broadcast_to(colmask[:, :, None, :], (ncb, NA_KW, kh, kwb)).reshape(ncb, NA_KW, nk)
    dr = row_idx - r[:, None] + (NA_KH - 1)
    dc = jnp.clip(kcol - qcol[..., None], -(NA_KW - 1), NA_KW - 1) + (NA_KW - 1)
    bias = rpb[:, dr[:, None, None, :, None], dc[None, :, :, None, :]]
    bias = bias.reshape(H, rows, ncb, NA_KW, nk).transpose(1, 2, 0, 3, 4).astype(jnp.float32)
    s_loc = jnp.where(mask[None, None, :, None], s_loc + bias[None], NEG)
    s_ctx = jnp.einsum('brcqhd,bkhd->brchqk', qg, kc).astype(jnp.float32) * scale
    p = jax.nn.softmax(jnp.concatenate([s_loc, s_ctx], axis=-1), axis=-1)
    o = (jnp.einsum('brchqk,brckhd->brcqhd', p[..., :nk].astype(v.dtype), vg)
         + jnp.einsum('brchqk,bkhd->brcqhd', p[..., nk:].astype(v.dtype), vc))
    return o.reshape(B, S, H * d)


def gated_merge(gate_logits, ya, yb, yc, w_out):
    ga, gb, gc = jnp.split(jax.nn.sigmoid(gate_logits), 3, axis=-1)
    return (ga * ya + gb * yb + gc * yc) @ w_out


def token_mixer(hx, hc, p, cos, sin, need_ctx):
    w_in = p['w_in']
    a_x, qw_x, qn_x, gt_x, kw_x, vw_x, kn_x, vn_x = split_cols(hx @ w_in, IN_SPLITS)
    if need_ctx:
        a_c, qw_c, qn_c, gt_c, kw_c, vw_c, kn_c, vn_c = split_cols(hc @ w_in, IN_SPLITS)
    else:
        kw_c, vw_c, kn_c, vn_c = split_cols(hc @ w_in[:, KV_OFF:], IN_SPLITS[4:])
    kwin_c = rms_norm(heads(kw_c, WIN_HKV), p['win_kn_g'])
    vwin_c = heads(vw_c, WIN_HKV)
    kna_c = rms_norm(heads(kn_c, NA_H), p['na_kn_g'])
    vna_c = heads(vn_c, NA_H)
    ya = conv_module(a_x, p['conv_w'], p['conv_b'], p['conv_ln_g'], p['conv_ln_b'], p['w_conv_out'])
    qwin = apply_rope(rms_norm(heads(qw_x, WIN_HQ), p['win_qn_g']), cos, sin)
    kwin = apply_rope(rms_norm(heads(kw_x, WIN_HKV), p['win_kn_g']), cos, sin)
    yb = window_attention(qwin, kwin, heads(vw_x, WIN_HKV), kwin_c, vwin_c, p['win_sink']) @ p['w_win_out']
    qna = rms_norm(heads(qn_x, NA_H), p['na_qn_g'])
    kna = rms_norm(heads(kn_x, NA_H), p['na_kn_g'])
    yc = neighborhood_attention(qna, kna, heads(vn_x, NA_H), kna_c, vna_c, p['na_rpb']) @ p['w_na_out']
    out_x = gated_merge(gt_x, ya, yb, yc, p['w_out'])
    if not need_ctx:
        return out_x, None
    ya_c = conv_module(a_c, p['conv_w'], p['conv_b'], p['conv_ln_g'], p['conv_ln_b'], p['w_conv_out'])
    qwin_c = rms_norm(heads(qw_c, WIN_HQ), p['win_qn_g'])
    yb_c = context_attention(qwin_c, kwin_c, vwin_c, p['win_sink']) @ p['w_win_out']
    qna_c = rms_norm(heads(qn_c, NA_H), p['na_qn_g'])
    yc_c = context_attention(qna_c, kna_c, vna_c, None) @ p['w_na_out']
    out_c = gated_merge(gt_c, ya_c, yb_c, yc_c, p['w_out'])
    return out_x, out_c


def peer_ffn(h, wq, k1, k2, u, v):
    T = h.shape[0]
    q = (h @ wq).reshape(T, PEER_HEADS, 2, PK_DIM)
    s1 = jnp.einsum('thd,hnd->thn', q[:, :, 0], k1).astype(jnp.float32)
    s2 = jnp.einsum('thd,hnd->thn', q[:, :, 1], k2).astype(jnp.float32)
    v1, i1 = lax.top_k(s1, PEER_TOPK)
    v2, i2 = lax.top_k(s2, PEER_TOPK)
    cand = (v1[..., :, None] + v2[..., None, :]).reshape(T, PEER_HEADS, PEER_TOPK * PEER_TOPK)
    cidx = (i1[..., :, None] * N_KEYS + i2[..., None, :]).reshape(T, PEER_HEADS, PEER_TOPK * PEER_TOPK)
    top_s, pos = lax.top_k(cand, PEER_TOPK)
    eidx = jnp.take_along_axis(cidx, pos, axis=-1).reshape(T, PEER_HEADS * PEER_TOPK)
    gate = jax.nn.softmax(top_s, axis=-1).reshape(T, PEER_HEADS * PEER_TOPK)
    act = jax.nn.gelu(jnp.einsum('td,nd->tn', h, u))
    a_sel = jnp.take_along_axis(act, eidx, axis=1)
    w = gate.astype(h.dtype) * a_sel
    wmat = jnp.zeros((T, N_EXPERTS), h.dtype).at[jnp.arange(T)[:, None], eidx].add(w)
    return wmat @ v


def hybrid_layer(x, xc, c, c_ctx, p, cos, sin, need_ctx):
    B, S, D = x.shape
    L = xc.shape[1]
    mod = jax.nn.silu(c) @ p['w_ada'] + p['b_ada']
    mod_c = jax.nn.silu(c_ctx) @ p['w_ada'] + p['b_ada']
    sh1, sc1, g1, sh2, sc2, g2 = jnp.split(mod[:, None, :], N_MOD, axis=-1)
    sh1c, sc1c, g1c, sh2c, sc2c, g2c = jnp.split(mod_c, N_MOD, axis=-1)
    hx = rms_norm(x, p['norm1_g']) * (1 + sc1) + sh1
    hc = rms_norm(xc, p['norm1_g']) * (1 + sc1c) + sh1c
    ox, oc = token_mixer(hx, hc, p, cos, sin, need_ctx)
    x = x + g1 * ox
    hx = rms_norm(x, p['norm2_g']) * (1 + sc2) + sh2
    if need_ctx:
        xc = xc + g1c * oc
        hc = rms_norm(xc, p['norm2_g']) * (1 + sc2c) + sh2c
        tokens = jnp.concatenate([hc, hx], axis=1).reshape(B * (L + S), D)
        f = peer_ffn(tokens, p['peer_wq'], p['peer_k1'], p['peer_k2'], p['peer_u'], p['peer_v']).reshape(B, L + S, D)
        xc = xc + g2c * f[:, :L]
        fx = f[:, L:]
    else:
        fx = peer_ffn(hx.reshape(B * S, D), p['peer_wq'], p['peer_k1'], p['peer_k2'], p['peer_u'], p['peer_v']).reshape(B, S, D)
    x = x + g2 * fx
    return x, xc


def setup_inputs(seed: int = 0) -> dict:
    key = jax.random.key(seed)
    ks = jax.random.split(key, 32)
    D = D_MODEL

    def nrm(k, shape, scale):
        return jax.random.normal(k, shape, jnp.float32) * scale

    return {
        'x': nrm(ks[0], (BATCH, SEQ, D), 1.0),
        'c': nrm(ks[1], (BATCH, D), 1.0),
        'ctx': nrm(ks[2], (BATCH, CTX_LEN, D), 1.0),
        'c_ctx': nrm(ks[3], (D,), 1.0),
        'w_ada': nrm(ks[4], (DEPTH, D, N_MOD * D), 0.5 * D ** -0.5),
        'b_ada': nrm(ks[5], (DEPTH, N_MOD * D), 0.02),
        'norm1_g': 1.0 + nrm(ks[6], (DEPTH, D), 0.02),
        'norm2_g': 1.0 + nrm(ks[7], (DEPTH, D), 0.02),
        'w_in': nrm(ks[8], (DEPTH, D, IN_COLS), D ** -0.5),
        'conv_w': nrm(ks[9], (DEPTH, CONV_K, CONV_CH), CONV_K ** -0.5),
        'conv_b': nrm(ks[10], (DEPTH, CONV_CH), 0.02),
        'conv_ln_g': 1.0 + nrm(ks[11], (DEPTH, CONV_CH), 0.02),
        'conv_ln_b': nrm(ks[12], (DEPTH, CONV_CH), 0.02),
        'w_conv_out': nrm(ks[13], (DEPTH, CONV_CH, D), CONV_CH ** -0.5),
        'win_qn_g': 1.0 + nrm(ks[14], (DEPTH, HEAD_DIM), 0.02),
        'win_kn_g': 1.0 + nrm(ks[15], (DEPTH, HEAD_DIM), 0.02),
        'win_sink': nrm(ks[16], (DEPTH, WIN_HQ), 1.0),
        'w_win_out': nrm(ks[17], (DEPTH, W_WIN_Q, D), W_WIN_Q ** -0.5),
        'na_qn_g': 1.0 + nrm(ks[18], (DEPTH, HEAD_DIM), 0.02),
        'na_kn_g': 1.0 + nrm(ks[19], (DEPTH, HEAD_DIM), 0.02),
        'na_rpb': nrm(ks[20], (DEPTH, NA_H, 2 * NA_KH - 1, 2 * NA_KW - 1), 0.5),
        'w_na_out': nrm(ks[21], (DEPTH, W_NA_Q, D), W_NA_Q ** -0.5),
        'w_out': nrm(ks[22], (DEPTH, D, D), D ** -0.5),
        'peer_wq': nrm(ks[23], (DEPTH, D, PEER_HEADS * PEER_QDIM), D ** -0.5),
        'peer_k1': nrm(ks[24], (DEPTH, PEER_HEADS, N_KEYS, PK_DIM), PK_DIM ** -0.5),
        'peer_k2': nrm(ks[25], (DEPTH, PEER_HEADS, N_KEYS, PK_DIM), PK_DIM ** -0.5),
        'peer_u': nrm(ks[26], (DEPTH, N_EXPERTS, D), D ** -0.5),
        'peer_v': nrm(ks[27], (DEPTH, N_EXPERTS, D), PEER_HEADS ** -0.5),
    }


def reference(x, c, ctx, c_ctx, w_ada, b_ada, norm1_g, norm2_g, w_in, conv_w, conv_b,
              conv_ln_g, conv_ln_b, w_conv_out, win_qn_g, win_kn_g, win_sink, w_win_out,
              na_qn_g, na_kn_g, na_rpb, w_na_out, w_out, peer_wq, peer_k1, peer_k2,
              peer_u, peer_v):
    cos, sin = axial_rope_tables(x.shape[1])
    xc = ctx
    for i in range(DEPTH):
        p = {
            'w_ada': w_ada[i], 'b_ada': b_ada[i], 'norm1_g': norm1_g[i], 'norm2_g': norm2_g[i],
            'w_in': w_in[i], 'conv_w': conv_w[i], 'conv_b': conv_b[i], 'conv_ln_g': conv_ln_g[i],
            'conv_ln_b': conv_ln_b[i], 'w_conv_out': w_conv_out[i], 'win_qn_g': win_qn_g[i],
            'win_kn_g': win_kn_g[i], 'win_sink': win_sink[i], 'w_win_out': w_win_out[i],
            'na_qn_g': na_qn_g[i], 'na_kn_g': na_kn_g[i], 'na_rpb': na_rpb[i], 'w_na_out': w_na_out[i],
            'w_out': w_out[i], 'peer_wq': peer_wq[i], 'peer_k1': peer_k1[i], 'peer_k2': peer_k2[i],
            'peer_u': peer_u[i], 'peer_v': peer_v[i],
        }
        x, xc = hybrid_layer(x, xc, c, c_ctx, p, cos, sin, need_ctx=(i < DEPTH - 1))
    return x
```

```python
import functools
import math

import jax
import jax.numpy as jnp
import numpy as np
from jax import lax
from jax.experimental import pallas as pl
from jax.experimental.pallas import tpu as pltpu

F32 = jnp.float32
BF16 = jnp.bfloat16

D = 4096
B = 2
S = 4096
L = 256
DEPTH = 2
GRID_W = 64
HD = 128
EPS = 1e-6
NEG = -1e30
ROPE_BASE = 10000.0

CONV_CH = D // 4
CONV_K = 31
WIN_HQ = 16
WIN_HKV = 4
WIN_G = WIN_HQ // WIN_HKV
NA_H = 8
NA_KH = 8
NA_KW = 16
PEER_HEADS = 8
N_KEYS = 128
N_EXPERTS = N_KEYS * N_KEYS
PEER_TOPK = 16

W_CONV_IN = 2 * CONV_CH
W_WIN_Q = WIN_HQ * HD
W_NA_Q = NA_H * HD
W_GATE = 3 * D
W_WIN_KV = WIN_HKV * HD
W_NA_KV = NA_H * HD
Q_WIN_OFF = W_CONV_IN
Q_NA_OFF = Q_WIN_OFF + W_WIN_Q
GATE_OFF = Q_NA_OFF + W_NA_Q
KV_OFF = GATE_OFF + W_GATE
IN_COLS = KV_OFF + 2 * W_WIN_KV + 2 * W_NA_KV

CTX_ROWS = B * L
T = CTX_ROWS + B * S
N_GROUPS = 8
ATT_SCALE = HD ** -0.5

VMEM_LIMIT = 56 * 1024 * 1024


def _cparams(*sem):
    return pltpu.CompilerParams(dimension_semantics=sem, vmem_limit_bytes=VMEM_LIMIT)


def _group_of_row(row0):
    return jnp.where(row0 < CTX_ROWS, 0, 1 + (row0 - CTX_ROWS) // S)


def _sigmoid(z):
    return 1.0 / (1.0 + jnp.exp(-z))


def _dot_nt(a, b):
    return lax.dot_general(a, b, (((1,), (1,)), ((), ())), preferred_element_type=F32)


MOD_TN = 512


def _mods_kernel(c_ref, w_ref, b_ref, o_ref):
    cv = c_ref[...]
    a = (cv * _sigmoid(cv)).astype(BF16)
    w = w_ref[0].astype(BF16)
    o_ref[0] = jnp.dot(a, w, preferred_element_type=F32) + b_ref[0]


def _mods(cvec, w_ada, b_ada):
    n = 6 * D
    return pl.pallas_call(
        _mods_kernel,
        grid=(DEPTH, n // MOD_TN),
        in_specs=[
            pl.BlockSpec((N_GROUPS, D), lambda l, j: (0, 0)),
            pl.BlockSpec((1, D, MOD_TN), lambda l, j: (l, 0, j)),
            pl.BlockSpec((1, 1, MOD_TN), lambda l, j: (l, 0, j)),
        ],
        out_specs=pl.BlockSpec((1, N_GROUPS, MOD_TN), lambda l, j: (l, 0, j)),
        out_shape=jax.ShapeDtypeStruct((DEPTH, N_GROUPS, n), F32),
        compiler_params=_cparams("arbitrary", "arbitrary"),
        name="adaln_mods",
    )(cvec, w_ada, b_ada.reshape(DEPTH, 1, n))


NM_TM = 256


def _normmod_kernel(has_f, *refs):
    if has_f:
        x_ref, f_ref, g_ref, ng_ref, sc_ref, sh_ref, xo_ref, h_ref = refs
        x = x_ref[...] + g_ref[0, 0] * f_ref[...]
        xo_ref[...] = x
    else:
        x_ref, ng_ref, sc_ref, sh_ref, h_ref = refs
        x = x_ref[...]
    ms = jnp.mean(x * x, axis=-1, keepdims=True)
    y = x * lax.rsqrt(ms + EPS) * ng_ref[...]
    h = y * (1.0 + sc_ref[0, 0]) + sh_ref[0, 0]
    h_ref[...] = h.astype(BF16)


def _mod_spec(layer, k, tm):
    return pl.BlockSpec((1, 1, 1, D), lambda i: (layer, _group_of_row(i * tm), 0, k))


def _normmod(x, modt, layer, norm_g, sc_k, sh_k, f=None, g_k=None):
    tm = NM_TM
    row = pl.BlockSpec((tm, D), lambda i: (i, 0))
    has_f = f is not None
    in_specs = [row]
    args = [x]
    if has_f:
        in_specs += [row, _mod_spec(layer - 1, g_k, tm)]
        args += [f, modt]
    in_specs += [pl.BlockSpec((1, D), lambda i: (0, 0)), _mod_spec(layer, sc_k, tm), _mod_spec(layer, sh_k, tm)]
    args += [norm_g.reshape(1, D), modt, modt]
    h_shape = jax.ShapeDtypeStruct((T, D), BF16)
    if has_f:
        out_specs = [row, row]
        out_shape = [jax.ShapeDtypeStruct((T, D), F32), h_shape]
    else:
        out_specs = row
        out_shape = h_shape
    return pl.pallas_call(
        functools.partial(_normmod_kernel, has_f),
        grid=(T // tm,),
        in_specs=in_specs,
        out_specs=out_specs,
        out_shape=out_shape,
        compiler_params=_cparams("arbitrary"),
        name="normmod_resid" if has_f else "normmod",
    )(*args)


MM_TM = 512


def _mm_kernel(a_ref, w_ref, o_ref):
    o_ref[...] = jnp.dot(a_ref[...], w_ref[...], preferred_element_type=F32).astype(o_ref.dtype)


def _mm_resid_kernel(a_ref, w_ref, r_ref, g_ref, o_ref):
    acc = jnp.dot(a_ref[...], w_ref[...], preferred_element_type=F32)
    o_ref[...] = r_ref[...] + g_ref[0, 0] * acc


def _matmul(a, w, out_dtype, tn):
    m, k = a.shape
    n = w.shape[1]
    return pl.pallas_call(
        _mm_kernel,
        grid=(m // MM_TM, n // tn),
        in_specs=[pl.BlockSpec((MM_TM, k), lambda i, j: (i, 0)),
                  pl.BlockSpec((k, tn), lambda i, j: (0, j))],
        out_specs=pl.BlockSpec((MM_TM, tn), lambda i, j: (i, j)),
        out_shape=jax.ShapeDtypeStruct((m, n), out_dtype),
        compiler_params=_cparams("arbitrary", "arbitrary"),
        name="matmul",
    )(a, w)


def _matmul_resid(a, w, resid, modt, layer, g_k, tn):
    m, k = a.shape
    n = w.shape[1]
    nj = n // tn
    return pl.pallas_call(
        _mm_resid_kernel,
        grid=(m // MM_TM, nj),
        in_specs=[pl.BlockSpec((MM_TM, k), lambda i, j: (i, 0)),
                  pl.BlockSpec((k, tn), lambda i, j: (0, j)),
                  pl.BlockSpec((MM_TM, tn), lambda i, j: (i, j)),
                  pl.BlockSpec((1, 1, 1, tn), lambda i, j: (layer, _group_of_row(i * MM_TM), 0, g_k * nj + j))],
        out_specs=pl.BlockSpec((MM_TM, tn), lambda i, j: (i, j)),
        out_shape=jax.ShapeDtypeStruct((m, n), F32),
        compiler_params=_cparams("arbitrary", "arbitrary"),
        name="matmul_resid",
    )(a, w, resid, modt)


CV_TM = 256
CV_HALO = 16
CV_RC = 64
CV_CC = 128


def _conv_kernel(prev_ref, cur_ref, next_ref, w_ref, cb_ref, lg_ref, lb_ref, o_ref, buf_ref, acc_ref):
    i = pl.program_id(0)
    row0 = i * CV_TM
    lat = row0 - CTX_ROWS
    seq_start = jnp.where(row0 < CTX_ROWS, row0 % L == 0, lat % S == 0)
    seq_end = jnp.where(row0 < CTX_ROWS, (row0 + CV_TM) % L == 0, (lat + CV_TM) % S == 0)

    def glu(u):
        return u[:, :CONV_CH] * _sigmoid(u[:, CONV_CH:])

    buf_ref[0:CV_HALO, :] = glu(prev_ref[...]) * jnp.where(seq_start, 0.0, 1.0)
    buf_ref[CV_HALO:CV_HALO + CV_TM, :] = glu(cur_ref[...])
    buf_ref[CV_HALO + CV_TM:, :] = glu(next_ref[...]) * jnp.where(seq_end, 0.0, 1.0)

    tap0 = CV_HALO - CONV_K // 2

    def col_chunk(c, carry):
        c0 = pl.multiple_of(c * CV_CC, CV_CC)
        for r in range(CV_TM // CV_RC):
            acc = jnp.zeros((CV_RC, CV_CC), F32)
            for k in range(CONV_K):
                r0 = r * CV_RC + k + tap0
                acc = acc + w_ref[k:k + 1, pl.ds(c0, CV_CC)] * buf_ref[r0:r0 + CV_RC, pl.ds(c0, CV_CC)]
            acc_ref[r * CV_RC:(r + 1) * CV_RC, pl.ds(c0, CV_CC)] = acc
        return carry

    lax.fori_loop(0, CONV_CH // CV_CC, col_chunk, 0)

    h = acc_ref[...] + cb_ref[...]
    mu = jnp.mean(h, axis=-1, keepdims=True)
    xc = h - mu
    var = jnp.mean(xc * xc, axis=-1, keepdims=True)
    y = xc * lax.rsqrt(var + EPS) * lg_ref[...] + lb_ref[...]
    o_ref[...] = (y * _sigmoid(y)).astype(BF16)


def _conv_module(u, conv_w, conv_b, ln_g, ln_b):
    hb = CV_TM // CV_HALO
    last = T // CV_HALO - 1
    vec = pl.BlockSpec((1, CONV_CH), lambda i: (0, 0))
    return pl.pallas_call(
        _conv_kernel,
        grid=(T // CV_TM,),
        in_specs=[
            pl.BlockSpec((CV_HALO, W_CONV_IN), lambda i: (jnp.maximum(i * hb - 1, 0), 0)),
            pl.BlockSpec((CV_TM, W_CONV_IN), lambda i: (i, 0)),
            pl.BlockSpec((CV_HALO, W_CONV_IN), lambda i: (jnp.minimum((i + 1) * hb, last), 0)),
            pl.BlockSpec((CONV_K, CONV_CH), lambda i: (0, 0)),
            vec, vec, vec,
        ],
        out_specs=pl.BlockSpec((CV_TM, CONV_CH), lambda i: (i, 0)),
        out_shape=jax.ShapeDtypeStruct((T, CONV_CH), BF16),
        scratch_shapes=[pltpu.VMEM((CV_TM + 2 * CV_HALO, CONV_CH), F32),
                        pltpu.VMEM((CV_TM, CONV_CH), F32)],
        compiler_params=_cparams("arbitrary"),
        name="conv_module",
    )(u, u, u, conv_w, conv_b.reshape(1, CONV_CH), ln_g.reshape(1, CONV_CH), ln_b.reshape(1, CONV_CH))


HN_TM = 256


def _headnorm_kernel(qw_ref, qn_ref, kvw_ref, kn_ref, vn_ref, cos_ref, sin_ref, g_ref,
                     qw_o, qn_o, kvw_o, kn_o, vn_o):
    cosf = cos_ref[...]
    sinf = sin_ref[...]
    lane = lax.broadcasted_iota(jnp.int32, (HN_TM, HD), 1)
    first_half = (lane % (HD // 2)) < (HD // 4)

    def norm(x, gi):
        ms = jnp.mean(x * x, axis=-1, keepdims=True)
        return x * lax.rsqrt(ms + EPS) * g_ref[gi:gi + 1, :]

    def rope(x):
        partner = jnp.where(first_half, pltpu.roll(x, HD - HD // 4, 1), pltpu.roll(x, HD // 4, 1))
        return x * cosf + partner * sinf

    for hd in range(WIN_HQ):
        sl = slice(hd * HD, (hd + 1) * HD)
        qw_o[:, sl] = rope(norm(qw_ref[:, sl], 0)).astype(BF16)
    for hd in range(WIN_HKV):
        sl = slice(hd * HD, (hd + 1) * HD)
        kvw_o[:, sl] = rope(norm(kvw_ref[:, sl], 1)).astype(BF16)
    kvw_o[:, W_WIN_KV:] = kvw_ref[:, W_WIN_KV:].astype(BF16)
    for hd in range(NA_H):
        sl = slice(hd * HD, (hd + 1) * HD)
        qn_o[:, sl] = norm(qn_ref[:, sl], 2).astype(BF16)
        kn_o[:, sl] = norm(kn_ref[:, sl], 3).astype(BF16)
    vn_o[...] = vn_ref[...].astype(BF16)


def _headnorm(u, cosf, sinf, gains):
    tm = HN_TM
    kvb = KV_OFF // W_NA_KV
    row128 = pl.BlockSpec((tm, HD), lambda i: (i, 0))

    def ospec(w):
        return pl.BlockSpec((tm, w), lambda i: (i, 0))

    def oshape(w):
        return jax.ShapeDtypeStruct((T, w), BF16)

    return pl.pallas_call(
        _headnorm_kernel,
        grid=(T // tm,),
        in_specs=[
            pl.BlockSpec((tm, W_WIN_Q), lambda i: (i, Q_WIN_OFF // W_WIN_Q)),
            pl.BlockSpec((tm, W_NA_Q), lambda i: (i, Q_NA_OFF // W_NA_Q)),
            pl.BlockSpec((tm, 2 * W_WIN_KV), lambda i: (i, kvb)),
            pl.BlockSpec((tm, W_NA_KV), lambda i: (i, kvb + 1)),
            pl.BlockSpec((tm, W_NA_KV), lambda i: (i, kvb + 2)),
            row128, row128,
            pl.BlockSpec((8, HD), lambda i: (0, 0)),
        ],
        out_specs=[ospec(W_WIN_Q), ospec(W_NA_Q), ospec(2 * W_WIN_KV), ospec(W_NA_KV), ospec(W_NA_KV)],
        out_shape=[oshape(W_WIN_Q), oshape(W_NA_Q), oshape(2 * W_WIN_KV), oshape(W_NA_KV), oshape(W_NA_KV)],
        compiler_params=_cparams("arbitrary"),
        name="headnorm_rope",
    )(u, u, u, u, u, cosf, sinf, gains)


def _rope_tables():
    t = jnp.arange(S)
    row = (t // GRID_W).astype(F32)
    col = (t % GRID_W).astype(F32)
    axis_dim = HD // 2
    inv = ROPE_BASE ** (-jnp.arange(0, axis_dim, 2, dtype=F32) / axis_dim)
    ar = row[:, None] * inv
    ac = col[:, None] * inv
    cosl = jnp.concatenate([jnp.cos(ar), jnp.cos(ar), jnp.cos(ac), jnp.cos(ac)], axis=1)
    sinl = jnp.concatenate([-jnp.sin(ar), jnp.sin(ar), -jnp.sin(ac), jnp.sin(ac)], axis=1)
    cosf = jnp.concatenate([jnp.ones((CTX_ROWS, HD), F32), cosl, cosl], axis=0)
    sinf = jnp.concatenate([jnp.zeros((CTX_ROWS, HD), F32), sinl, sinl], axis=0)
    return cosf, sinf


WA_TQ = 128
WA_CTX_TILES = CTX_ROWS // WA_TQ
WA_LAT_TILES = S // WA_TQ


def _win_attn_kernel(sink_ref, q_ref, kp_ref, kc_ref, kn_ref, kx_ref, o_ref):
    i = pl.program_id(0)
    is_ctx = i < WA_CTX_TILES
    n = (i - WA_CTX_TILES) % WA_LAT_TILES
    far = 4 * WA_TQ
    off_prev = jnp.where(jnp.logical_or(is_ctx, n == 0), far, 0)
    off_cur = jnp.where(is_ctx, far, 0)
    off_next = jnp.where(jnp.logical_or(is_ctx, n == WA_LAT_TILES - 1), far, 0)
    rows = WIN_G * WA_TQ
    r = lax.broadcasted_iota(jnp.int32, (rows, WA_TQ), 0) % WA_TQ
    c = lax.broadcasted_iota(jnp.int32, (rows, WA_TQ), 1)
    m_prev = c >= r + off_prev
    m_cur = c >= off_cur
    m_next = c + off_next <= r
    grp = lax.broadcasted_iota(jnp.int32, (rows, 1), 0) // WA_TQ

    for h in range(WIN_HKV):
        q = jnp.concatenate([q_ref[:, (h * WIN_G + g) * HD:(h * WIN_G + g + 1) * HD] for g in range(WIN_G)], axis=0)
        ks = slice(h * HD, (h + 1) * HD)
        vs = slice(W_WIN_KV + h * HD, W_WIN_KV + (h + 1) * HD)
        s_p = jnp.where(m_prev, _dot_nt(q, kp_ref[:, ks]) * ATT_SCALE, NEG)
        s_c = jnp.where(m_cur, _dot_nt(q, kc_ref[:, ks]) * ATT_SCALE, NEG)
        s_n = jnp.where(m_next, _dot_nt(q, kn_ref[:, ks]) * ATT_SCALE, NEG)
        s_x = _dot_nt(q, kx_ref[:, ks]) * ATT_SCALE
        snk = jnp.zeros((rows, 1), F32)
        for g in range(WIN_G):
            snk = jnp.where(grp == g, sink_ref[h * WIN_G + g], snk)
        m = jnp.maximum(jnp.maximum(jnp.max(s_p, axis=-1, keepdims=True), jnp.max(s_c, axis=-1, keepdims=True)),
                        jnp.maximum(jnp.max(s_n, axis=-1, keepdims=True), jnp.max(s_x, axis=-1, keepdims=True)))
        m = jnp.maximum(m, snk)
        p_p = jnp.exp(s_p - m)
        p_c = jnp.exp(s_c - m)
        p_n = jnp.exp(s_n - m)
        p_x = jnp.exp(s_x - m)
        den = (jnp.sum(p_p, axis=-1, keepdims=True) + jnp.sum(p_c, axis=-1, keepdims=True)
               + jnp.sum(p_n, axis=-1, keepdims=True) + jnp.sum(p_x, axis=-1, keepdims=True) + jnp.exp(snk - m))
        o = (jnp.dot(p_p.astype(BF16), kp_ref[:, vs], preferred_element_type=F32)
             + jnp.dot(p_c.astype(BF16), kc_ref[:, vs], preferred_element_type=F32)
             + jnp.dot(p_n.astype(BF16), kn_ref[:, vs], preferred_element_type=F32)
             + jnp.dot(p_x.astype(BF16), kx_ref[:, vs], preferred_element_type=F32))
        o = o / den
        for g in range(WIN_G):
            o_ref[:, (h * WIN_G + g) * HD:(h * WIN_G + g + 1) * HD] = o[g * WA_TQ:(g + 1) * WA_TQ].astype(BF16)


def _win_attention(qw, kvw, sink):
    nt = T // WA_TQ

    def bounds(i):
        is_ctx = i < WA_CTX_TILES
        b = (i - WA_CTX_TILES) // WA_LAT_TILES
        lo = jnp.where(is_ctx, 0, WA_CTX_TILES + b * WA_LAT_TILES)
        hi = jnp.where(is_ctx, nt - 1, WA_CTX_TILES + (b + 1) * WA_LAT_TILES - 1)
        return lo, hi

    def prev_map(i):
        lo, _ = bounds(i)
        return (jnp.maximum(i - 1, lo), 0)

    def next_map(i):
        _, hi = bounds(i)
        return (jnp.minimum(i + 1, hi), 0)

    def ctx_map(i):
        b = jnp.where(i < WA_CTX_TILES, i // (L // WA_TQ), (i - WA_CTX_TILES) // WA_LAT_TILES)
        return (b, 0)

    kvw_w = 2 * W_WIN_KV
    return pl.pallas_call(
        _win_attn_kernel,
        grid=(nt,),
        in_specs=[
            pl.BlockSpec(memory_space=pltpu.SMEM),
            pl.BlockSpec((WA_TQ, W_WIN_Q), lambda i: (i, 0)),
            pl.BlockSpec((WA_TQ, kvw_w), prev_map),
            pl.BlockSpec((WA_TQ, kvw_w), lambda i: (i, 0)),
            pl.BlockSpec((WA_TQ, kvw_w), next_map),
            pl.BlockSpec((L, kvw_w), ctx_map),
        ],
        out_specs=pl.BlockSpec((WA_TQ, W_WIN_Q), lambda i: (i, 0)),
        out_shape=jax.ShapeDtypeStruct((T, W_WIN_Q), BF16),
        compiler_params=_cparams("arbitrary"),
        name="window_attention",
    )(sink, qw, kvw, kvw, kvw, kvw)


NA_TQ = 256
NA_QROWS = NA_TQ // GRID_W
NA_KROWS = 3 * NA_QROWS
NA_NKEY = NA_KROWS * GRID_W
NA_TYPES = 4
RPB_R = 2 * NA_KH - 1
RPB_C = 2 * NA_KW - 1


def _na_row_valid(ty, a, j):
    if ty == 0:
        return NA_QROWS <= j < NA_QROWS + NA_KH
    if ty == 1:
        return a <= j < a + NA_KH
    if ty == 2:
        return j < NA_KH
    return False


def _rpb_kernel(rpb_ref, o_ref):
    h = pl.program_id(0)
    shp = (GRID_W, 2 * GRID_W)
    qc = lax.broadcasted_iota(jnp.int32, shp, 0)
    lane = lax.broadcasted_iota(jnp.int32, shp, 1)
    kc = lane % GRID_W
    second = lane >= GRID_W
    dcol = kc - qc + (NA_KW - 1)
    cs = jnp.clip(qc - NA_KW // 2, 0, GRID_W - NA_KW)
    colmask = jnp.logical_and(kc >= cs, kc < cs + NA_KW)
    neg = jnp.full(shp, NEG, F32)
    base = h * (RPB_R * RPB_C)
    pair = []
    for dr in range(RPB_R - 1):
        acc = jnp.zeros(shp, F32)
        for dd in range(RPB_C):
            v0 = rpb_ref[base + dr * RPB_C + dd]
            v1 = rpb_ref[base + (dr + 1) * RPB_C + dd]
            acc = jnp.where(dcol == dd, jnp.where(second, v1, v0), acc)
        pair.append(jnp.where(colmask, acc, neg))
    for ty in range(NA_TYPES):
        for a in range(NA_QROWS):
            for jp in range(NA_KROWS // 2):
                j = 2 * jp
                ok0 = _na_row_valid(ty, a, j)
                ok1 = _na_row_valid(ty, a, j + 1)
                dr = j - a + NA_QROWS - 1
                if ok0 and ok1:
                    tile = pair[dr]
                elif ok0:
                    tile = jnp.where(second, neg, pair[dr])
                elif ok1:
                    tile = jnp.where(second, pair[dr], neg)
                else:
                    tile = neg
                o_ref[ty, 0, a * GRID_W:(a + 1) * GRID_W, jp * 2 * GRID_W:(jp + 1) * 2 * GRID_W] = tile


def _rpb_tiles(rpb):
    return pl.pallas_call(
        _rpb_kernel,
        grid=(NA_H,),
        in_specs=[pl.BlockSpec(memory_space=pltpu.SMEM)],
        out_specs=pl.BlockSpec((NA_TYPES, 1, NA_TQ, NA_NKEY), lambda h: (0, h, 0, 0)),
        out_shape=jax.ShapeDtypeStruct((NA_TYPES, NA_H, NA_TQ, NA_NKEY), F32),
        compiler_params=_cparams("arbitrary"),
        name="rpb_tiles",
    )(rpb.reshape(-1))


NA_CTX_TILES = CTX_ROWS // NA_TQ
NA_LAT_TILES = S // NA_TQ


def _na_attn_kernel(q_ref, kp_ref, kc_ref, kn_ref, vp_ref, vc_ref, vn_ref, kx_ref, vx_ref, bias_ref, o_ref):
    for h in range(NA_H):
        hs = slice(h * HD, (h + 1) * HD)
        q = q_ref[:, hs]
        s_p = _dot_nt(q, kp_ref[:, hs]) * ATT_SCALE + bias_ref[0, h, :, 0:NA_TQ]
        s_c = _dot_nt(q, kc_ref[:, hs]) * ATT_SCALE + bias_ref[0, h, :, NA_TQ:2 * NA_TQ]
        s_n = _dot_nt(q, kn_ref[:, hs]) * ATT_SCALE + bias_ref[0, h, :, 2 * NA_TQ:3 * NA_TQ]
        s_x = _dot_nt(q, kx_ref[:, hs]) * ATT_SCALE
        m = jnp.maximum(jnp.maximum(jnp.max(s_p, axis=-1, keepdims=True), jnp.max(s_c, axis=-1, keepdims=True)),
                        jnp.maximum(jnp.max(s_n, axis=-1, keepdims=True), jnp.max(s_x, axis=-1, keepdims=True)))
        p_p = jnp.exp(s_p - m)
        p_c = jnp.exp(s_c - m)
        p_n = jnp.exp(s_n - m)
        p_x = jnp.exp(s_x - m)
        den = (jnp.sum(p_p, axis=-1, keepdims=True) + jnp.sum(p_c, axis=-1, keepdims=True)
               + jnp.sum(p_n, axis=-1, keepdims=True) + jnp.sum(p_x, axis=-1, keepdims=True))
        o = (jnp.dot(p_p.astype(BF16), vp_ref[:, hs], preferred_element_type=F32)
             + jnp.dot(p_c.astype(BF16), vc_ref[:, hs], preferred_element_type=F32)
             + jnp.dot(p_n.astype(BF16), vn_ref[:, hs], preferred_element_type=F32)
             + jnp.dot(p_x.astype(BF16), vx_ref[:, hs], preferred_element_type=F32))
        o_ref[:, hs] = (o / den).astype(BF16)


def _na_attention(qn, kn, vn, bias):
    nt = T // NA_TQ

    def bounds(i):
        is_ctx = i < NA_CTX_TILES
        b = (i - NA_CTX_TILES) // NA_LAT_TILES
        lo = jnp.where(is_ctx, 0, NA_CTX_TILES + b * NA_LAT_TILES)
        hi = jnp.where(is_ctx, nt - 1, NA_CTX_TILES + (b + 1) * NA_LAT_TILES - 1)
        return lo, hi

    def prev_map(i):
        lo, _ = bounds(i)
        return (jnp.maximum(i - 1, lo), 0)

    def next_map(i):
        _, hi = bounds(i)
        return (jnp.minimum(i + 1, hi), 0)

    def ctx_map(i):
        return (jnp.where(i < NA_CTX_TILES, i, (i - NA_CTX_TILES) // NA_LAT_TILES), 0)

    def bias_map(i):
        n = (i - NA_CTX_TILES) % NA_LAT_TILES
        ty = jnp.where(i < NA_CTX_TILES, 3, jnp.where(n == 0, 0, jnp.where(n == NA_LAT_TILES - 1, 2, 1)))
        return (ty, 0, 0, 0)

    w = W_NA_KV
    cur = pl.BlockSpec((NA_TQ, w), lambda i: (i, 0))
    prv = pl.BlockSpec((NA_TQ, w), prev_map)
    nxt = pl.BlockSpec((NA_TQ, w), next_map)
    ctx = pl.BlockSpec((L, w), ctx_map)
    return pl.pallas_call(
        _na_attn_kernel,
        grid=(nt,),
        in_specs=[cur, prv, cur, nxt, prv, cur, nxt, ctx, ctx,
                  pl.BlockSpec((1, NA_H, NA_TQ, NA_NKEY), bias_map)],
        out_specs=cur,
        out_shape=jax.ShapeDtypeStruct((T, W_NA_Q), BF16),
        compiler_params=_cparams("arbitrary"),
        name="neighborhood_attention",
    )(qn, kn, kn, kn, vn, vn, vn, kn, vn, bias)


MG_TM = 512
MG_TN = 512


def _merge_kernel(ca_ref, aw_ref, an_ref, wc_ref, ww_ref, wn_ref, ga_ref, gb_ref, gc_ref, o_ref):
    ya = jnp.dot(ca_ref[...], wc_ref[...], preferred_element_type=F32)
    yb = jnp.dot(aw_ref[...], ww_ref[...], preferred_element_type=F32)
    yc = jnp.dot(an_ref[...], wn_ref[...], preferred_element_type=F32)
    o = _sigmoid(ga_ref[...]) * ya + _sigmoid(gb_ref[...]) * yb + _sigmoid(gc_ref[...]) * yc
    o_ref[...] = o.astype(BF16)


def _merge(hconv, aw, an, w_conv_out, w_win_out, w_na_out, u):
    gb0 = GATE_OFF // MG_TN
    gstep = D // MG_TN

    def a_spec(k):
        return pl.BlockSpec((MG_TM, k), lambda i, j: (i, 0))

    def w_spec(k):
        return pl.BlockSpec((k, MG_TN), lambda i, j: (0, j))

    def g_spec(which):
        return pl.BlockSpec((MG_TM, MG_TN), lambda i, j: (i, gb0 + which * gstep + j))

    return pl.pallas_call(
        _merge_kernel,
        grid=(T // MG_TM, D // MG_TN),
        in_specs=[a_spec(CONV_CH), a_spec(W_WIN_Q), a_spec(W_NA_Q),
                  w_spec(CONV_CH), w_spec(W_WIN_Q), w_spec(W_NA_Q),
                  g_spec(0), g_spec(1), g_spec(2)],
        out_specs=pl.BlockSpec((MG_TM, MG_TN), lambda i, j: (i, j)),
        out_shape=jax.ShapeDtypeStruct((T, D), BF16),
        compiler_params=_cparams("arbitrary", "arbitrary"),
        name="gated_merge",
    )(hconv, aw, an, w_conv_out, w_win_out, w_na_out, u, u, u)


TK_TT = 256


def _topk_rounds(scores, n_rounds):
    nrow = scores.shape[0]
    idx = lax.broadcasted_iota(jnp.int32, scores.shape, 0)
    work = scores
    rank = jnp.full(scores.shape, n_rounds, jnp.int32)
    vals = []
    for a in range(n_rounds):
        m = jnp.max(work, axis=0, keepdims=True)
        first = jnp.min(jnp.where(work == m, idx, nrow), axis=0, keepdims=True)
        sel = idx == first
        rank = jnp.where(sel, a, rank)
        work = jnp.where(sel, -jnp.inf, work)
        vals.append(m)
    return vals, rank


def _peer_topk_kernel(q_ref, k1_ref, k2_ref, n_ref, e1_ref, r2_ref, e2_ref):
    kk = PEER_TOPK

    def head(h, carry):
        c1 = pl.multiple_of(h * 2 * HD, 2 * HD)
        q1 = q_ref[:, pl.ds(c1, HD)]
        q2 = q_ref[:, pl.ds(c1 + HD, HD)]
        s1 = _dot_nt(k1_ref[h].astype(BF16), q1)
        s2 = _dot_nt(k2_ref[h].astype(BF16), q2)
        v1, rank1 = _topk_rounds(s1, kk)
        v2, rank2 = _topk_rounds(s2, kk)
        v2all = jnp.concatenate(v2, axis=0)
        cand = jnp.concatenate([v1[a] + v2all for a in range(kk)], axis=0)
        pos = lax.broadcasted_iota(jnp.int32, cand.shape, 0)
        arow = lax.broadcasted_iota(jnp.int32, v2all.shape, 0)
        cnt = jnp.zeros(v2all.shape, F32)
        top = v1[0] + v2[0]
        z = jnp.zeros_like(top)
        for _ in range(kk):
            m = jnp.max(cand, axis=0, keepdims=True)
            first = jnp.min(jnp.where(cand == m, pos, kk * kk), axis=0, keepdims=True)
            cand = jnp.where(pos == first, -jnp.inf, cand)
            cnt = cnt + jnp.where(arow == first // kk, 1.0, 0.0)
            z = z + jnp.exp(m - top)
        nfull = jnp.zeros(s1.shape, F32)
        for a in range(kk):
            nfull = jnp.where(rank1 == a, cnt[a:a + 1, :], nfull)
        n_ref[h] = nfull
        e1_ref[h] = jnp.exp(s1 - v1[0]) / z
        r2_ref[h] = rank2.astype(F32)
        e2_ref[h] = jnp.exp(s2 - v2[0])
        return carry

    lax.fori_loop(0, PEER_HEADS, head, 0)


def _peer_topk(q, k1, k2):
    tt = TK_TT
    kspec = pl.BlockSpec((PEER_HEADS, N_KEYS, HD), lambda i: (0, 0, 0))
    ospec = pl.BlockSpec((PEER_HEADS, N_KEYS, tt), lambda i: (0, 0, i))
    oshape = jax.ShapeDtypeStruct((PEER_HEADS, N_KEYS, T), F32)
    return pl.pallas_call(
        _peer_topk_kernel,
        grid=(T // tt,),
        in_specs=[pl.BlockSpec((tt, 2 * HD * PEER_HEADS), lambda i: (i, 0)), kspec, kspec],
        out_specs=[ospec] * 4,
        out_shape=[oshape] * 4,
        compiler_params=_cparams("arbitrary"),
        name="peer_topk",
    )(q, k1, k2)


EX_TM = 512
EX_TN = 256
EX_LC = 128


def _gelu_tanh(x):
    cdf = 0.5 * (1.0 + jnp.tanh(math.sqrt(2.0 / math.pi) * (x + 0.044715 * (x * x * x))))
    return x * cdf


def _experts_kernel(h_ref, u_ref, v_ref, n_ref, e1_ref, r2_ref, e2_ref, o_ref, act_ref, w_ref):
    j = pl.program_id(1)

    @pl.when(j == 0)
    def _():
        o_ref[...] = jnp.zeros_like(o_ref)

    act_ref[...] = _gelu_tanh(_dot_nt(u_ref[...], h_ref[...]))
    for ii in range(EX_TN // N_KEYS):
        i1 = j * (EX_TN // N_KEYS) + ii
        rs = slice(ii * N_KEYS, (ii + 1) * N_KEYS)
        for c in range(EX_TM // EX_LC):
            cs = slice(c * EX_LC, (c + 1) * EX_LC)
            gate = jnp.zeros((N_KEYS, EX_LC), F32)
            for hh in range(PEER_HEADS):
                nrow = n_ref[i1, hh:hh + 1, cs]
                e1row = e1_ref[i1, hh:hh + 1, cs]
                gate = gate + jnp.where(r2_ref[hh, :, cs] < nrow, e2_ref[hh, :, cs], 0.0) * e1row
            w_ref[rs, cs] = (gate * act_ref[rs, cs]).astype(BF16)
    o_ref[...] += lax.dot_general(w_ref[...], v_ref[...], (((0,), (0,)), ((), ())),
                                  preferred_element_type=F32)


def _experts(h, u, v, tables):
    n, e1, r2, e2 = tables
    n = jnp.transpose(n, (1, 0, 2))
    e1 = jnp.transpose(e1, (1, 0, 2))
    kspec = pl.BlockSpec((N_KEYS, PEER_HEADS, EX_TM), lambda i, j: (0, 0, i))
    tspec = pl.BlockSpec((PEER_HEADS, N_KEYS, EX_TM), lambda i, j: (0, 0, i))
    return pl.pallas_call(
        _experts_kernel,
        grid=(T // EX_TM, N_EXPERTS // EX_TN),
        in_specs=[pl.BlockSpec((EX_TM, D), lambda i, j: (i, 0)),
                  pl.BlockSpec((EX_TN, D), lambda i, j: (j, 0)),
                  pl.BlockSpec((EX_TN, D), lambda i, j: (j, 0)),
                  kspec, kspec, tspec, tspec],
        out_specs=pl.BlockSpec((EX_TM, D), lambda i, j: (i, 0)),
        out_shape=jax.ShapeDtypeStruct((T, D), F32),
        scratch_shapes=[pltpu.VMEM((EX_TN, EX_TM), F32), pltpu.VMEM((EX_TN, EX_TM), BF16)],
        compiler_params=_cparams("arbitrary", "arbitrary"),
        name="peer_experts",
    )(h, u, v, n, e1, r2, e2)


FR_TM = 256


def _final_kernel(x_ref, f_ref, g_ref, o_ref):
    o_ref[...] = x_ref[...] + g_ref[0, 0] * f_ref[...]


def _final_residual(x, f, modt, layer, g_k):
    off = CTX_ROWS // FR_TM
    lat = pl.BlockSpec((FR_TM, D), lambda i: (i + off, 0))
    return pl.pallas_call(
        _final_kernel,
        grid=(B * S // FR_TM,),
        in_specs=[lat, lat,
                  pl.BlockSpec((1, 1, 1, D), lambda i: (layer, _group_of_row((i + off) * FR_TM), 0, g_k))],
        out_specs=pl.BlockSpec((FR_TM, D), lambda i: (i, 0)),
        out_shape=jax.ShapeDtypeStruct((B * S, D), F32),
        compiler_params=_cparams("arbitrary"),
        name="final_residual",
    )(x, f, modt)


SH1, SC1, G1, SH2, SC2, G2 = range(6)


def kernel(x, c, ctx, c_ctx, w_ada, b_ada, norm1_g, norm2_g, w_in, conv_w, conv_b, conv_ln_g, conv_ln_b,
           w_conv_out, win_qn_g, win_kn_g, win_sink, w_win_out, na_qn_g, na_kn_g, na_rpb, w_na_out, w_out,
           peer_wq, peer_k1, peer_k2, peer_u, peer_v):
    xs = jnp.concatenate([ctx.reshape(CTX_ROWS, D), x.reshape(B * S, D)], axis=0)
    cvec = jnp.concatenate([c_ctx[None], c, jnp.zeros((N_GROUPS - 1 - B, D), F32)], axis=0)
    modt = _mods(cvec, w_ada, b_ada).reshape(DEPTH, N_GROUPS, 1, 6 * D)
    cosf, sinf = _rope_tables()

    f = None
    for l in range(DEPTH):
        if l == 0:
            h1 = _normmod(xs, modt, l, norm1_g[l], SC1, SH1)
        else:
            xs, h1 = _normmod(xs, modt, l, norm1_g[l], SC1, SH1, f=f, g_k=G2)
        u = _matmul(h1, w_in[l].astype(BF16), F32, 1024)
        hconv = _conv_module(u, conv_w[l], conv_b[l], conv_ln_g[l], conv_ln_b[l])
        gains = jnp.concatenate([win_qn_g[l][None], win_kn_g[l][None], na_qn_g[l][None], na_kn_g[l][None],
                                 jnp.zeros((4, HD), F32)], axis=0)
        qw, qn, kvw, kn, vn = _headnorm(u, cosf, sinf, gains)
        aw = _win_attention(qw, kvw, win_sink[l])
        an = _na_attention(qn, kn, vn, _rpb_tiles(na_rpb[l]))
        merged = _merge(hconv, aw, an, w_conv_out[l].astype(BF16), w_win_out[l].astype(BF16),
                        w_na_out[l].astype(BF16), u)
        xs = _matmul_resid(merged, w_out[l].astype(BF16), xs, modt, l, G1, 1024)
        h2 = _normmod(xs, modt, l, norm2_g[l], SC2, SH2)
        q = _matmul(h2, peer_wq[l].astype(BF16), BF16, 1024)
        tables = _peer_topk(q, peer_k1[l], peer_k2[l])
        f = _experts(h2, peer_u[l].astype(BF16), peer_v[l].astype(BF16), tables)
    out = _final_residual(xs, f, modt, DEPTH - 1, G2)
    return out.reshape(B, S, D)
```

```python
import functools
import math

import jax
import jax.numpy as jnp
import numpy as np
from jax import lax
from jax.experimental import pallas as pl
from jax.experimental.pallas import tpu as pltpu

F32 = jnp.float32
BF16 = jnp.bfloat16

D = 4096
B = 2
S = 4096
L = 256
DEPTH = 2
GRID_W = 64
HD = 128
EPS = 1e-6
NEG = -1e30
ROPE_BASE = 10000.0

CONV_CH = D // 4
CONV_K = 31
WIN_HQ = 16
WIN_HKV = 4
WIN_G = WIN_HQ // WIN_HKV
NA_H = 8
NA_KH = 8
NA_KW = 16
PEER_HEADS = 8
N_KEYS = 128
N_EXPERTS = N_KEYS * N_KEYS
PEER_TOPK = 16

W_CONV_IN = 2 * CONV_CH
W_WIN_Q = WIN_HQ * HD
W_NA_Q = NA_H * HD
W_GATE = 3 * D
W_WIN_KV = WIN_HKV * HD
W_NA_KV = NA_H * HD
Q_WIN_OFF = W_CONV_IN
Q_NA_OFF = Q_WIN_OFF + W_WIN_Q
GATE_OFF = Q_NA_OFF + W_NA_Q
KV_OFF = GATE_OFF + W_GATE
IN_COLS = KV_OFF + 2 * W_WIN_KV + 2 * W_NA_KV

CTX_ROWS = B * L
T = CTX_ROWS + B * S
N_GROUPS = 8
ATT_SCALE = HD ** -0.5

VMEM_LIMIT = 56 * 1024 * 1024


def _cparams(*sem):
    return pltpu.CompilerParams(dimension_semantics=sem, vmem_limit_bytes=VMEM_LIMIT)


def _group_of_row(row0):
    return jnp.where(row0 < CTX_ROWS, 0, 1 + (row0 - CTX_ROWS) // S)


def _sigmoid(z):
    return 1.0 / (1.0 + jnp.exp(-z))


def _dot_nt(a, b):
    return lax.dot_general(a, b, (((1,), (1,)), ((), ())), preferred_element_type=F32)


MOD_TN = 512


def _mods_kernel(c_ref, w_ref, b_ref, o_ref):
    cv = c_ref[...]
    a = (cv * _sigmoid(cv)).astype(BF16)
    w = w_ref[0].astype(BF16)
    o_ref[0] = jnp.dot(a, w, preferred_element_type=F32) + b_ref[0]


def _mods(cvec, w_ada, b_ada):
    n = 6 * D
    return pl.pallas_call(
        _mods_kernel,
        grid=(DEPTH, n // MOD_TN),
        in_specs=[
            pl.BlockSpec((N_GROUPS, D), lambda l, j: (0, 0)),
            pl.BlockSpec((1, D, MOD_TN), lambda l, j: (l, 0, j)),
            pl.BlockSpec((1, 1, MOD_TN), lambda l, j: (l, 0, j)),
        ],
        out_specs=pl.BlockSpec((1, N_GROUPS, MOD_TN), lambda l, j: (l, 0, j)),
        out_shape=jax.ShapeDtypeStruct((DEPTH, N_GROUPS, n), F32),
        compiler_params=_cparams("arbitrary", "arbitrary"),
        name="adaln_mods",
    )(cvec, w_ada, b_ada.reshape(DEPTH, 1, n))


NM_TM = 256


def _normmod_kernel(has_f, want_t, *refs):
    refs = list(refs)
    ht_ref = refs.pop() if want_t else None
    if has_f:
        x_ref, f_ref, g_ref, ng_ref, sc_ref, sh_ref, xo_ref, h_ref = refs
        x = x_ref[...] + g_ref[0, 0] * f_ref[...]
        xo_ref[...] = x
    else:
        x_ref, ng_ref, sc_ref, sh_ref, h_ref = refs
        x = x_ref[...]
    ms = jnp.mean(x * x, axis=-1, keepdims=True)
    y = x * lax.rsqrt(ms + EPS) * ng_ref[...]
    h = y * (1.0 + sc_ref[0, 0]) + sh_ref[0, 0]
    h_ref[...] = h.astype(BF16)
    if want_t:
        ht_ref[...] = h.T.astype(BF16)


def _mod_spec(layer, k, tm):
    return pl.BlockSpec((1, 1, 1, D), lambda i: (layer, _group_of_row(i * tm), 0, k))


def _normmod(x, modt, layer, norm_g, sc_k, sh_k, f=None, g_k=None, want_t=False):
    tm = NM_TM
    row = pl.BlockSpec((tm, D), lambda i: (i, 0))
    has_f = f is not None
    in_specs = [row]
    args = [x]
    if has_f:
        in_specs += [row, _mod_spec(layer - 1, g_k, tm)]
        args += [f, modt]
    in_specs += [pl.BlockSpec((1, D), lambda i: (0, 0)), _mod_spec(layer, sc_k, tm), _mod_spec(layer, sh_k, tm)]
    args += [norm_g.reshape(1, D), modt, modt]
    h_shape = jax.ShapeDtypeStruct((T, D), BF16)
    if has_f:
        out_specs = [row, row]
        out_shape = [jax.ShapeDtypeStruct((T, D), F32), h_shape]
    else:
        out_specs = [row]
        out_shape = [h_shape]
    if want_t:
        out_specs = out_specs + [pl.BlockSpec((D, tm), lambda i: (0, i))]
        out_shape = out_shape + [jax.ShapeDtypeStruct((D, T), BF16)]
    outs = pl.pallas_call(
        functools.partial(_normmod_kernel, has_f, want_t),
        grid=(T // tm,),
        in_specs=in_specs,
        out_specs=out_specs,
        out_shape=out_shape,
        compiler_params=_cparams("arbitrary"),
        name="normmod_resid" if has_f else "normmod",
    )(*args)
    return outs[0] if len(outs) == 1 else tuple(outs)


MM_TM = 512


def _mm_kernel(a_ref, w_ref, o_ref):
    o_ref[...] = jnp.dot(a_ref[...], w_ref[...], preferred_element_type=F32).astype(o_ref.dtype)


def _mm_resid_kernel(a_ref, w_ref, r_ref, g_ref, o_ref):
    acc = jnp.dot(a_ref[...], w_ref[...], preferred_element_type=F32)
    o_ref[...] = r_ref[...] + g_ref[0, 0] * acc


def _matmul(a, w, out_dtype, tn):
    m, k = a.shape
    n = w.shape[1]
    return pl.pallas_call(
        _mm_kernel,
        grid=(m // MM_TM, n // tn),
        in_specs=[pl.BlockSpec((MM_TM, k), lambda i, j: (i, 0)),
                  pl.BlockSpec((k, tn), lambda i, j: (0, j))],
        out_specs=pl.BlockSpec((MM_TM, tn), lambda i, j: (i, j)),
        out_shape=jax.ShapeDtypeStruct((m, n), out_dtype),
        compiler_params=_cparams("arbitrary", "arbitrary"),
        name="matmul",
    )(a, w)


def _matmul_resid(a, w, resid, modt, layer, g_k, tn):
    m, k = a.shape
    n = w.shape[1]
    nj = n // tn
    return pl.pallas_call(
        _mm_resid_kernel,
        grid=(m // MM_TM, nj),
        in_specs=[pl.BlockSpec((MM_TM, k), lambda i, j: (i, 0)),
                  pl.BlockSpec((k, tn), lambda i, j: (0, j)),
                  pl.BlockSpec((MM_TM, tn), lambda i, j: (i, j)),
                  pl.BlockSpec((1, 1, 1, tn), lambda i, j: (layer, _group_of_row(i * MM_TM), 0, g_k * nj + j))],
        out_specs=pl.BlockSpec((MM_TM, tn), lambda i, j: (i, j)),
        out_shape=jax.ShapeDtypeStruct((m, n), F32),
        compiler_params=_cparams("arbitrary", "arbitrary"),
        name="matmul_resid",
    )(a, w, resid, modt)


CV_TM = 256
CV_HALO = 16
CV_RC = 64
CV_CC = 128


def _conv_kernel(prev_ref, cur_ref, next_ref, w_ref, cb_ref, lg_ref, lb_ref, o_ref, buf_ref, acc_ref):
    i = pl.program_id(0)
    row0 = i * CV_TM
    lat = row0 - CTX_ROWS
    seq_start = jnp.where(row0 < CTX_ROWS, row0 % L == 0, lat % S == 0)
    seq_end = jnp.where(row0 < CTX_ROWS, (row0 + CV_TM) % L == 0, (lat + CV_TM) % S == 0)

    def glu(u):
        return u[:, :CONV_CH] * _sigmoid(u[:, CONV_CH:])

    buf_ref[0:CV_HALO, :] = glu(prev_ref[...]) * jnp.where(seq_start, 0.0, 1.0)
    buf_ref[CV_HALO:CV_HALO + CV_TM, :] = glu(cur_ref[...])
    buf_ref[CV_HALO + CV_TM:, :] = glu(next_ref[...]) * jnp.where(seq_end, 0.0, 1.0)

    tap0 = CV_HALO - CONV_K // 2

    def col_chunk(c, carry):
        c0 = pl.multiple_of(c * CV_CC, CV_CC)
        for r in range(CV_TM // CV_RC):
            acc = jnp.zeros((CV_RC, CV_CC), F32)
            for k in range(CONV_K):
                r0 = r * CV_RC + k + tap0
                acc = acc + w_ref[k:k + 1, pl.ds(c0, CV_CC)] * buf_ref[r0:r0 + CV_RC, pl.ds(c0, CV_CC)]
            acc_ref[r * CV_RC:(r + 1) * CV_RC, pl.ds(c0, CV_CC)] = acc
        return carry

    lax.fori_loop(0, CONV_CH // CV_CC, col_chunk, 0)

    h = acc_ref[...] + cb_ref[...]
    mu = jnp.mean(h, axis=-1, keepdims=True)
    xc = h - mu
    var = jnp.mean(xc * xc, axis=-1, keepdims=True)
    y = xc * lax.rsqrt(var + EPS) * lg_ref[...] + lb_ref[...]
    o_ref[...] = (y * _sigmoid(y)).astype(BF16)


def _conv_module(u, conv_w, conv_b, ln_g, ln_b):
    hb = CV_TM // CV_HALO
    last = T // CV_HALO - 1
    vec = pl.BlockSpec((1, CONV_CH), lambda i: (0, 0))
    return pl.pallas_call(
        _conv_kernel,
        grid=(T // CV_TM,),
        in_specs=[
            pl.BlockSpec((CV_HALO, W_CONV_IN), lambda i: (jnp.maximum(i * hb - 1, 0), 0)),
            pl.BlockSpec((CV_TM, W_CONV_IN), lambda i: (i, 0)),
            pl.BlockSpec((CV_HALO, W_CONV_IN), lambda i: (jnp.minimum((i + 1) * hb, last), 0)),
            pl.BlockSpec((CONV_K, CONV_CH), lambda i: (0, 0)),
            vec, vec, vec,
        ],
        out_specs=pl.BlockSpec((CV_TM, CONV_CH), lambda i: (i, 0)),
        out_shape=jax.ShapeDtypeStruct((T, CONV_CH), BF16),
        scratch_shapes=[pltpu.VMEM((CV_TM + 2 * CV_HALO, CONV_CH), F32),
                        pltpu.VMEM((CV_TM, CONV_CH), F32)],
        compiler_params=_cparams("arbitrary"),
        name="conv_module",
    )(u, u, u, conv_w, conv_b.reshape(1, CONV_CH), ln_g.reshape(1, CONV_CH), ln_b.reshape(1, CONV_CH))


HN_TM = 256


def _headnorm_kernel(qw_ref, qn_ref, kvw_ref, kn_ref, vn_ref, cos_ref, sin_ref, g_ref,
                     qw_o, qn_o, kvw_o, kn_o, vn_o):
    cosf = cos_ref[...]
    sinf = sin_ref[...]
    lane = lax.broadcasted_iota(jnp.int32, (HN_TM, HD), 1)
    first_half = (lane % (HD // 2)) < (HD // 4)

    def norm(x, gi):
        ms = jnp.mean(x * x, axis=-1, keepdims=True)
        return x * lax.rsqrt(ms + EPS) * g_ref[gi:gi + 1, :]

    def rope(x):
        partner = jnp.where(first_half, pltpu.roll(x, HD - HD // 4, 1), pltpu.roll(x, HD // 4, 1))
        return x * cosf + partner * sinf

    for hd in range(WIN_HQ):
        sl = slice(hd * HD, (hd + 1) * HD)
        qw_o[:, sl] = rope(norm(qw_ref[:, sl], 0)).astype(BF16)
    for hd in range(WIN_HKV):
        sl = slice(hd * HD, (hd + 1) * HD)
        kvw_o[:, sl] = rope(norm(kvw_ref[:, sl], 1)).astype(BF16)
    kvw_o[:, W_WIN_KV:] = kvw_ref[:, W_WIN_KV:].astype(BF16)
    for hd in range(NA_H):
        sl = slice(hd * HD, (hd + 1) * HD)
        qn_o[:, sl] = norm(qn_ref[:, sl], 2).astype(BF16)
        kn_o[:, sl] = norm(kn_ref[:, sl], 3).astype(BF16)
    vn_o[...] = vn_ref[...].astype(BF16)


def _headnorm(u, cosf, sinf, gains):
    tm = HN_TM
    kvb = KV_OFF // W_NA_KV
    row128 = pl.BlockSpec((tm, HD), lambda i: (i, 0))

    def ospec(w):
        return pl.BlockSpec((tm, w), lambda i: (i, 0))

    def oshape(w):
        return jax.ShapeDtypeStruct((T, w), BF16)

    return pl.pallas_call(
        _headnorm_kernel,
        grid=(T // tm,),
        in_specs=[
            pl.BlockSpec((tm, W_WIN_Q), lambda i: (i, Q_WIN_OFF // W_WIN_Q)),
            pl.BlockSpec((tm, W_NA_Q), lambda i: (i, Q_NA_OFF // W_NA_Q)),
            pl.BlockSpec((tm, 2 * W_WIN_KV), lambda i: (i, kvb)),
            pl.BlockSpec((tm, W_NA_KV), lambda i: (i, kvb + 1)),
            pl.BlockSpec((tm, W_NA_KV), lambda i: (i, kvb + 2)),
            row128, row128,
            pl.BlockSpec((8, HD), lambda i: (0, 0)),
        ],
        out_specs=[ospec(W_WIN_Q), ospec(W_NA_Q), ospec(2 * W_WIN_KV), ospec(W_NA_KV), ospec(W_NA_KV)],
        out_shape=[oshape(W_WIN_Q), oshape(W_NA_Q), oshape(2 * W_WIN_KV), oshape(W_NA_KV), oshape(W_NA_KV)],
        compiler_params=_cparams("arbitrary"),
        name="headnorm_rope",
    )(u, u, u, u, u, cosf, sinf, gains)


def _rope_tables():
    t = jnp.arange(S)
    row = (t // GRID_W).astype(F32)
    col = (t % GRID_W).astype(F32)
    axis_dim = HD // 2
    inv = ROPE_BASE ** (-jnp.arange(0, axis_dim, 2, dtype=F32) / axis_dim)
    ar = row[:, None] * inv
    ac = col[:, None] * inv
    cosl = jnp.concatenate([jnp.cos(ar), jnp.cos(ar), jnp.cos(ac), jnp.cos(ac)], axis=1)
    sinl = jnp.concatenate([-jnp.sin(ar), jnp.sin(ar), -jnp.sin(ac), jnp.sin(ac)], axis=1)
    cosf = jnp.concatenate([jnp.ones((CTX_ROWS, HD), F32), cosl, cosl], axis=0)
    sinf = jnp.concatenate([jnp.zeros((CTX_ROWS, HD), F32), sinl, sinl], axis=0)
    return cosf, sinf


WA_TQ = 128
WA_CTX_TILES = CTX_ROWS // WA_TQ
WA_LAT_TILES = S // WA_TQ


def _win_attn_kernel(sink_ref, q_ref, kp_ref, kc_ref, kn_ref, kx_ref, o_ref):
    i = pl.program_id(0)
    is_ctx = i < WA_CTX_TILES
    n = (i - WA_CTX_TILES) % WA_LAT_TILES
    far = 4 * WA_TQ
    off_prev = jnp.where(jnp.logical_or(is_ctx, n == 0), far, 0)
    off_cur = jnp.where(is_ctx, far, 0)
    off_next = jnp.where(jnp.logical_or(is_ctx, n == WA_LAT_TILES - 1), far, 0)
    rows = WIN_G * WA_TQ
    r = lax.broadcasted_iota(jnp.int32, (rows, WA_TQ), 0) % WA_TQ
    c = lax.broadcasted_iota(jnp.int32, (rows, WA_TQ), 1)
    m_prev = c >= r + off_prev
    m_cur = c >= off_cur
    m_next = c + off_next <= r
    grp = lax.broadcasted_iota(jnp.int32, (rows, 1), 0) // WA_TQ

    for h in range(WIN_HKV):
        q = jnp.concatenate([q_ref[:, (h * WIN_G + g) * HD:(h * WIN_G + g + 1) * HD] for g in range(WIN_G)], axis=0)
        ks = slice(h * HD, (h + 1) * HD)
        vs = slice(W_WIN_KV + h * HD, W_WIN_KV + (h + 1) * HD)
        s_p = jnp.where(m_prev, _dot_nt(q, kp_ref[:, ks]) * ATT_SCALE, NEG)
        s_c = jnp.where(m_cur, _dot_nt(q, kc_ref[:, ks]) * ATT_SCALE, NEG)
        s_n = jnp.where(m_next, _dot_nt(q, kn_ref[:, ks]) * ATT_SCALE, NEG)
        s_x = _dot_nt(q, kx_ref[:, ks]) * ATT_SCALE
        snk = jnp.zeros((rows, 1), F32)
        for g in range(WIN_G):
            snk = jnp.where(grp == g, sink_ref[h * WIN_G + g], snk)
        m = jnp.maximum(jnp.maximum(jnp.max(s_p, axis=-1, keepdims=True), jnp.max(s_c, axis=-1, keepdims=True)),
                        jnp.maximum(jnp.max(s_n, axis=-1, keepdims=True), jnp.max(s_x, axis=-1, keepdims=True)))
        m = jnp.maximum(m, snk)
        p_p = jnp.exp(s_p - m)
        p_c = jnp.exp(s_c - m)
        p_n = jnp.exp(s_n - m)
        p_x = jnp.exp(s_x - m)
        den = (jnp.sum(p_p, axis=-1, keepdims=True) + jnp.sum(p_c, axis=-1, keepdims=True)
               + jnp.sum(p_n, axis=-1, keepdims=True) + jnp.sum(p_x, axis=-1, keepdims=True) + jnp.exp(snk - m))
        o = (jnp.dot(p_p.astype(BF16), kp_ref[:, vs], preferred_element_type=F32)
             + jnp.dot(p_c.astype(BF16), kc_ref[:, vs], preferred_element_type=F32)
             + jnp.dot(p_n.astype(BF16), kn_ref[:, vs], preferred_element_type=F32)
             + jnp.dot(p_x.astype(BF16), kx_ref[:, vs], preferred_element_type=F32))
        o = o / den
        for g in range(WIN_G):
            o_ref[:, (h * WIN_G + g) * HD:(h * WIN_G + g + 1) * HD] = o[g * WA_TQ:(g + 1) * WA_TQ].astype(BF16)


def _win_attention(qw, kvw, sink):
    nt = T // WA_TQ

    def bounds(i):
        is_ctx = i < WA_CTX_TILES
        b = (i - WA_CTX_TILES) // WA_LAT_TILES
        lo = jnp.where(is_ctx, 0, WA_CTX_TILES + b * WA_LAT_TILES)
        hi = jnp.where(is_ctx, nt - 1, WA_CTX_TILES + (b + 1) * WA_LAT_TILES - 1)
        return lo, hi

    def prev_map(i):
        lo, _ = bounds(i)
        return (jnp.maximum(i - 1, lo), 0)

    def next_map(i):
        _, hi = bounds(i)
        return (jnp.minimum(i + 1, hi), 0)

    def ctx_map(i):
        b = jnp.where(i < WA_CTX_TILES, i // (L // WA_TQ), (i - WA_CTX_TILES) // WA_LAT_TILES)
        return (b, 0)

    kvw_w = 2 * W_WIN_KV
    return pl.pallas_call(
        _win_attn_kernel,
        grid=(nt,),
        in_specs=[
            pl.BlockSpec(memory_space=pltpu.SMEM),
            pl.BlockSpec((WA_TQ, W_WIN_Q), lambda i: (i, 0)),
            pl.BlockSpec((WA_TQ, kvw_w), prev_map),
            pl.BlockSpec((WA_TQ, kvw_w), lambda i: (i, 0)),
            pl.BlockSpec((WA_TQ, kvw_w), next_map),
            pl.BlockSpec((L, kvw_w), ctx_map),
        ],
        out_specs=pl.BlockSpec((WA_TQ, W_WIN_Q), lambda i: (i, 0)),
        out_shape=jax.ShapeDtypeStruct((T, W_WIN_Q), BF16),
        compiler_params=_cparams("arbitrary"),
        name="window_attention",
    )(sink, qw, kvw, kvw, kvw, kvw)


NA_TQ = 256
NA_QROWS = NA_TQ // GRID_W
NA_KROWS = 3 * NA_QROWS
NA_NKEY = NA_KROWS * GRID_W
NA_TYPES = 4
RPB_R = 2 * NA_KH - 1
RPB_C = 2 * NA_KW - 1


def _na_row_valid(ty, a, j):
    if ty == 0:
        return NA_QROWS <= j < NA_QROWS + NA_KH
    if ty == 1:
        return a <= j < a + NA_KH
    if ty == 2:
        return j < NA_KH
    return False


def _rpb_kernel(rpb_ref, o_ref):
    h = pl.program_id(0)
    shp = (GRID_W, 2 * GRID_W)
    qc = lax.broadcasted_iota(jnp.int32, shp, 0)
    lane = lax.broadcasted_iota(jnp.int32, shp, 1)
    kc = lane % GRID_W
    second = lane >= GRID_W
    dcol = kc - qc + (NA_KW - 1)
    cs = jnp.clip(qc - NA_KW // 2, 0, GRID_W - NA_KW)
    colmask = jnp.logical_and(kc >= cs, kc < cs + NA_KW)
    neg = jnp.full(shp, NEG, F32)
    base = h * (RPB_R * RPB_C)
    pair = []
    for dr in range(RPB_R - 1):
        acc = jnp.zeros(shp, F32)
        for dd in range(RPB_C):
            v0 = rpb_ref[base + dr * RPB_C + dd]
            v1 = rpb_ref[base + (dr + 1) * RPB_C + dd]
            acc = jnp.where(dcol == dd, jnp.where(second, v1, v0), acc)
        pair.append(jnp.where(colmask, acc, neg))
    for ty in range(NA_TYPES):
        for a in range(NA_QROWS):
            for jp in range(NA_KROWS // 2):
                j = 2 * jp
                ok0 = _na_row_valid(ty, a, j)
                ok1 = _na_row_valid(ty, a, j + 1)
                dr = j - a + NA_QROWS - 1
                if ok0 and ok1:
                    tile = pair[dr]
                elif ok0:
                    tile = jnp.where(second, neg, pair[dr])
                elif ok1:
                    tile = jnp.where(second, pair[dr], neg)
                else:
                    tile = neg
                o_ref[ty, 0, a * GRID_W:(a + 1) * GRID_W, jp * 2 * GRID_W:(jp + 1) * 2 * GRID_W] = tile


def _rpb_tiles(rpb):
    return pl.pallas_call(
        _rpb_kernel,
        grid=(NA_H,),
        in_specs=[pl.BlockSpec(memory_space=pltpu.SMEM)],
        out_specs=pl.BlockSpec((NA_TYPES, 1, NA_TQ, NA_NKEY), lambda h: (0, h, 0, 0)),
        out_shape=jax.ShapeDtypeStruct((NA_TYPES, NA_H, NA_TQ, NA_NKEY), F32),
        compiler_params=_cparams("arbitrary"),
        name="rpb_tiles",
    )(rpb.reshape(-1))


NA_CTX_TILES = CTX_ROWS // NA_TQ
NA_LAT_TILES = S // NA_TQ


def _na_attn_kernel(q_ref, kp_ref, kc_ref, kn_ref, vp_ref, vc_ref, vn_ref, kx_ref, vx_ref, bias_ref, o_ref):
    for h in range(NA_H):
        hs = slice(h * HD, (h + 1) * HD)
        q = q_ref[:, hs]
        s_p = _dot_nt(q, kp_ref[:, hs]) * ATT_SCALE + bias_ref[0, h, :, 0:NA_TQ]
        s_c = _dot_nt(q, kc_ref[:, hs]) * ATT_SCALE + bias_ref[0, h, :, NA_TQ:2 * NA_TQ]
        s_n = _dot_nt(q, kn_ref[:, hs]) * ATT_SCALE + bias_ref[0, h, :, 2 * NA_TQ:3 * NA_TQ]
        s_x = _dot_nt(q, kx_ref[:, hs]) * ATT_SCALE
        m = jnp.maximum(jnp.maximum(jnp.max(s_p, axis=-1, keepdims=True), jnp.max(s_c, axis=-1, keepdims=True)),
                        jnp.maximum(jnp.max(s_n, axis=-1, keepdims=True), jnp.max(s_x, axis=-1, keepdims=True)))
        p_p = jnp.exp(s_p - m)
        p_c = jnp.exp(s_c - m)
        p_n = jnp.exp(s_n - m)
        p_x = jnp.exp(s_x - m)
        den = (jnp.sum(p_p, axis=-1, keepdims=True) + jnp.sum(p_c, axis=-1, keepdims=True)
               + jnp.sum(p_n, axis=-1, keepdims=True) + jnp.sum(p_x, axis=-1, keepdims=True))
        o = (jnp.dot(p_p.astype(BF16), vp_ref[:, hs], preferred_element_type=F32)
             + jnp.dot(p_c.astype(BF16), vc_ref[:, hs], preferred_element_type=F32)
             + jnp.dot(p_n.astype(BF16), vn_ref[:, hs], preferred_element_type=F32)
             + jnp.dot(p_x.astype(BF16), vx_ref[:, hs], preferred_element_type=F32))
        o_ref[:, hs] = (o / den).astype(BF16)


def _na_attention(qn, kn, vn, bias):
    nt = T // NA_TQ

    def bounds(i):
        is_ctx = i < NA_CTX_TILES
        b = (i - NA_CTX_TILES) // NA_LAT_TILES
        lo = jnp.where(is_ctx, 0, NA_CTX_TILES + b * NA_LAT_TILES)
        hi = jnp.where(is_ctx, nt - 1, NA_CTX_TILES + (b + 1) * NA_LAT_TILES - 1)
        return lo, hi

    def prev_map(i):
        lo, _ = bounds(i)
        return (jnp.maximum(i - 1, lo), 0)

    def next_map(i):
        _, hi = bounds(i)
        return (jnp.minimum(i + 1, hi), 0)

    def ctx_map(i):
        return (jnp.where(i < NA_CTX_TILES, i, (i - NA_CTX_TILES) // NA_LAT_TILES), 0)

    def bias_map(i):
        n = (i - NA_CTX_TILES) % NA_LAT_TILES
        ty = jnp.where(i < NA_CTX_TILES, 3, jnp.where(n == 0, 0, jnp.where(n == NA_LAT_TILES - 1, 2, 1)))
        return (ty, 0, 0, 0)

    w = W_NA_KV
    cur = pl.BlockSpec((NA_TQ, w), lambda i: (i, 0))
    prv = pl.BlockSpec((NA_TQ, w), prev_map)
    nxt = pl.BlockSpec((NA_TQ, w), next_map)
    ctx = pl.BlockSpec((L, w), ctx_map)
    return pl.pallas_call(
        _na_attn_kernel,
        grid=(nt,),
        in_specs=[cur, prv, cur, nxt, prv, cur, nxt, ctx, ctx,
                  pl.BlockSpec((1, NA_H, NA_TQ, NA_NKEY), bias_map)],
        out_specs=cur,
        out_shape=jax.ShapeDtypeStruct((T, W_NA_Q), BF16),
        compiler_params=_cparams("arbitrary"),
        name="neighborhood_attention",
    )(qn, kn, kn, kn, vn, vn, vn, kn, vn, bias)


MG_TM = 512
MG_TN = 512


def _merge_kernel(ca_ref, aw_ref, an_ref, wc_ref, ww_ref, wn_ref, ga_ref, gb_ref, gc_ref, o_ref):
    ya = jnp.dot(ca_ref[...], wc_ref[...], preferred_element_type=F32)
    yb = jnp.dot(aw_ref[...], ww_ref[...], preferred_element_type=F32)
    yc = jnp.dot(an_ref[...], wn_ref[...], preferred_element_type=F32)
    o = _sigmoid(ga_ref[...]) * ya + _sigmoid(gb_ref[...]) * yb + _sigmoid(gc_ref[...]) * yc
    o_ref[...] = o.astype(BF16)


def _merge(hconv, aw, an, w_conv_out, w_win_out, w_na_out, u):
    gb0 = GATE_OFF // MG_TN
    gstep = D // MG_TN

    def a_spec(k):
        return pl.BlockSpec((MG_TM, k), lambda i, j: (i, 0))

    def w_spec(k):
        return pl.BlockSpec((k, MG_TN), lambda i, j: (0, j))

    def g_spec(which):
        return pl.BlockSpec((MG_TM, MG_TN), lambda i, j: (i, gb0 + which * gstep + j))

    return pl.pallas_call(
        _merge_kernel,
        grid=(T // MG_TM, D // MG_TN),
        in_specs=[a_spec(CONV_CH), a_spec(W_WIN_Q), a_spec(W_NA_Q),
                  w_spec(CONV_CH), w_spec(W_WIN_Q), w_spec(W_NA_Q),
                  g_spec(0), g_spec(1), g_spec(2)],
        out_specs=pl.BlockSpec((MG_TM, MG_TN), lambda i, j: (i, j)),
        out_shape=jax.ShapeDtypeStruct((T, D), BF16),
        compiler_params=_cparams("arbitrary", "arbitrary"),
        name="gated_merge",
    )(hconv, aw, an, w_conv_out, w_win_out, w_na_out, u, u, u)


TK_TT = 256


def _topk_rounds(scores, n_rounds):
    nrow = scores.shape[0]
    idx = lax.broadcasted_iota(jnp.int32, scores.shape, 0).astype(F32)
    work = scores
    rank = jnp.full(scores.shape, float(n_rounds), F32)
    vals = []
    for a in range(n_rounds):
        m = jnp.max(work, axis=0, keepdims=True)
        first = jnp.min(jnp.where(work == m, idx, float(nrow)), axis=0, keepdims=True)
        sel = idx == first
        rank = jnp.where(sel, float(a), rank)
        work = jnp.where(sel, -jnp.inf, work)
        vals.append(m)
    return vals, rank


def _peer_topk_kernel(q_ref, k1_ref, k2_ref, n_ref, e1_ref, r2_ref, e2_ref):
    kk = PEER_TOPK

    def head(h, carry):
        c1 = pl.multiple_of(h * 2 * HD, 2 * HD)
        q1 = q_ref[:, pl.ds(c1, HD)]
        q2 = q_ref[:, pl.ds(c1 + HD, HD)]
        s1 = _dot_nt(k1_ref[h].astype(BF16), q1)
        s2 = _dot_nt(k2_ref[h].astype(BF16), q2)
        v1, rank1 = _topk_rounds(s1, kk)
        v2, rank2 = _topk_rounds(s2, kk)
        v1all = jnp.concatenate(v1, axis=0)
        v2all = jnp.concatenate(v2, axis=0)
        arow = lax.broadcasted_iota(jnp.int32, v2all.shape, 0).astype(F32)
        cnt = jnp.zeros(v2all.shape, F32)
        front = v1all + v2[0]
        top = v1[0] + v2[0]
        z = jnp.zeros_like(top)
        for _ in range(kk):
            m = jnp.max(front, axis=0, keepdims=True)
            first = jnp.min(jnp.where(front == m, arow, float(kk)), axis=0, keepdims=True)
            sel = arow == first
            cnt = cnt + jnp.where(sel, 1.0, 0.0)
            z = z + jnp.exp(m - top)
            taken = jnp.max(jnp.where(sel, cnt, -1.0), axis=0, keepdims=True)
            nxt = jnp.max(jnp.where(arow == taken, v2all, -jnp.inf), axis=0, keepdims=True)
            front = jnp.where(sel, v1all + nxt, front)
        nfull = jnp.zeros(s1.shape, F32)
        for a in range(kk):
            nfull = jnp.where(rank1 == float(a), cnt[a:a + 1, :], nfull)
        n_ref[h] = nfull
        e1_ref[h] = jnp.exp(s1 - v1[0]) / z
        r2_ref[h] = rank2
        e2_ref[h] = jnp.exp(s2 - v2[0])
        return carry

    lax.fori_loop(0, PEER_HEADS, head, 0)


def _peer_topk(q, k1, k2):
    tt = TK_TT
    kspec = pl.BlockSpec((PEER_HEADS, N_KEYS, HD), lambda i: (0, 0, 0))
    ospec = pl.BlockSpec((PEER_HEADS, N_KEYS, tt), lambda i: (0, 0, i))
    oshape = jax.ShapeDtypeStruct((PEER_HEADS, N_KEYS, T), F32)
    return pl.pallas_call(
        _peer_topk_kernel,
        grid=(T // tt,),
        in_specs=[pl.BlockSpec((tt, 2 * HD * PEER_HEADS), lambda i: (i, 0)), kspec, kspec],
        out_specs=[ospec] * 4,
        out_shape=[oshape] * 4,
        compiler_params=_cparams("arbitrary"),
        name="peer_topk",
    )(q, k1, k2)


EX_TM = 512
EX_TN = 512
EX_SUB = 256
EX_NC = 1024
EX_LC = 128
EX_RC = 32


def _gelu_tanh(x):
    cdf = 0.5 * (1.0 + jnp.tanh(math.sqrt(2.0 / math.pi) * (x + 0.044715 * (x * x * x))))
    return x * cdf


def _experts_kernel(h_ref, u_ref, v_ref, n_ref, e1_ref, r2_ref, e2_ref, o_ref, act_ref, w_ref):
    j = pl.program_id(1)

    @pl.when(j == 0)
    def _():
        o_ref[...] = jnp.zeros_like(o_ref)

    def gate_tile(i1, rows, cs):
        gate = jnp.zeros((EX_RC, EX_LC), F32)
        for hh in range(PEER_HEADS):
            nrow = n_ref[i1, hh:hh + 1, cs]
            e1row = e1_ref[i1, hh:hh + 1, cs]
            gate = gate + jnp.where(r2_ref[hh, rows, cs] < nrow, e2_ref[hh, rows, cs], 0.0) * e1row
        return gate

    w_parts = []
    for s in range(EX_TN // EX_SUB):
        es = slice(s * EX_SUB, (s + 1) * EX_SUB)
        act = _gelu_tanh(jnp.dot(u_ref[es, :], h_ref[...], preferred_element_type=F32))
        for ii in range(EX_SUB // N_KEYS):
            i1 = j * (EX_TN // N_KEYS) + s * (EX_SUB // N_KEYS) + ii
            for rc in range(N_KEYS // EX_RC):
                rows = slice(rc * EX_RC, (rc + 1) * EX_RC)
                arows = slice(ii * N_KEYS + rc * EX_RC, ii * N_KEYS + (rc + 1) * EX_RC)
                row = []
                for c in range(EX_TM // EX_LC):
                    cs = slice(c * EX_LC, (c + 1) * EX_LC)
                    row.append((gate_tile(i1, rows, cs) * act[arows, cs]).astype(BF16))
                w_parts.append(jnp.concatenate(row, axis=1))
    w_t = jnp.concatenate(w_parts, axis=0)
    for nc in range(D // EX_NC):
        ns = slice(nc * EX_NC, (nc + 1) * EX_NC)
        o_ref[:, ns] += lax.dot_general(w_t, v_ref[:, ns], (((0,), (0,)), ((), ())),
                                        preferred_element_type=F32)


def _experts(h_t, u, v, tables):
    n, e1, r2, e2 = tables
    n = jnp.transpose(n, (1, 0, 2))
    e1 = jnp.transpose(e1, (1, 0, 2))
    kspec = pl.BlockSpec((N_KEYS, PEER_HEADS, EX_TM), lambda i, j: (0, 0, i))
    tspec = pl.BlockSpec((PEER_HEADS, N_KEYS, EX_TM), lambda i, j: (0, 0, i))
    return pl.pallas_call(
        _experts_kernel,
        grid=(T // EX_TM, N_EXPERTS // EX_TN),
        in_specs=[pl.BlockSpec((D, EX_TM), lambda i, j: (0, i)),
                  pl.BlockSpec((EX_TN, D), lambda i, j: (j, 0)),
                  pl.BlockSpec((EX_TN, D), lambda i, j: (j, 0)),
                  kspec, kspec, tspec, tspec],
        out_specs=pl.BlockSpec((EX_TM, D), lambda i, j: (i, 0), pipeline_mode=pl.Buffered(1)),
        out_shape=jax.ShapeDtypeStruct((T, D), F32),
        scratch_shapes=[pltpu.VMEM((EX_TN, EX_TM), F32), pltpu.VMEM((EX_TN, EX_TM), BF16)],
        compiler_params=_cparams("arbitrary", "arbitrary"),
        name="peer_experts",
    )(h_t, u, v, n, e1, r2, e2)


FR_TM = 256


def _final_kernel(x_ref, f_ref, g_ref, o_ref):
    o_ref[...] = x_ref[...] + g_ref[0, 0] * f_ref[...]


def _final_residual(x, f, modt, layer, g_k):
    off = CTX_ROWS // FR_TM
    lat = pl.BlockSpec((FR_TM, D), lambda i: (i + off, 0))
    return pl.pallas_call(
        _final_kernel,
        grid=(B * S // FR_TM,),
        in_specs=[lat, lat,
                  pl.BlockSpec((1, 1, 1, D), lambda i: (layer, _group_of_row((i + off) * FR_TM), 0, g_k))],
        out_specs=pl.BlockSpec((FR_TM, D), lambda i: (i, 0)),
        out_shape=jax.ShapeDtypeStruct((B * S, D), F32),
        compiler_params=_cparams("arbitrary"),
        name="final_residual",
    )(x, f, modt)


SH1, SC1, G1, SH2, SC2, G2 = range(6)


def kernel(x, c, ctx, c_ctx, w_ada, b_ada, norm1_g, norm2_g, w_in, conv_w, conv_b, conv_ln_g, conv_ln_b,
           w_conv_out, win_qn_g, win_kn_g, win_sink, w_win_out, na_qn_g, na_kn_g, na_rpb, w_na_out, w_out,
           peer_wq, peer_k1, peer_k2, peer_u, peer_v):
    xs = jnp.concatenate([ctx.reshape(CTX_ROWS, D), x.reshape(B * S, D)], axis=0)
    cvec = jnp.concatenate([c_ctx[None], c, jnp.zeros((N_GROUPS - 1 - B, D), F32)], axis=0)
    modt = _mods(cvec, w_ada, b_ada).reshape(DEPTH, N_GROUPS, 1, 6 * D)
    cosf, sinf = _rope_tables()

    f = None
    for l in range(DEPTH):
        if l == 0:
            h1 = _normmod(xs, modt, l, norm1_g[l], SC1, SH1)
        else:
            xs, h1 = _normmod(xs, modt, l, norm1_g[l], SC1, SH1, f=f, g_k=G2)
        u = _matmul(h1, w_in[l].astype(BF16), F32, 1024)
        hconv = _conv_module(u, conv_w[l], conv_b[l], conv_ln_g[l], conv_ln_b[l])
        gains = jnp.concatenate([win_qn_g[l][None], win_kn_g[l][None], na_qn_g[l][None], na_kn_g[l][None],
                                 jnp.zeros((4, HD), F32)], axis=0)
        qw, qn, kvw, kn, vn = _headnorm(u, cosf, sinf, gains)
        aw = _win_attention(qw, kvw, win_sink[l])
        an = _na_attention(qn, kn, vn, _rpb_tiles(na_rpb[l]))
        merged = _merge(hconv, aw, an, w_conv_out[l].astype(BF16), w_win_out[l].astype(BF16),
                        w_na_out[l].astype(BF16), u)
        xs = _matmul_resid(merged, w_out[l].astype(BF16), xs, modt, l, G1, 1024)
        h2, h2_t = _normmod(xs, modt, l, norm2_g[l], SC2, SH2, want_t=True)
        q = _matmul(h2, peer_wq[l].astype(BF16), BF16, 1024)
        tables = _peer_topk(q, peer_k1[l], peer_k2[l])
        f = _experts(h2_t, peer_u[l].astype(BF16), peer_v[l].astype(BF16), tables)
    out = _final_residual(xs, f, modt, DEPTH - 1, G2)
    return out.reshape(B, S, D)
```

```python
import functools
import math

import jax
import jax.numpy as jnp
import numpy as np
from jax import lax
from jax.experimental import pallas as pl
from jax.experimental.pallas import tpu as pltpu

F32 = jnp.float32
BF16 = jnp.bfloat16

D = 4096
B = 2
S = 4096
L = 256
DEPTH = 2
GRID_W = 64
HD = 128
EPS = 1e-6
NEG = -1e30
ROPE_BASE = 10000.0

CONV_CH = D // 4
CONV_K = 31
WIN_HQ = 16
WIN_HKV = 4
WIN_G = WIN_HQ // WIN_HKV
NA_H = 8
NA_KH = 8
NA_KW = 16
PEER_HEADS = 8
N_KEYS = 128
N_EXPERTS = N_KEYS * N_KEYS
PEER_TOPK = 16

W_CONV_IN = 2 * CONV_CH
W_WIN_Q = WIN_HQ * HD
W_NA_Q = NA_H * HD
W_GATE = 3 * D
W_WIN_KV = WIN_HKV * HD
W_NA_KV = NA_H * HD
Q_WIN_OFF = W_CONV_IN
Q_NA_OFF = Q_WIN_OFF + W_WIN_Q
GATE_OFF = Q_NA_OFF + W_NA_Q
KV_OFF = GATE_OFF + W_GATE
IN_COLS = KV_OFF + 2 * W_WIN_KV + 2 * W_NA_KV

CTX_ROWS = B * L
T = CTX_ROWS + B * S
N_GROUPS = 8
ATT_SCALE = HD ** -0.5

VMEM_LIMIT = 56 * 1024 * 1024


def _cparams(*sem):
    return pltpu.CompilerParams(dimension_semantics=sem, vmem_limit_bytes=VMEM_LIMIT)


def _group_of_row(row0):
    return jnp.where(row0 < CTX_ROWS, 0, 1 + (row0 - CTX_ROWS) // S)


def _sigmoid(z):
    return 1.0 / (1.0 + jnp.exp(-z))


def _dot_nt(a, b):
    return lax.dot_general(a, b, (((1,), (1,)), ((), ())), preferred_element_type=F32)


def _first_tile(tile, latent_only):
    return CTX_ROWS // tile if latent_only else 0


CAST_BLOCK_BYTES = 8 * 1024 * 1024


def _cast_kernel(w_ref, o_ref):
    o_ref[...] = w_ref[0].astype(BF16)


def _cast_bf16(w, layer):
    _, m, n = w.shape
    rb = m
    while rb * n * 4 > CAST_BLOCK_BYTES:
        rb //= 2
    return pl.pallas_call(
        _cast_kernel,
        grid=(m // rb,),
        in_specs=[pl.BlockSpec((1, rb, n), lambda i: (layer, i, 0))],
        out_specs=pl.BlockSpec((rb, n), lambda i: (i, 0)),
        out_shape=jax.ShapeDtypeStruct((m, n), BF16),
        compiler_params=_cparams("arbitrary"),
        name="cast_bf16",
    )(w)


MOD_TN = 512


def _mods_kernel(c_ref, w_ref, b_ref, o_ref):
    cv = c_ref[...]
    a = (cv * _sigmoid(cv)).astype(BF16)
    w = w_ref[0].astype(BF16)
    o_ref[0] = jnp.dot(a, w, preferred_element_type=F32) + b_ref[0]


def _mods(cvec, w_ada, b_ada):
    n = 6 * D
    return pl.pallas_call(
        _mods_kernel,
        grid=(DEPTH, n // MOD_TN),
        in_specs=[
            pl.BlockSpec((N_GROUPS, D), lambda l, j: (0, 0)),
            pl.BlockSpec((1, D, MOD_TN), lambda l, j: (l, 0, j)),
            pl.BlockSpec((1, 1, MOD_TN), lambda l, j: (l, 0, j)),
        ],
        out_specs=pl.BlockSpec((1, N_GROUPS, MOD_TN), lambda l, j: (l, 0, j)),
        out_shape=jax.ShapeDtypeStruct((DEPTH, N_GROUPS, n), F32),
        compiler_params=_cparams("arbitrary", "arbitrary"),
        name="adaln_mods",
    )(cvec, w_ada, b_ada.reshape(DEPTH, 1, n))


NM_TM = 256


def _normmod_kernel(has_f, want_t, *refs):
    refs = list(refs)
    ht_ref = refs.pop() if want_t else None
    if has_f:
        x_ref, f_ref, g_ref, ng_ref, sc_ref, sh_ref, xo_ref, h_ref = refs
        x = x_ref[...] + g_ref[0, 0] * f_ref[...]
        xo_ref[...] = x
    else:
        x_ref, ng_ref, sc_ref, sh_ref, h_ref = refs
        x = x_ref[...]
    ms = jnp.mean(x * x, axis=-1, keepdims=True)
    y = x * lax.rsqrt(ms + EPS) * ng_ref[...]
    h = y * (1.0 + sc_ref[0, 0]) + sh_ref[0, 0]
    h_ref[...] = h.astype(BF16)
    if want_t:
        ht_ref[...] = h.T.astype(BF16)


def _mod_spec(layer, k, tm, t0=0):
    return pl.BlockSpec((1, 1, 1, D), lambda i: (layer, _group_of_row((i + t0) * tm), 0, k))


def _normmod(x, modt, layer, norm_g, sc_k, sh_k, f=None, g_k=None, want_t=False, latent_only=False):
    tm = NM_TM
    t0 = _first_tile(tm, latent_only)
    row = pl.BlockSpec((tm, D), lambda i: (i + t0, 0))
    has_f = f is not None
    in_specs = [row]
    args = [x]
    if has_f:
        in_specs += [row, _mod_spec(layer - 1, g_k, tm, t0)]
        args += [f, modt]
    in_specs += [pl.BlockSpec((1, D), lambda i: (0, 0)), _mod_spec(layer, sc_k, tm, t0),
                 _mod_spec(layer, sh_k, tm, t0)]
    args += [norm_g.reshape(1, D), modt, modt]
    h_shape = jax.ShapeDtypeStruct((T, D), BF16)
    if has_f:
        out_specs = [row, row]
        out_shape = [jax.ShapeDtypeStruct((T, D), F32), h_shape]
    else:
        out_specs = [row]
        out_shape = [h_shape]
    if want_t:
        out_specs = out_specs + [pl.BlockSpec((D, tm), lambda i: (0, i + t0))]
        out_shape = out_shape + [jax.ShapeDtypeStruct((D, T), BF16)]
    outs = pl.pallas_call(
        functools.partial(_normmod_kernel, has_f, want_t),
        grid=(T // tm - t0,),
        in_specs=in_specs,
        out_specs=out_specs,
        out_shape=out_shape,
        compiler_params=_cparams("arbitrary"),
        name="normmod_resid" if has_f else "normmod",
    )(*args)
    return outs[0] if len(outs) == 1 else tuple(outs)


MM_TM = 512


def _mm_kernel(a_ref, w_ref, o_ref):
    o_ref[...] = jnp.dot(a_ref[...], w_ref[...], preferred_element_type=F32).astype(o_ref.dtype)


def _mm_resid_kernel(a_ref, w_ref, r_ref, g_ref, o_ref):
    acc = jnp.dot(a_ref[...], w_ref[...], preferred_element_type=F32)
    o_ref[...] = r_ref[...] + g_ref[0, 0] * acc


def _matmul(a, w, out_dtype, tn, latent_only=False):
    m, k = a.shape
    n = w.shape[1]
    t0 = _first_tile(MM_TM, latent_only)
    return pl.pallas_call(
        _mm_kernel,
        grid=(m // MM_TM - t0, n // tn),
        in_specs=[pl.BlockSpec((MM_TM, k), lambda i, j: (i + t0, 0)),
                  pl.BlockSpec((k, tn), lambda i, j: (0, j))],
        out_specs=pl.BlockSpec((MM_TM, tn), lambda i, j: (i + t0, j)),
        out_shape=jax.ShapeDtypeStruct((m, n), out_dtype),
        compiler_params=_cparams("arbitrary", "arbitrary"),
        name="matmul",
    )(a, w)


def _matmul_resid(a, w, resid, modt, layer, g_k, tn, latent_only=False):
    m, k = a.shape
    n = w.shape[1]
    nj = n // tn
    t0 = _first_tile(MM_TM, latent_only)
    return pl.pallas_call(
        _mm_resid_kernel,
        grid=(m // MM_TM - t0, nj),
        in_specs=[pl.BlockSpec((MM_TM, k), lambda i, j: (i + t0, 0)),
                  pl.BlockSpec((k, tn), lambda i, j: (0, j)),
                  pl.BlockSpec((MM_TM, tn), lambda i, j: (i + t0, j)),
                  pl.BlockSpec((1, 1, 1, tn),
                               lambda i, j: (layer, _group_of_row((i + t0) * MM_TM), 0, g_k * nj + j))],
        out_specs=pl.BlockSpec((MM_TM, tn), lambda i, j: (i + t0, j)),
        out_shape=jax.ShapeDtypeStruct((m, n), F32),
        compiler_params=_cparams("arbitrary", "arbitrary"),
        name="matmul_resid",
    )(a, w, resid, modt)


CV_TM = 256
CV_HALO = 16
CV_RC = 64
CV_CC = 128


def _conv_kernel(t0, prev_ref, cur_ref, next_ref, w_ref, cb_ref, lg_ref, lb_ref, o_ref, buf_ref, acc_ref):
    i = pl.program_id(0) + t0
    row0 = i * CV_TM
    lat = row0 - CTX_ROWS
    seq_start = jnp.where(row0 < CTX_ROWS, row0 % L == 0, lat % S == 0)
    seq_end = jnp.where(row0 < CTX_ROWS, (row0 + CV_TM) % L == 0, (lat + CV_TM) % S == 0)

    def glu(u):
        return u[:, :CONV_CH] * _sigmoid(u[:, CONV_CH:])

    buf_ref[0:CV_HALO, :] = glu(prev_ref[...]) * jnp.where(seq_start, 0.0, 1.0)
    buf_ref[CV_HALO:CV_HALO + CV_TM, :] = glu(cur_ref[...])
    buf_ref[CV_HALO + CV_TM:, :] = glu(next_ref[...]) * jnp.where(seq_end, 0.0, 1.0)

    tap0 = CV_HALO - CONV_K // 2

    def col_chunk(c, carry):
        c0 = pl.multiple_of(c * CV_CC, CV_CC)
        for r in range(CV_TM // CV_RC):
            acc = jnp.zeros((CV_RC, CV_CC), F32)
            for k in range(CONV_K):
                r0 = r * CV_RC + k + tap0
                acc = acc + w_ref[k:k + 1, pl.ds(c0, CV_CC)] * buf_ref[r0:r0 + CV_RC, pl.ds(c0, CV_CC)]
            acc_ref[r * CV_RC:(r + 1) * CV_RC, pl.ds(c0, CV_CC)] = acc
        return carry

    lax.fori_loop(0, CONV_CH // CV_CC, col_chunk, 0)

    h = acc_ref[...] + cb_ref[...]
    mu = jnp.mean(h, axis=-1, keepdims=True)
    xc = h - mu
    var = jnp.mean(xc * xc, axis=-1, keepdims=True)
    y = xc * lax.rsqrt(var + EPS) * lg_ref[...] + lb_ref[...]
    o_ref[...] = (y * _sigmoid(y)).astype(BF16)


def _conv_module(u, conv_w, conv_b, ln_g, ln_b, latent_only=False):
    hb = CV_TM // CV_HALO
    last = T // CV_HALO - 1
    t0 = _first_tile(CV_TM, latent_only)
    vec = pl.BlockSpec((1, CONV_CH), lambda i: (0, 0))
    return pl.pallas_call(
        functools.partial(_conv_kernel, t0),
        grid=(T // CV_TM - t0,),
        in_specs=[
            pl.BlockSpec((CV_HALO, W_CONV_IN), lambda i: (jnp.maximum((i + t0) * hb - 1, 0), 0)),
            pl.BlockSpec((CV_TM, W_CONV_IN), lambda i: (i + t0, 0)),
            pl.BlockSpec((CV_HALO, W_CONV_IN), lambda i: (jnp.minimum((i + t0 + 1) * hb, last), 0)),
            pl.BlockSpec((CONV_K, CONV_CH), lambda i: (0, 0)),
            vec, vec, vec,
        ],
        out_specs=pl.BlockSpec((CV_TM, CONV_CH), lambda i: (i + t0, 0)),
        out_shape=jax.ShapeDtypeStruct((T, CONV_CH), BF16),
        scratch_shapes=[pltpu.VMEM((CV_TM + 2 * CV_HALO, CONV_CH), F32),
                        pltpu.VMEM((CV_TM, CONV_CH), F32)],
        compiler_params=_cparams("arbitrary"),
        name="conv_module",
    )(u, u, u, conv_w, conv_b.reshape(1, CONV_CH), ln_g.reshape(1, CONV_CH), ln_b.reshape(1, CONV_CH))


HN_TM = 256


def _headnorm_kernel(qw_ref, qn_ref, kvw_ref, kn_ref, vn_ref, cos_ref, sin_ref, g_ref,
                     qw_o, qn_o, kvw_o, kn_o, vn_o):
    cosf = cos_ref[...]
    sinf = sin_ref[...]
    lane = lax.broadcasted_iota(jnp.int32, (HN_TM, HD), 1)
    first_half = (lane % (HD // 2)) < (HD // 4)

    def norm(x, gi):
        ms = jnp.mean(x * x, axis=-1, keepdims=True)
        return x * lax.rsqrt(ms + EPS) * g_ref[gi:gi + 1, :]

    def rope(x):
        partner = jnp.where(first_half, pltpu.roll(x, HD - HD // 4, 1), pltpu.roll(x, HD // 4, 1))
        return x * cosf + partner * sinf

    for hd in range(WIN_HQ):
        sl = slice(hd * HD, (hd + 1) * HD)
        qw_o[:, sl] = rope(norm(qw_ref[:, sl], 0)).astype(BF16)
    for hd in range(WIN_HKV):
        sl = slice(hd * HD, (hd + 1) * HD)
        kvw_o[:, sl] = rope(norm(kvw_ref[:, sl], 1)).astype(BF16)
    kvw_o[:, W_WIN_KV:] = kvw_ref[:, W_WIN_KV:].astype(BF16)
    for hd in range(NA_H):
        sl = slice(hd * HD, (hd + 1) * HD)
        qn_o[:, sl] = norm(qn_ref[:, sl], 2).astype(BF16)
        kn_o[:, sl] = norm(kn_ref[:, sl], 3).astype(BF16)
    vn_o[...] = vn_ref[...].astype(BF16)


def _headnorm(u, cosf, sinf, gains):
    tm = HN_TM
    kvb = KV_OFF // W_NA_KV
    row128 = pl.BlockSpec((tm, HD), lambda i: (i, 0))

    def ospec(w):
        return pl.BlockSpec((tm, w), lambda i: (i, 0))

    def oshape(w):
        return jax.ShapeDtypeStruct((T, w), BF16)

    return pl.pallas_call(
        _headnorm_kernel,
        grid=(T // tm,),
        in_specs=[
            pl.BlockSpec((tm, W_WIN_Q), lambda i: (i, Q_WIN_OFF // W_WIN_Q)),
            pl.BlockSpec((tm, W_NA_Q), lambda i: (i, Q_NA_OFF // W_NA_Q)),
            pl.BlockSpec((tm, 2 * W_WIN_KV), lambda i: (i, kvb)),
            pl.BlockSpec((tm, W_NA_KV), lambda i: (i, kvb + 1)),
            pl.BlockSpec((tm, W_NA_KV), lambda i: (i, kvb + 2)),
            row128, row128,
            pl.BlockSpec((8, HD), lambda i: (0, 0)),
        ],
        out_specs=[ospec(W_WIN_Q), ospec(W_NA_Q), ospec(2 * W_WIN_KV), ospec(W_NA_KV), ospec(W_NA_KV)],
        out_shape=[oshape(W_WIN_Q), oshape(W_NA_Q), oshape(2 * W_WIN_KV), oshape(W_NA_KV), oshape(W_NA_KV)],
        compiler_params=_cparams("arbitrary"),
        name="headnorm_rope",
    )(u, u, u, u, u, cosf, sinf, gains)


def _rope_tables():
    t = jnp.arange(S)
    row = (t // GRID_W).astype(F32)
    col = (t % GRID_W).astype(F32)
    axis_dim = HD // 2
    inv = ROPE_BASE ** (-jnp.arange(0, axis_dim, 2, dtype=F32) / axis_dim)
    ar = row[:, None] * inv
    ac = col[:, None] * inv
    cosl = jnp.concatenate([jnp.cos(ar), jnp.cos(ar), jnp.cos(ac), jnp.cos(ac)], axis=1)
    sinl = jnp.concatenate([-jnp.sin(ar), jnp.sin(ar), -jnp.sin(ac), jnp.sin(ac)], axis=1)
    cosf = jnp.concatenate([jnp.ones((CTX_ROWS, HD), F32), cosl, cosl], axis=0)
    sinf = jnp.concatenate([jnp.zeros((CTX_ROWS, HD), F32), sinl, sinl], axis=0)
    return cosf, sinf


WA_TQ = 128
WA_CTX_TILES = CTX_ROWS // WA_TQ
WA_LAT_TILES = S // WA_TQ


def _win_attn_kernel(t0, sink_ref, q_ref, kp_ref, kc_ref, kn_ref, kx_ref, o_ref):
    i = pl.program_id(0) + t0
    is_ctx = i < WA_CTX_TILES
    n = (i - WA_CTX_TILES) % WA_LAT_TILES
    far = 4 * WA_TQ
    off_prev = jnp.where(jnp.logical_or(is_ctx, n == 0), far, 0)
    off_cur = jnp.where(is_ctx, far, 0)
    off_next = jnp.where(jnp.logical_or(is_ctx, n == WA_LAT_TILES - 1), far, 0)
    rows = WIN_G * WA_TQ
    r = lax.broadcasted_iota(jnp.int32, (rows, WA_TQ), 0) % WA_TQ
    c = lax.broadcasted_iota(jnp.int32, (rows, WA_TQ), 1)
    m_prev = c >= r + off_prev
    m_cur = c >= off_cur
    m_next = c + off_next <= r
    grp = lax.broadcasted_iota(jnp.int32, (rows, 1), 0) // WA_TQ

    for h in range(WIN_HKV):
        q = jnp.concatenate([q_ref[:, (h * WIN_G + g) * HD:(h * WIN_G + g + 1) * HD] for g in range(WIN_G)], axis=0)
        ks = slice(h * HD, (h + 1) * HD)
        vs = slice(W_WIN_KV + h * HD, W_WIN_KV + (h + 1) * HD)
        s_p = jnp.where(m_prev, _dot_nt(q, kp_ref[:, ks]) * ATT_SCALE, NEG)
        s_c = jnp.where(m_cur, _dot_nt(q, kc_ref[:, ks]) * ATT_SCALE, NEG)
        s_n = jnp.where(m_next, _dot_nt(q, kn_ref[:, ks]) * ATT_SCALE, NEG)
        s_x = _dot_nt(q, kx_ref[:, ks]) * ATT_SCALE
        snk = jnp.zeros((rows, 1), F32)
        for g in range(WIN_G):
            snk = jnp.where(grp == g, sink_ref[h * WIN_G + g], snk)
        m = jnp.maximum(jnp.maximum(jnp.max(s_p, axis=-1, keepdims=True), jnp.max(s_c, axis=-1, keepdims=True)),
                        jnp.maximum(jnp.max(s_n, axis=-1, keepdims=True), jnp.max(s_x, axis=-1, keepdims=True)))
        m = jnp.maximum(m, snk)
        p_p = jnp.exp(s_p - m)
        p_c = jnp.exp(s_c - m)
        p_n = jnp.exp(s_n - m)
        p_x = jnp.exp(s_x - m)
        den = (jnp.sum(p_p, axis=-1, keepdims=True) + jnp.sum(p_c, axis=-1, keepdims=True)
               + jnp.sum(p_n, axis=-1, keepdims=True) + jnp.sum(p_x, axis=-1, keepdims=True) + jnp.exp(snk - m))
        o = (jnp.dot(p_p.astype(BF16), kp_ref[:, vs], preferred_element_type=F32)
             + jnp.dot(p_c.astype(BF16), kc_ref[:, vs], preferred_element_type=F32)
             + jnp.dot(p_n.astype(BF16), kn_ref[:, vs], preferred_element_type=F32)
             + jnp.dot(p_x.astype(BF16), kx_ref[:, vs], preferred_element_type=F32))
        o = o / den
        for g in range(WIN_G):
            o_ref[:, (h * WIN_G + g) * HD:(h * WIN_G + g + 1) * HD] = o[g * WA_TQ:(g + 1) * WA_TQ].astype(BF16)


def _win_attention(qw, kvw, sink, latent_only=False):
    nt = T // WA_TQ
    t0 = _first_tile(WA_TQ, latent_only)

    def bounds(i):
        is_ctx = i < WA_CTX_TILES
        b = (i - WA_CTX_TILES) // WA_LAT_TILES
        lo = jnp.where(is_ctx, 0, WA_CTX_TILES + b * WA_LAT_TILES)
        hi = jnp.where(is_ctx, nt - 1, WA_CTX_TILES + (b + 1) * WA_LAT_TILES - 1)
        return lo, hi

    def prev_map(g):
        i = g + t0
        lo, _ = bounds(i)
        return (jnp.maximum(i - 1, lo), 0)

    def next_map(g):
        i = g + t0
        _, hi = bounds(i)
        return (jnp.minimum(i + 1, hi), 0)

    def ctx_map(g):
        i = g + t0
        b = jnp.where(i < WA_CTX_TILES, i // (L // WA_TQ), (i - WA_CTX_TILES) // WA_LAT_TILES)
        return (b, 0)

    kvw_w = 2 * W_WIN_KV
    return pl.pallas_call(
        functools.partial(_win_attn_kernel, t0),
        grid=(nt - t0,),
        in_specs=[
            pl.BlockSpec(memory_space=pltpu.SMEM),
            pl.BlockSpec((WA_TQ, W_WIN_Q), lambda g: (g + t0, 0)),
            pl.BlockSpec((WA_TQ, kvw_w), prev_map),
            pl.BlockSpec((WA_TQ, kvw_w), lambda g: (g + t0, 0)),
            pl.BlockSpec((WA_TQ, kvw_w), next_map),
            pl.BlockSpec((L, kvw_w), ctx_map),
        ],
        out_specs=pl.BlockSpec((WA_TQ, W_WIN_Q), lambda g: (g + t0, 0)),
        out_shape=jax.ShapeDtypeStruct((T, W_WIN_Q), BF16),
        compiler_params=_cparams("arbitrary"),
        name="window_attention",
    )(sink, qw, kvw, kvw, kvw, kvw)


NA_TQ = 256
NA_QROWS = NA_TQ // GRID_W
NA_KROWS = 3 * NA_QROWS
NA_NKEY = NA_KROWS * GRID_W
NA_TYPES = 4
RPB_R = 2 * NA_KH - 1
RPB_C = 2 * NA_KW - 1


def _na_row_valid(ty, a, j):
    if ty == 0:
        return NA_QROWS <= j < NA_QROWS + NA_KH
    if ty == 1:
        return a <= j < a + NA_KH
    if ty == 2:
        return j < NA_KH
    return False


def _rpb_kernel(rpb_ref, o_ref):
    h = pl.program_id(0)
    shp = (GRID_W, 2 * GRID_W)
    qc = lax.broadcasted_iota(jnp.int32, shp, 0)
    lane = lax.broadcasted_iota(jnp.int32, shp, 1)
    kc = lane % GRID_W
    second = lane >= GRID_W
    dcol = kc - qc + (NA_KW - 1)
    cs = jnp.clip(qc - NA_KW // 2, 0, GRID_W - NA_KW)
    colmask = jnp.logical_and(kc >= cs, kc < cs + NA_KW)
    neg = jnp.full(shp, NEG, F32)
    base = h * (RPB_R * RPB_C)
    pair = []
    for dr in range(RPB_R - 1):
        acc = jnp.zeros(shp, F32)
        for dd in range(RPB_C):
            v0 = rpb_ref[base + dr * RPB_C + dd]
            v1 = rpb_ref[base + (dr + 1) * RPB_C + dd]
            acc = jnp.where(dcol == dd, jnp.where(second, v1, v0), acc)
        pair.append(jnp.where(colmask, acc, neg))
    for ty in range(NA_TYPES):
        for a in range(NA_QROWS):
            for jp in range(NA_KROWS // 2):
                j = 2 * jp
                ok0 = _na_row_valid(ty, a, j)
                ok1 = _na_row_valid(ty, a, j + 1)
                dr = j - a + NA_QROWS - 1
                if ok0 and ok1:
                    tile = pair[dr]
                elif ok0:
                    tile = jnp.where(second, neg, pair[dr])
                elif ok1:
                    tile = jnp.where(second, pair[dr], neg)
                else:
                    tile = neg
                o_ref[ty, 0, a * GRID_W:(a + 1) * GRID_W, jp * 2 * GRID_W:(jp + 1) * 2 * GRID_W] = tile


def _rpb_tiles(rpb):
    return pl.pallas_call(
        _rpb_kernel,
        grid=(NA_H,),
        in_specs=[pl.BlockSpec(memory_space=pltpu.SMEM)],
        out_specs=pl.BlockSpec((NA_TYPES, 1, NA_TQ, NA_NKEY), lambda h: (0, h, 0, 0)),
        out_shape=jax.ShapeDtypeStruct((NA_TYPES, NA_H, NA_TQ, NA_NKEY), F32),
        compiler_params=_cparams("arbitrary"),
        name="rpb_tiles",
    )(rpb.reshape(-1))


NA_CTX_TILES = CTX_ROWS // NA_TQ
NA_LAT_TILES = S // NA_TQ


def _na_attn_kernel(q_ref, kp_ref, kc_ref, kn_ref, vp_ref, vc_ref, vn_ref, kx_ref, vx_ref, bias_ref, o_ref):
    for h in range(NA_H):
        hs = slice(h * HD, (h + 1) * HD)
        q = q_ref[:, hs]
        s_p = _dot_nt(q, kp_ref[:, hs]) * ATT_SCALE + bias_ref[0, h, :, 0:NA_TQ]
        s_c = _dot_nt(q, kc_ref[:, hs]) * ATT_SCALE + bias_ref[0, h, :, NA_TQ:2 * NA_TQ]
        s_n = _dot_nt(q, kn_ref[:, hs]) * ATT_SCALE + bias_ref[0, h, :, 2 * NA_TQ:3 * NA_TQ]
        s_x = _dot_nt(q, kx_ref[:, hs]) * ATT_SCALE
        m = jnp.maximum(jnp.maximum(jnp.max(s_p, axis=-1, keepdims=True), jnp.max(s_c, axis=-1, keepdims=True)),
                        jnp.maximum(jnp.max(s_n, axis=-1, keepdims=True), jnp.max(s_x, axis=-1, keepdims=True)))
        p_p = jnp.exp(s_p - m)
        p_c = jnp.exp(s_c - m)
        p_n = jnp.exp(s_n - m)
        p_x = jnp.exp(s_x - m)
        den = (jnp.sum(p_p, axis=-1, keepdims=True) + jnp.sum(p_c, axis=-1, keepdims=True)
               + jnp.sum(p_n, axis=-1, keepdims=True) + jnp.sum(p_x, axis=-1, keepdims=True))
        o = (jnp.dot(p_p.astype(BF16), vp_ref[:, hs], preferred_element_type=F32)
             + jnp.dot(p_c.astype(BF16), vc_ref[:, hs], preferred_element_type=F32)
             + jnp.dot(p_n.astype(BF16), vn_ref[:, hs], preferred_element_type=F32)
             + jnp.dot(p_x.astype(BF16), vx_ref[:, hs], preferred_element_type=F32))
        o_ref[:, hs] = (o / den).astype(BF16)


def _na_attention(qn, kn, vn, bias, latent_only=False):
    nt = T // NA_TQ
    t0 = _first_tile(NA_TQ, latent_only)

    def bounds(i):
        is_ctx = i < NA_CTX_TILES
        b = (i - NA_CTX_TILES) // NA_LAT_TILES
        lo = jnp.where(is_ctx, 0, NA_CTX_TILES + b * NA_LAT_TILES)
        hi = jnp.where(is_ctx, nt - 1, NA_CTX_TILES + (b + 1) * NA_LAT_TILES - 1)
        return lo, hi

    def prev_map(g):
        i = g + t0
        lo, _ = bounds(i)
        return (jnp.maximum(i - 1, lo), 0)

    def next_map(g):
        i = g + t0
        _, hi = bounds(i)
        return (jnp.minimum(i + 1, hi), 0)

    def ctx_map(g):
        i = g + t0
        return (jnp.where(i < NA_CTX_TILES, i, (i - NA_CTX_TILES) // NA_LAT_TILES), 0)

    def bias_map(g):
        i = g + t0
        n = (i - NA_CTX_TILES) % NA_LAT_TILES
        ty = jnp.where(i < NA_CTX_TILES, 3, jnp.where(n == 0, 0, jnp.where(n == NA_LAT_TILES - 1, 2, 1)))
        return (ty, 0, 0, 0)

    w = W_NA_KV
    cur = pl.BlockSpec((NA_TQ, w), lambda g: (g + t0, 0))
    prv = pl.BlockSpec((NA_TQ, w), prev_map)
    nxt = pl.BlockSpec((NA_TQ, w), next_map)
    ctx = pl.BlockSpec((L, w), ctx_map)
    return pl.pallas_call(
        _na_attn_kernel,
        grid=(nt - t0,),
        in_specs=[cur, prv, cur, nxt, prv, cur, nxt, ctx, ctx,
                  pl.BlockSpec((1, NA_H, NA_TQ, NA_NKEY), bias_map)],
        out_specs=cur,
        out_shape=jax.ShapeDtypeStruct((T, W_NA_Q), BF16),
        compiler_params=_cparams("arbitrary"),
        name="neighborhood_attention",
    )(qn, kn, kn, kn, vn, vn, vn, kn, vn, bias)


MG_TM = 512
MG_TN = 512


def _merge_kernel(ca_ref, aw_ref, an_ref, wc_ref, ww_ref, wn_ref, ga_ref, gb_ref, gc_ref, o_ref):
    ya = jnp.dot(ca_ref[...], wc_ref[...], preferred_element_type=F32)
    yb = jnp.dot(aw_ref[...], ww_ref[...], preferred_element_type=F32)
    yc = jnp.dot(an_ref[...], wn_ref[...], preferred_element_type=F32)
    o = _sigmoid(ga_ref[...]) * ya + _sigmoid(gb_ref[...]) * yb + _sigmoid(gc_ref[...]) * yc
    o_ref[...] = o.astype(BF16)


def _merge(hconv, aw, an, w_conv_out, w_win_out, w_na_out, u, latent_only=False):
    gb0 = GATE_OFF // MG_TN
    gstep = D // MG_TN
    t0 = _first_tile(MG_TM, latent_only)

    def a_spec(k):
        return pl.BlockSpec((MG_TM, k), lambda i, j: (i + t0, 0))

    def w_spec(k):
        return pl.BlockSpec((k, MG_TN), lambda i, j: (0, j))

    def g_spec(which):
        return pl.BlockSpec((MG_TM, MG_TN), lambda i, j: (i + t0, gb0 + which * gstep + j))

    return pl.pallas_call(
        _merge_kernel,
        grid=(T // MG_TM - t0, D // MG_TN),
        in_specs=[a_spec(CONV_CH), a_spec(W_WIN_Q), a_spec(W_NA_Q),
                  w_spec(CONV_CH), w_spec(W_WIN_Q), w_spec(W_NA_Q),
                  g_spec(0), g_spec(1), g_spec(2)],
        out_specs=pl.BlockSpec((MG_TM, MG_TN), lambda i, j: (i + t0, j)),
        out_shape=jax.ShapeDtypeStruct((T, D), BF16),
        compiler_params=_cparams("arbitrary", "arbitrary"),
        name="gated_merge",
    )(hconv, aw, an, w_conv_out, w_win_out, w_na_out, u, u, u)


TK_TT = 256


def _topk_rounds(scores, n_rounds):
    nrow = scores.shape[0]
    idx = lax.broadcasted_iota(jnp.int32, scores.shape, 0).astype(F32)
    work = scores
    rank = jnp.full(scores.shape, float(n_rounds), F32)
    vals = []
    for a in range(n_rounds):
        m = jnp.max(work, axis=0, keepdims=True)
        first = jnp.min(jnp.where(work == m, idx, float(nrow)), axis=0, keepdims=True)
        sel = idx == first
        rank = jnp.where(sel, float(a), rank)
        work = jnp.where(sel, -jnp.inf, work)
        vals.append(m)
    return jnp.concatenate(vals, axis=0), rank


def _topk_rounds_no_ties(scores, n_rounds):
    work = scores
    rank = jnp.full(scores.shape, float(n_rounds), F32)
    vals = []
    for a in range(n_rounds):
        m = jnp.max(work, axis=0, keepdims=True)
        sel = work == m
        rank = jnp.where(sel, float(a), rank)
        work = jnp.where(sel, -jnp.inf, work)
        vals.append(m)
    taken = jnp.sum(jnp.where(rank < float(n_rounds), 1.0, 0.0), axis=0, keepdims=True)
    return jnp.concatenate(vals, axis=0), rank, taken


def _peer_topk_kernel(q_ref, k1_ref, k2_ref, n_ref, e1_ref, r2_ref, e2_ref):
    kk = PEER_TOPK

    def head(h, carry):
        c1 = pl.multiple_of(h * 2 * HD, 2 * HD)
        q1 = q_ref[:, pl.ds(c1, HD)]
        q2 = q_ref[:, pl.ds(c1 + HD, HD)]
        s1 = _dot_nt(k1_ref[h].astype(BF16), q1)
        s2 = _dot_nt(k2_ref[h].astype(BF16), q2)
        f1, fr1, t1 = _topk_rounds_no_ties(s1, kk)
        f2, fr2, t2 = _topk_rounds_no_ties(s2, kk)
        tied = jnp.max(jnp.abs(t1 - float(kk)) + jnp.abs(t2 - float(kk))) > 0.0

        def exact(_):
            return _topk_rounds(s1, kk) + _topk_rounds(s2, kk)

        def keep(_):
            return f1, fr1, f2, fr2

        v1all, rank1, v2all, rank2 = lax.cond(tied, exact, keep, None)
        v1 = [v1all[0:1, :]]
        v2 = [v2all[0:1, :]]
        arow = lax.broadcasted_iota(jnp.int32, v2all.shape, 0).astype(F32)
        cnt = jnp.zeros(v2all.shape, F32)
        front = v1all + v2[0]
        top = v1[0] + v2[0]
        z = jnp.zeros_like(top)
        for _ in range(kk):
            m = jnp.max(front, axis=0, keepdims=True)
            first = jnp.min(jnp.where(front == m, arow, float(kk)), axis=0, keepdims=True)
            sel = arow == first
            cnt = cnt + jnp.where(sel, 1.0, 0.0)
            z = z + jnp.exp(m - top)
            taken = jnp.max(jnp.where(sel, cnt, -1.0), axis=0, keepdims=True)
            nxt = jnp.max(jnp.where(arow == taken, v2all, -jnp.inf), axis=0, keepdims=True)
            front = jnp.where(sel, v1all + nxt, front)
        nfull = jnp.zeros(s1.shape, F32)
        for a in range(kk):
            nfull = jnp.where(rank1 == float(a), cnt[a:a + 1, :], nfull)
        n_ref[h] = nfull
        e1_ref[h] = jnp.exp(s1 - v1[0]) / z
        r2_ref[h] = rank2
        e2_ref[h] = jnp.exp(s2 - v2[0])
        return carry

    lax.fori_loop(0, PEER_HEADS, head, 0)


def _peer_topk(q, k1, k2, latent_only=False):
    tt = TK_TT
    t0 = _first_tile(tt, latent_only)
    kspec = pl.BlockSpec((PEER_HEADS, N_KEYS, HD), lambda i: (0, 0, 0))
    ospec = pl.BlockSpec((PEER_HEADS, N_KEYS, tt), lambda i: (0, 0, i + t0))
    oshape = jax.ShapeDtypeStruct((PEER_HEADS, N_KEYS, T), F32)
    return pl.pallas_call(
        _peer_topk_kernel,
        grid=(T // tt - t0,),
        in_specs=[pl.BlockSpec((tt, 2 * HD * PEER_HEADS), lambda i: (i + t0, 0)), kspec, kspec],
        out_specs=[ospec] * 4,
        out_shape=[oshape] * 4,
        compiler_params=_cparams("arbitrary"),
        name="peer_topk",
    )(q, k1, k2)


EX_TM = 512
EX_TN = 512
EX_SUB = 256
EX_NC = 1024
EX_LC = 128
EX_RC = 32


def _gelu_tanh(x):
    cdf = 0.5 * (1.0 + jnp.tanh(math.sqrt(2.0 / math.pi) * (x + 0.044715 * (x * x * x))))
    return x * cdf


def _experts_kernel(h_ref, u_ref, v_ref, n_ref, e1_ref, r2_ref, e2_ref, o_ref):
    j = pl.program_id(1)

    @pl.when(j == 0)
    def _():
        o_ref[...] = jnp.zeros_like(o_ref)

    def gate_tile(i1, rows, cs):
        gate = jnp.zeros((EX_RC, EX_LC), F32)
        for hh in range(PEER_HEADS):
            nrow = n_ref[i1, hh:hh + 1, cs]
            e1row = e1_ref[i1, hh:hh + 1, cs]
            gate = gate + jnp.where(r2_ref[hh, rows, cs] < nrow, e2_ref[hh, rows, cs], 0.0) * e1row
        return gate

    w_parts = []
    for s in range(EX_TN // EX_SUB):
        es = slice(s * EX_SUB, (s + 1) * EX_SUB)
        act = _gelu_tanh(jnp.dot(u_ref[es, :], h_ref[...], preferred_element_type=F32))
        for ii in range(EX_SUB // N_KEYS):
            i1 = j * (EX_TN // N_KEYS) + s * (EX_SUB // N_KEYS) + ii
            for rc in range(N_KEYS // EX_RC):
                rows = slice(rc * EX_RC, (rc + 1) * EX_RC)
                arows = slice(ii * N_KEYS + rc * EX_RC, ii * N_KEYS + (rc + 1) * EX_RC)
                row = []
                for c in range(EX_TM // EX_LC):
                    cs = slice(c * EX_LC, (c + 1) * EX_LC)
                    row.append((gate_tile(i1, rows, cs) * act[arows, cs]).astype(BF16))
                w_parts.append(jnp.concatenate(row, axis=1))
    w_t = jnp.concatenate(w_parts, axis=0)
    for nc in range(D // EX_NC):
        ns = slice(nc * EX_NC, (nc + 1) * EX_NC)
        o_ref[:, ns] += lax.dot_general(w_t, v_ref[:, ns], (((0,), (0,)), ((), ())),
                                        preferred_element_type=F32)


def _experts(h_t, u, v, tables, latent_only=False):
    n, e1, r2, e2 = tables
    n = jnp.transpose(n, (1, 0, 2))
    e1 = jnp.transpose(e1, (1, 0, 2))
    t0 = _first_tile(EX_TM, latent_only)
    kspec = pl.BlockSpec((N_KEYS, PEER_HEADS, EX_TM), lambda i, j: (0, 0, i + t0))
    tspec = pl.BlockSpec((PEER_HEADS, N_KEYS, EX_TM), lambda i, j: (0, 0, i + t0))
    return pl.pallas_call(
        _experts_kernel,
        grid=(T // EX_TM - t0, N_EXPERTS // EX_TN),
        in_specs=[pl.BlockSpec((D, EX_TM), lambda i, j: (0, i + t0)),
                  pl.BlockSpec((EX_TN, D), lambda i, j: (j, 0)),
                  pl.BlockSpec((EX_TN, D), lambda i, j: (j, 0)),
                  kspec, kspec, tspec, tspec],
        out_specs=pl.BlockSpec((EX_TM, D), lambda i, j: (i + t0, 0), pipeline_mode=pl.Buffered(1)),
        out_shape=jax.ShapeDtypeStruct((T, D), F32),
        compiler_params=_cparams("arbitrary", "arbitrary"),
        name="peer_experts",
    )(h_t, u, v, n, e1, r2, e2)


FR_TM = 256


def _final_kernel(x_ref, f_ref, g_ref, o_ref):
    o_ref[...] = x_ref[...] + g_ref[0, 0] * f_ref[...]


def _final_residual(x, f, modt, layer, g_k):
    off = CTX_ROWS // FR_TM
    lat = pl.BlockSpec((FR_TM, D), lambda i: (i + off, 0))
    return pl.pallas_call(
        _final_kernel,
        grid=(B * S // FR_TM,),
        in_specs=[lat, lat,
                  pl.BlockSpec((1, 1, 1, D), lambda i: (layer, _group_of_row((i + off) * FR_TM), 0, g_k))],
        out_specs=pl.BlockSpec((FR_TM, D), lambda i: (i, 0)),
        out_shape=jax.ShapeDtypeStruct((B * S, D), F32),
        compiler_params=_cparams("arbitrary"),
        name="final_residual",
    )(x, f, modt)


SH1, SC1, G1, SH2, SC2, G2 = range(6)


def kernel(x, c, ctx, c_ctx, w_ada, b_ada, norm1_g, norm2_g, w_in, conv_w, conv_b, conv_ln_g, conv_ln_b,
           w_conv_out, win_qn_g, win_kn_g, win_sink, w_win_out, na_qn_g, na_kn_g, na_rpb, w_na_out, w_out,
           peer_wq, peer_k1, peer_k2, peer_u, peer_v):
    xs = jnp.concatenate([ctx.reshape(CTX_ROWS, D), x.reshape(B * S, D)], axis=0)
    cvec = jnp.concatenate([c_ctx[None], c, jnp.zeros((N_GROUPS - 1 - B, D), F32)], axis=0)
    modt = _mods(cvec, w_ada, b_ada).reshape(DEPTH, N_GROUPS, 1, 6 * D)
    cosf, sinf = _rope_tables()

    f = None
    for l in range(DEPTH):
        if l == 0:
            h1 = _normmod(xs, modt, l, norm1_g[l], SC1, SH1)
        else:
            xs, h1 = _normmod(xs, modt, l, norm1_g[l], SC1, SH1, f=f, g_k=G2)
        lat = l == DEPTH - 1
        u = _matmul(h1, _cast_bf16(w_in, l), F32, 1024)
        hconv = _conv_module(u, conv_w[l], conv_b[l], conv_ln_g[l], conv_ln_b[l], latent_only=lat)
        gains = jnp.concatenate([win_qn_g[l][None], win_kn_g[l][None], na_qn_g[l][None], na_kn_g[l][None],
                                 jnp.zeros((4, HD), F32)], axis=0)
        qw, qn, kvw, kn, vn = _headnorm(u, cosf, sinf, gains)
        aw = _win_attention(qw, kvw, win_sink[l], latent_only=lat)
        an = _na_attention(qn, kn, vn, _rpb_tiles(na_rpb[l]), latent_only=lat)
        merged = _merge(hconv, aw, an, _cast_bf16(w_conv_out, l), _cast_bf16(w_win_out, l),
                        _cast_bf16(w_na_out, l), u, latent_only=lat)
        xs = _matmul_resid(merged, _cast_bf16(w_out, l), xs, modt, l, G1, 1024, latent_only=lat)
        h2, h2_t = _normmod(xs, modt, l, norm2_g[l], SC2, SH2, want_t=True, latent_only=lat)
        q = _matmul(h2, _cast_bf16(peer_wq, l), BF16, 1024, latent_only=lat)
        tables = _peer_topk(q, peer_k1[l], peer_k2[l], latent_only=lat)
        f = _experts(h2_t, _cast_bf16(peer_u, l), _cast_bf16(peer_v, l), tables, latent_only=lat)
    out = _final_residual(xs, f, modt, DEPTH - 1, G2)
    return out.reshape(B, S, D)
```

```python
import functools
import math

import jax
import jax.numpy as jnp
import numpy as np
from jax import lax
from jax.experimental import pallas as pl
from jax.experimental.pallas import tpu as pltpu

F32 = jnp.float32
BF16 = jnp.bfloat16

D = 4096
B = 2
S = 4096
L = 256
DEPTH = 2
GRID_W = 64
HD = 128
EPS = 1e-6
NEG = -1e30
ROPE_BASE = 10000.0

CONV_CH = D // 4
CONV_K = 31
WIN_HQ = 16
WIN_HKV = 4
WIN_G = WIN_HQ // WIN_HKV
NA_H = 8
NA_KH = 8
NA_KW = 16
PEER_HEADS = 8
N_KEYS = 128
N_EXPERTS = N_KEYS * N_KEYS
PEER_TOPK = 16

W_CONV_IN = 2 * CONV_CH
W_WIN_Q = WIN_HQ * HD
W_NA_Q = NA_H * HD
W_GATE = 3 * D
W_WIN_KV = WIN_HKV * HD
W_NA_KV = NA_H * HD
Q_WIN_OFF = W_CONV_IN
Q_NA_OFF = Q_WIN_OFF + W_WIN_Q
GATE_OFF = Q_NA_OFF + W_NA_Q
KV_OFF = GATE_OFF + W_GATE
IN_COLS = KV_OFF + 2 * W_WIN_KV + 2 * W_NA_KV

CTX_ROWS = B * L
T = CTX_ROWS + B * S
N_GROUPS = 8
ATT_SCALE = HD ** -0.5

VMEM_LIMIT = 56 * 1024 * 1024


def _cparams(*sem):
    return pltpu.CompilerParams(dimension_semantics=sem, vmem_limit_bytes=VMEM_LIMIT)


def _group_of_row(row0):
    return jnp.where(row0 < CTX_ROWS, 0, 1 + (row0 - CTX_ROWS) // S)


def _sigmoid(z):
    return 0.5 * jnp.tanh(0.5 * z) + 0.5


def _dot_nt(a, b):
    return lax.dot_general(a, b, (((1,), (1,)), ((), ())), preferred_element_type=F32)


def _first_tile(tile, latent_only):
    return CTX_ROWS // tile if latent_only else 0


CAST_BLOCK_BYTES = 8 * 1024 * 1024


def _cast_kernel(w_ref, o_ref):
    o_ref[...] = w_ref[0].astype(BF16)


def _cast_bf16(w, layer):
    _, m, n = w.shape
    rb = m
    while rb * n * 4 > CAST_BLOCK_BYTES:
        rb //= 2
    return pl.pallas_call(
        _cast_kernel,
        grid=(m // rb,),
        in_specs=[pl.BlockSpec((1, rb, n), lambda i: (layer, i, 0))],
        out_specs=pl.BlockSpec((rb, n), lambda i: (i, 0)),
        out_shape=jax.ShapeDtypeStruct((m, n), BF16),
        compiler_params=_cparams("arbitrary"),
        name="cast_bf16",
    )(w)


MOD_TN = 512


def _mods_kernel(c_ref, w_ref, b_ref, o_ref):
    cv = c_ref[...]
    a = (cv * _sigmoid(cv)).astype(BF16)
    w = w_ref[0].astype(BF16)
    o_ref[0] = jnp.dot(a, w, preferred_element_type=F32) + b_ref[0]


def _mods(cvec, w_ada, b_ada):
    n = 6 * D
    return pl.pallas_call(
        _mods_kernel,
        grid=(DEPTH, n // MOD_TN),
        in_specs=[
            pl.BlockSpec((N_GROUPS, D), lambda l, j: (0, 0)),
            pl.BlockSpec((1, D, MOD_TN), lambda l, j: (l, 0, j)),
            pl.BlockSpec((1, 1, MOD_TN), lambda l, j: (l, 0, j)),
        ],
        out_specs=pl.BlockSpec((1, N_GROUPS, MOD_TN), lambda l, j: (l, 0, j)),
        out_shape=jax.ShapeDtypeStruct((DEPTH, N_GROUPS, n), F32),
        compiler_params=_cparams("arbitrary", "arbitrary"),
        name="adaln_mods",
    )(cvec, w_ada, b_ada.reshape(DEPTH, 1, n))


NM_TM = 256


def _normmod_kernel(has_f, want_t, *refs):
    refs = list(refs)
    ht_ref = refs.pop() if want_t else None
    if has_f:
        x_ref, f_ref, g_ref, ng_ref, sc_ref, sh_ref, xo_ref, h_ref = refs
        x = x_ref[...] + g_ref[0, 0] * f_ref[...]
        xo_ref[...] = x
    else:
        x_ref, ng_ref, sc_ref, sh_ref, h_ref = refs
        x = x_ref[...]
    ms = jnp.mean(x * x, axis=-1, keepdims=True)
    y = x * lax.rsqrt(ms + EPS) * ng_ref[...]
    h = y * (1.0 + sc_ref[0, 0]) + sh_ref[0, 0]
    h_ref[...] = h.astype(BF16)
    if want_t:
        ht_ref[...] = h.T.astype(BF16)


def _mod_spec(layer, k, tm, t0=0):
    return pl.BlockSpec((1, 1, 1, D), lambda i: (layer, _group_of_row((i + t0) * tm), 0, k))


def _normmod(x, modt, layer, norm_g, sc_k, sh_k, f=None, g_k=None, want_t=False, latent_only=False):
    tm = NM_TM
    t0 = _first_tile(tm, latent_only)
    row = pl.BlockSpec((tm, D), lambda i: (i + t0, 0))
    has_f = f is not None
    in_specs = [row]
    args = [x]
    if has_f:
        in_specs += [row, _mod_spec(layer - 1, g_k, tm, t0)]
        args += [f, modt]
    in_specs += [pl.BlockSpec((1, D), lambda i: (0, 0)), _mod_spec(layer, sc_k, tm, t0),
                 _mod_spec(layer, sh_k, tm, t0)]
    args += [norm_g.reshape(1, D), modt, modt]
    h_shape = jax.ShapeDtypeStruct((T, D), BF16)
    if has_f:
        out_specs = [row, row]
        out_shape = [jax.ShapeDtypeStruct((T, D), F32), h_shape]
    else:
        out_specs = [row]
        out_shape = [h_shape]
    if want_t:
        out_specs = out_specs + [pl.BlockSpec((D, tm), lambda i: (0, i + t0))]
        out_shape = out_shape + [jax.ShapeDtypeStruct((D, T), BF16)]
    outs = pl.pallas_call(
        functools.partial(_normmod_kernel, has_f, want_t),
        grid=(T // tm - t0,),
        in_specs=in_specs,
        out_specs=out_specs,
        out_shape=out_shape,
        compiler_params=_cparams("arbitrary"),
        name="normmod_resid" if has_f else "normmod",
    )(*args)
    return outs[0] if len(outs) == 1 else tuple(outs)


MM_TM = 512


def _mm_kernel(a_ref, w_ref, o_ref):
    o_ref[...] = jnp.dot(a_ref[...], w_ref[...], preferred_element_type=F32).astype(o_ref.dtype)


def _mm_resid_kernel(a_ref, w_ref, r_ref, g_ref, o_ref):
    acc = jnp.dot(a_ref[...], w_ref[...], preferred_element_type=F32)
    o_ref[...] = r_ref[...] + g_ref[0, 0] * acc


def _mm_wstat_kernel(a_ref, w_ref, o_ref, wb_ref):
    @pl.when(pl.program_id(1) == 0)
    def _():
        wb_ref[...] = w_ref[0].astype(BF16)

    o_ref[...] = jnp.dot(a_ref[...], wb_ref[...], preferred_element_type=F32).astype(o_ref.dtype)


def _matmul_wstat(a, w, layer, out_dtype, tn):
    m, k = a.shape
    n = w.shape[2]
    return pl.pallas_call(
        _mm_wstat_kernel,
        grid=(n // tn, m // MM_TM),
        in_specs=[pl.BlockSpec((MM_TM, k), lambda j, i: (i, 0)),
                  pl.BlockSpec((1, k, tn), lambda j, i: (layer, 0, j))],
        out_specs=pl.BlockSpec((MM_TM, tn), lambda j, i: (i, j)),
        out_shape=jax.ShapeDtypeStruct((m, n), out_dtype),
        scratch_shapes=[pltpu.VMEM((k, tn), BF16)],
        compiler_params=_cparams("arbitrary", "arbitrary"),
        name="matmul_wstat",
    )(a, w)


def _matmul(a, w, out_dtype, tn, latent_only=False):
    m, k = a.shape
    n = w.shape[1]
    t0 = _first_tile(MM_TM, latent_only)
    return pl.pallas_call(
        _mm_kernel,
        grid=(m // MM_TM - t0, n // tn),
        in_specs=[pl.BlockSpec((MM_TM, k), lambda i, j: (i + t0, 0)),
                  pl.BlockSpec((k, tn), lambda i, j: (0, j))],
        out_specs=pl.BlockSpec((MM_TM, tn), lambda i, j: (i + t0, j)),
        out_shape=jax.ShapeDtypeStruct((m, n), out_dtype),
        compiler_params=_cparams("arbitrary", "arbitrary"),
        name="matmul",
    )(a, w)


def _matmul_resid(a, w, resid, modt, layer, g_k, tn, latent_only=False):
    m, k = a.shape
    n = w.shape[1]
    nj = n // tn
    t0 = _first_tile(MM_TM, latent_only)
    return pl.pallas_call(
        _mm_resid_kernel,
        grid=(m // MM_TM - t0, nj),
        in_specs=[pl.BlockSpec((MM_TM, k), lambda i, j: (i + t0, 0)),
                  pl.BlockSpec((k, tn), lambda i, j: (0, j)),
                  pl.BlockSpec((MM_TM, tn), lambda i, j: (i + t0, j)),
                  pl.BlockSpec((1, 1, 1, tn),
                               lambda i, j: (layer, _group_of_row((i + t0) * MM_TM), 0, g_k * nj + j))],
        out_specs=pl.BlockSpec((MM_TM, tn), lambda i, j: (i + t0, j)),
        out_shape=jax.ShapeDtypeStruct((m, n), F32),
        compiler_params=_cparams("arbitrary", "arbitrary"),
        name="matmul_resid",
    )(a, w, resid, modt)


CV_TM = 256
CV_HALO = 16
CV_RC = 64
CV_CC = 128


def _conv_kernel(t0, prev_ref, cur_ref, next_ref, w_ref, cb_ref, lg_ref, lb_ref, o_ref, buf_ref, acc_ref):
    i = pl.program_id(0) + t0
    row0 = i * CV_TM
    lat = row0 - CTX_ROWS
    seq_start = jnp.where(row0 < CTX_ROWS, row0 % L == 0, lat % S == 0)
    seq_end = jnp.where(row0 < CTX_ROWS, (row0 + CV_TM) % L == 0, (lat + CV_TM) % S == 0)

    def glu(u):
        return u[:, :CONV_CH] * _sigmoid(u[:, CONV_CH:])

    buf_ref[0:CV_HALO, :] = glu(prev_ref[...]) * jnp.where(seq_start, 0.0, 1.0)
    buf_ref[CV_HALO:CV_HALO + CV_TM, :] = glu(cur_ref[...])
    buf_ref[CV_HALO + CV_TM:, :] = glu(next_ref[...]) * jnp.where(seq_end, 0.0, 1.0)

    tap0 = CV_HALO - CONV_K // 2

    def col_chunk(c, carry):
        c0 = pl.multiple_of(c * CV_CC, CV_CC)
        for r in range(CV_TM // CV_RC):
            acc = jnp.zeros((CV_RC, CV_CC), F32)
            for k in range(CONV_K):
                r0 = r * CV_RC + k + tap0
                acc = acc + w_ref[k:k + 1, pl.ds(c0, CV_CC)] * buf_ref[r0:r0 + CV_RC, pl.ds(c0, CV_CC)]
            acc_ref[r * CV_RC:(r + 1) * CV_RC, pl.ds(c0, CV_CC)] = acc
        return carry

    lax.fori_loop(0, CONV_CH // CV_CC, col_chunk, 0)

    h = acc_ref[...] + cb_ref[...]
    mu = jnp.mean(h, axis=-1, keepdims=True)
    xc = h - mu
    var = jnp.mean(xc * xc, axis=-1, keepdims=True)
    y = xc * lax.rsqrt(var + EPS) * lg_ref[...] + lb_ref[...]
    o_ref[...] = (y * _sigmoid(y)).astype(BF16)


def _conv_module(u, conv_w, conv_b, ln_g, ln_b, latent_only=False):
    hb = CV_TM // CV_HALO
    last = T // CV_HALO - 1
    t0 = _first_tile(CV_TM, latent_only)
    vec = pl.BlockSpec((1, CONV_CH), lambda i: (0, 0))
    return pl.pallas_call(
        functools.partial(_conv_kernel, t0),
        grid=(T // CV_TM - t0,),
        in_specs=[
            pl.BlockSpec((CV_HALO, W_CONV_IN), lambda i: (jnp.maximum((i + t0) * hb - 1, 0), 0)),
            pl.BlockSpec((CV_TM, W_CONV_IN), lambda i: (i + t0, 0)),
            pl.BlockSpec((CV_HALO, W_CONV_IN), lambda i: (jnp.minimum((i + t0 + 1) * hb, last), 0)),
            pl.BlockSpec((CONV_K, CONV_CH), lambda i: (0, 0)),
            vec, vec, vec,
        ],
        out_specs=pl.BlockSpec((CV_TM, CONV_CH), lambda i: (i + t0, 0)),
        out_shape=jax.ShapeDtypeStruct((T, CONV_CH), BF16),
        scratch_shapes=[pltpu.VMEM((CV_TM + 2 * CV_HALO, CONV_CH), F32),
                        pltpu.VMEM((CV_TM, CONV_CH), F32)],
        compiler_params=_cparams("arbitrary"),
        name="conv_module",
    )(u, u, u, conv_w, conv_b.reshape(1, CONV_CH), ln_g.reshape(1, CONV_CH), ln_b.reshape(1, CONV_CH))


HN_TM = 256


def _headnorm_kernel(qw_ref, qn_ref, kvw_ref, kn_ref, vn_ref, cos_ref, sin_ref, g_ref,
                     qw_o, qn_o, kvw_o, kn_o, vn_o):
    cosf = cos_ref[...]
    sinf = sin_ref[...]
    lane = lax.broadcasted_iota(jnp.int32, (HN_TM, HD), 1)
    first_half = (lane % (HD // 2)) < (HD // 4)

    def norm(x, gi):
        ms = jnp.mean(x * x, axis=-1, keepdims=True)
        return x * lax.rsqrt(ms + EPS) * g_ref[gi:gi + 1, :]

    def rope(x):
        partner = jnp.where(first_half, pltpu.roll(x, HD - HD // 4, 1), pltpu.roll(x, HD // 4, 1))
        return x * cosf + partner * sinf

    for hd in range(WIN_HQ):
        sl = slice(hd * HD, (hd + 1) * HD)
        qw_o[:, sl] = rope(norm(qw_ref[:, sl], 0)).astype(BF16)
    for hd in range(WIN_HKV):
        sl = slice(hd * HD, (hd + 1) * HD)
        kvw_o[:, sl] = rope(norm(kvw_ref[:, sl], 1)).astype(BF16)
    kvw_o[:, W_WIN_KV:] = kvw_ref[:, W_WIN_KV:].astype(BF16)
    for hd in range(NA_H):
        sl = slice(hd * HD, (hd + 1) * HD)
        qn_o[:, sl] = norm(qn_ref[:, sl], 2).astype(BF16)
        kn_o[:, sl] = norm(kn_ref[:, sl], 3).astype(BF16)
    vn_o[...] = vn_ref[...].astype(BF16)


def _headnorm(u, cosf, sinf, gains):
    tm = HN_TM
    kvb = KV_OFF // W_NA_KV
    row128 = pl.BlockSpec((tm, HD), lambda i: (i, 0))

    def ospec(w):
        return pl.BlockSpec((tm, w), lambda i: (i, 0))

    def oshape(w):
        return jax.ShapeDtypeStruct((T, w), BF16)

    return pl.pallas_call(
        _headnorm_kernel,
        grid=(T // tm,),
        in_specs=[
            pl.BlockSpec((tm, W_WIN_Q), lambda i: (i, Q_WIN_OFF // W_WIN_Q)),
            pl.BlockSpec((tm, W_NA_Q), lambda i: (i, Q_NA_OFF // W_NA_Q)),
            pl.BlockSpec((tm, 2 * W_WIN_KV), lambda i: (i, kvb)),
            pl.BlockSpec((tm, W_NA_KV), lambda i: (i, kvb + 1)),
            pl.BlockSpec((tm, W_NA_KV), lambda i: (i, kvb + 2)),
            row128, row128,
            pl.BlockSpec((8, HD), lambda i: (0, 0)),
        ],
        out_specs=[ospec(W_WIN_Q), ospec(W_NA_Q), ospec(2 * W_WIN_KV), ospec(W_NA_KV), ospec(W_NA_KV)],
        out_shape=[oshape(W_WIN_Q), oshape(W_NA_Q), oshape(2 * W_WIN_KV), oshape(W_NA_KV), oshape(W_NA_KV)],
        compiler_params=_cparams("arbitrary"),
        name="headnorm_rope",
    )(u, u, u, u, u, cosf, sinf, gains)


def _rope_tables():
    t = jnp.arange(S)
    row = (t // GRID_W).astype(F32)
    col = (t % GRID_W).astype(F32)
    axis_dim = HD // 2
    inv = ROPE_BASE ** (-jnp.arange(0, axis_dim, 2, dtype=F32) / axis_dim)
    ar = row[:, None] * inv
    ac = col[:, None] * inv
    cosl = jnp.concatenate([jnp.cos(ar), jnp.cos(ar), jnp.cos(ac), jnp.cos(ac)], axis=1)
    sinl = jnp.concatenate([-jnp.sin(ar), jnp.sin(ar), -jnp.sin(ac), jnp.sin(ac)], axis=1)
    cosf = jnp.concatenate([jnp.ones((CTX_ROWS, HD), F32), cosl, cosl], axis=0)
    sinf = jnp.concatenate([jnp.zeros((CTX_ROWS, HD), F32), sinl, sinl], axis=0)
    return cosf, sinf


WA_TQ = 128
WA_CTX_TILES = CTX_ROWS // WA_TQ
WA_LAT_TILES = S // WA_TQ


def _win_attn_kernel(t0, sink_ref, q_ref, kp_ref, kc_ref, kn_ref, kx_ref, o_ref):
    i = pl.program_id(0) + t0
    is_ctx = i < WA_CTX_TILES
    n = (i - WA_CTX_TILES) % WA_LAT_TILES
    far = 4 * WA_TQ
    off_prev = jnp.where(jnp.logical_or(is_ctx, n == 0), far, 0)
    off_cur = jnp.where(is_ctx, far, 0)
    off_next = jnp.where(jnp.logical_or(is_ctx, n == WA_LAT_TILES - 1), far, 0)
    rows = WIN_G * WA_TQ
    r = lax.broadcasted_iota(jnp.int32, (rows, WA_TQ), 0) % WA_TQ
    c = lax.broadcasted_iota(jnp.int32, (rows, WA_TQ), 1)
    m_prev = c >= r + off_prev
    m_cur = c >= off_cur
    m_next = c + off_next <= r
    grp = lax.broadcasted_iota(jnp.int32, (rows, 1), 0) // WA_TQ

    for h in range(WIN_HKV):
        q = jnp.concatenate([q_ref[:, (h * WIN_G + g) * HD:(h * WIN_G + g + 1) * HD] for g in range(WIN_G)], axis=0)
        ks = slice(h * HD, (h + 1) * HD)
        vs = slice(W_WIN_KV + h * HD, W_WIN_KV + (h + 1) * HD)
        s_p = jnp.where(m_prev, _dot_nt(q, kp_ref[:, ks]) * ATT_SCALE, NEG)
        s_c = jnp.where(m_cur, _dot_nt(q, kc_ref[:, ks]) * ATT_SCALE, NEG)
        s_n = jnp.where(m_next, _dot_nt(q, kn_ref[:, ks]) * ATT_SCALE, NEG)
        s_x = _dot_nt(q, kx_ref[:, ks]) * ATT_SCALE
        snk = jnp.zeros((rows, 1), F32)
        for g in range(WIN_G):
            snk = jnp.where(grp == g, sink_ref[h * WIN_G + g], snk)
        m = jnp.maximum(jnp.maximum(jnp.max(s_p, axis=-1, keepdims=True), jnp.max(s_c, axis=-1, keepdims=True)),
                        jnp.maximum(jnp.max(s_n, axis=-1, keepdims=True), jnp.max(s_x, axis=-1, keepdims=True)))
        m = jnp.maximum(m, snk)
        p_p = jnp.exp(s_p - m)
        p_c = jnp.exp(s_c - m)
        p_n = jnp.exp(s_n - m)
        p_x = jnp.exp(s_x - m)
        den = (jnp.sum(p_p, axis=-1, keepdims=True) + jnp.sum(p_c, axis=-1, keepdims=True)
               + jnp.sum(p_n, axis=-1, keepdims=True) + jnp.sum(p_x, axis=-1, keepdims=True) + jnp.exp(snk - m))
        o = (jnp.dot(p_p.astype(BF16), kp_ref[:, vs], preferred_element_type=F32)
             + jnp.dot(p_c.astype(BF16), kc_ref[:, vs], preferred_element_type=F32)
             + jnp.dot(p_n.astype(BF16), kn_ref[:, vs], preferred_element_type=F32)
             + jnp.dot(p_x.astype(BF16), kx_ref[:, vs], preferred_element_type=F32))
        o = o / den
        for g in range(WIN_G):
            o_ref[:, (h * WIN_G + g) * HD:(h * WIN_G + g + 1) * HD] = o[g * WA_TQ:(g + 1) * WA_TQ].astype(BF16)


def _win_attention(qw, kvw, sink, latent_only=False):
    nt = T // WA_TQ
    t0 = _first_tile(WA_TQ, latent_only)

    def bounds(i):
        is_ctx = i < WA_CTX_TILES
        b = (i - WA_CTX_TILES) // WA_LAT_TILES
        lo = jnp.where(is_ctx, 0, WA_CTX_TILES + b * WA_LAT_TILES)
        hi = jnp.where(is_ctx, nt - 1, WA_CTX_TILES + (b + 1) * WA_LAT_TILES - 1)
        return lo, hi

    def prev_map(g):
        i = g + t0
        lo, _ = bounds(i)
        return (jnp.maximum(i - 1, lo), 0)

    def next_map(g):
        i = g + t0
        _, hi = bounds(i)
        return (jnp.minimum(i + 1, hi), 0)

    def ctx_map(g):
        i = g + t0
        b = jnp.where(i < WA_CTX_TILES, i // (L // WA_TQ), (i - WA_CTX_TILES) // WA_LAT_TILES)
        return (b, 0)

    kvw_w = 2 * W_WIN_KV
    return pl.pallas_call(
        functools.partial(_win_attn_kernel, t0),
        grid=(nt - t0,),
        in_specs=[
            pl.BlockSpec(memory_space=pltpu.SMEM),
            pl.BlockSpec((WA_TQ, W_WIN_Q), lambda g: (g + t0, 0)),
            pl.BlockSpec((WA_TQ, kvw_w), prev_map),
            pl.BlockSpec((WA_TQ, kvw_w), lambda g: (g + t0, 0)),
            pl.BlockSpec((WA_TQ, kvw_w), next_map),
            pl.BlockSpec((L, kvw_w), ctx_map),
        ],
        out_specs=pl.BlockSpec((WA_TQ, W_WIN_Q), lambda g: (g + t0, 0)),
        out_shape=jax.ShapeDtypeStruct((T, W_WIN_Q), BF16),
        compiler_params=_cparams("arbitrary"),
        name="window_attention",
    )(sink, qw, kvw, kvw, kvw, kvw)


NA_TQ = 256
NA_QROWS = NA_TQ // GRID_W
NA_KROWS = 3 * NA_QROWS
NA_NKEY = NA_KROWS * GRID_W
NA_TYPES = 4
RPB_R = 2 * NA_KH - 1
RPB_C = 2 * NA_KW - 1


def _na_row_valid(ty, a, j):
    if ty == 0:
        return NA_QROWS <= j < NA_QROWS + NA_KH
    if ty == 1:
        return a <= j < a + NA_KH
    if ty == 2:
        return j < NA_KH
    return False


def _rpb_kernel(rpb_ref, o_ref):
    h = pl.program_id(0)
    shp = (GRID_W, 2 * GRID_W)
    qc = lax.broadcasted_iota(jnp.int32, shp, 0)
    lane = lax.broadcasted_iota(jnp.int32, shp, 1)
    kc = lane % GRID_W
    second = lane >= GRID_W
    dcol = kc - qc + (NA_KW - 1)
    cs = jnp.clip(qc - NA_KW // 2, 0, GRID_W - NA_KW)
    colmask = jnp.logical_and(kc >= cs, kc < cs + NA_KW)
    neg = jnp.full(shp, NEG, F32)
    base = h * (RPB_R * RPB_C)
    pair = []
    for dr in range(RPB_R - 1):
        acc = jnp.zeros(shp, F32)
        for dd in range(RPB_C):
            v0 = rpb_ref[base + dr * RPB_C + dd]
            v1 = rpb_ref[base + (dr + 1) * RPB_C + dd]
            acc = jnp.where(dcol == dd, jnp.where(second, v1, v0), acc)
        pair.append(jnp.where(colmask, acc, neg))
    for ty in range(NA_TYPES):
        for a in range(NA_QROWS):
            for jp in range(NA_KROWS // 2):
                j = 2 * jp
                ok0 = _na_row_valid(ty, a, j)
                ok1 = _na_row_valid(ty, a, j + 1)
                dr = j - a + NA_QROWS - 1
                if ok0 and ok1:
                    tile = pair[dr]
                elif ok0:
                    tile = jnp.where(second, neg, pair[dr])
                elif ok1:
                    tile = jnp.where(second, pair[dr], neg)
                else:
                    tile = neg
                o_ref[ty, 0, a * GRID_W:(a + 1) * GRID_W, jp * 2 * GRID_W:(jp + 1) * 2 * GRID_W] = tile


def _rpb_tiles(rpb):
    return pl.pallas_call(
        _rpb_kernel,
        grid=(NA_H,),
        in_specs=[pl.BlockSpec(memory_space=pltpu.SMEM)],
        out_specs=pl.BlockSpec((NA_TYPES, 1, NA_TQ, NA_NKEY), lambda h: (0, h, 0, 0)),
        out_shape=jax.ShapeDtypeStruct((NA_TYPES, NA_H, NA_TQ, NA_NKEY), F32),
        compiler_params=_cparams("arbitrary"),
        name="rpb_tiles",
    )(rpb.reshape(-1))


NA_CTX_TILES = CTX_ROWS // NA_TQ
NA_LAT_TILES = S // NA_TQ


def _na_attn_kernel(q_ref, kp_ref, kc_ref, kn_ref, vp_ref, vc_ref, vn_ref, kx_ref, vx_ref, bias_ref, o_ref):
    for h in range(NA_H):
        hs = slice(h * HD, (h + 1) * HD)
        q = q_ref[:, hs]
        s_p = _dot_nt(q, kp_ref[:, hs]) * ATT_SCALE + bias_ref[0, h, :, 0:NA_TQ]
        s_c = _dot_nt(q, kc_ref[:, hs]) * ATT_SCALE + bias_ref[0, h, :, NA_TQ:2 * NA_TQ]
        s_n = _dot_nt(q, kn_ref[:, hs]) * ATT_SCALE + bias_ref[0, h, :, 2 * NA_TQ:3 * NA_TQ]
        s_x = _dot_nt(q, kx_ref[:, hs]) * ATT_SCALE
        m = jnp.maximum(jnp.maximum(jnp.max(s_p, axis=-1, keepdims=True), jnp.max(s_c, axis=-1, keepdims=True)),
                        jnp.maximum(jnp.max(s_n, axis=-1, keepdims=True), jnp.max(s_x, axis=-1, keepdims=True)))
        p_p = jnp.exp(s_p - m)
        p_c = jnp.exp(s_c - m)
        p_n = jnp.exp(s_n - m)
        p_x = jnp.exp(s_x - m)
        den = (jnp.sum(p_p, axis=-1, keepdims=True) + jnp.sum(p_c, axis=-1, keepdims=True)
               + jnp.sum(p_n, axis=-1, keepdims=True) + jnp.sum(p_x, axis=-1, keepdims=True))
        o = (jnp.dot(p_p.astype(BF16), vp_ref[:, hs], preferred_element_type=F32)
             + jnp.dot(p_c.astype(BF16), vc_ref[:, hs], preferred_element_type=F32)
             + jnp.dot(p_n.astype(BF16), vn_ref[:, hs], preferred_element_type=F32)
             + jnp.dot(p_x.astype(BF16), vx_ref[:, hs], preferred_element_type=F32))
        o_ref[:, hs] = (o / den).astype(BF16)


def _na_attention(qn, kn, vn, bias, latent_only=False):
    nt = T // NA_TQ
    t0 = _first_tile(NA_TQ, latent_only)

    def bounds(i):
        is_ctx = i < NA_CTX_TILES
        b = (i - NA_CTX_TILES) // NA_LAT_TILES
        lo = jnp.where(is_ctx, 0, NA_CTX_TILES + b * NA_LAT_TILES)
        hi = jnp.where(is_ctx, nt - 1, NA_CTX_TILES + (b + 1) * NA_LAT_TILES - 1)
        return lo, hi

    def prev_map(g):
        i = g + t0
        lo, _ = bounds(i)
        return (jnp.maximum(i - 1, lo), 0)

    def next_map(g):
        i = g + t0
        _, hi = bounds(i)
        return (jnp.minimum(i + 1, hi), 0)

    def ctx_map(g):
        i = g + t0
        return (jnp.where(i < NA_CTX_TILES, i, (i - NA_CTX_TILES) // NA_LAT_TILES), 0)

    def bias_map(g):
        i = g + t0
        n = (i - NA_CTX_TILES) % NA_LAT_TILES
        ty = jnp.where(i < NA_CTX_TILES, 3, jnp.where(n == 0, 0, jnp.where(n == NA_LAT_TILES - 1, 2, 1)))
        return (ty, 0, 0, 0)

    w = W_NA_KV
    cur = pl.BlockSpec((NA_TQ, w), lambda g: (g + t0, 0))
    prv = pl.BlockSpec((NA_TQ, w), prev_map)
    nxt = pl.BlockSpec((NA_TQ, w), next_map)
    ctx = pl.BlockSpec((L, w), ctx_map)
    return pl.pallas_call(
        _na_attn_kernel,
        grid=(nt - t0,),
        in_specs=[cur, prv, cur, nxt, prv, cur, nxt, ctx, ctx,
                  pl.BlockSpec((1, NA_H, NA_TQ, NA_NKEY), bias_map)],
        out_specs=cur,
        out_shape=jax.ShapeDtypeStruct((T, W_NA_Q), BF16),
        compiler_params=_cparams("arbitrary"),
        name="neighborhood_attention",
    )(qn, kn, kn, kn, vn, vn, vn, kn, vn, bias)


MG_TM = 512
MG_TN = 1024


def _merge_kernel(ca_ref, aw_ref, an_ref, wc_ref, ww_ref, wn_ref, ga_ref, gb_ref, gc_ref, o_ref):
    ya = jnp.dot(ca_ref[...], wc_ref[...], preferred_element_type=F32)
    yb = jnp.dot(aw_ref[...], ww_ref[...], preferred_element_type=F32)
    yc = jnp.dot(an_ref[...], wn_ref[...], preferred_element_type=F32)
    o = _sigmoid(ga_ref[...]) * ya + _sigmoid(gb_ref[...]) * yb + _sigmoid(gc_ref[...]) * yc
    o_ref[...] = o.astype(BF16)


def _merge(hconv, aw, an, w_conv_out, w_win_out, w_na_out, u, latent_only=False):
    gb0 = GATE_OFF // MG_TN
    gstep = D // MG_TN
    t0 = _first_tile(MG_TM, latent_only)

    def a_spec(k):
        return pl.BlockSpec((MG_TM, k), lambda i, j: (i + t0, 0))

    def w_spec(k):
        return pl.BlockSpec((k, MG_TN), lambda i, j: (0, j))

    def g_spec(which):
        return pl.BlockSpec((MG_TM, MG_TN), lambda i, j: (i + t0, gb0 + which * gstep + j))

    return pl.pallas_call(
        _merge_kernel,
        grid=(T // MG_TM - t0, D // MG_TN),
        in_specs=[a_spec(CONV_CH), a_spec(W_WIN_Q), a_spec(W_NA_Q),
                  w_spec(CONV_CH), w_spec(W_WIN_Q), w_spec(W_NA_Q),
                  g_spec(0), g_spec(1), g_spec(2)],
        out_specs=pl.BlockSpec((MG_TM, MG_TN), lambda i, j: (i + t0, j)),
        out_shape=jax.ShapeDtypeStruct((T, D), BF16),
        compiler_params=_cparams("arbitrary", "arbitrary"),
        name="gated_merge",
    )(hconv, aw, an, w_conv_out, w_win_out, w_na_out, u, u, u)


TK_TT = 256


def _topk_rounds(scores, n_rounds):
    nrow = scores.shape[0]
    idx = lax.broadcasted_iota(jnp.int32, scores.shape, 0).astype(F32)
    work = scores
    rank = jnp.full(scores.shape, float(n_rounds), F32)
    vals = []
    for a in range(n_rounds):
        m = jnp.max(work, axis=0, keepdims=True)
        first = jnp.min(jnp.where(work == m, idx, float(nrow)), axis=0, keepdims=True)
        sel = idx == first
        rank = jnp.where(sel, float(a), rank)
        work = jnp.where(sel, -jnp.inf, work)
        vals.append(m)
    return jnp.concatenate(vals, axis=0), rank


def _topk_rounds_no_ties(scores, n_rounds):
    work = scores
    rank = jnp.full(scores.shape, float(n_rounds), F32)
    vals = []
    for a in range(n_rounds):
        m = jnp.max(work, axis=0, keepdims=True)
        sel = work == m
        rank = jnp.where(sel, float(a), rank)
        work = jnp.where(sel, -jnp.inf, work)
        vals.append(m)
    taken = jnp.sum(jnp.where(rank < float(n_rounds), 1.0, 0.0), axis=0, keepdims=True)
    return jnp.concatenate(vals, axis=0), rank, taken


def _peer_topk_kernel(q_ref, k1_ref, k2_ref, n_ref, e1_ref, r2_ref, e2_ref):
    kk = PEER_TOPK

    def head(h, carry):
        c1 = pl.multiple_of(h * 2 * HD, 2 * HD)
        q1 = q_ref[:, pl.ds(c1, HD)]
        q2 = q_ref[:, pl.ds(c1 + HD, HD)]
        s1 = _dot_nt(k1_ref[h].astype(BF16), q1)
        s2 = _dot_nt(k2_ref[h].astype(BF16), q2)
        f1, fr1, t1 = _topk_rounds_no_ties(s1, kk)
        f2, fr2, t2 = _topk_rounds_no_ties(s2, kk)
        tied = jnp.max(jnp.abs(t1 - float(kk)) + jnp.abs(t2 - float(kk))) > 0.0

        def exact(_):
            return _topk_rounds(s1, kk) + _topk_rounds(s2, kk)

        def keep(_):
            return f1, fr1, f2, fr2

        v1all, rank1, v2all, rank2 = lax.cond(tied, exact, keep, None)
        v1 = [v1all[0:1, :]]
        v2 = [v2all[0:1, :]]
        arow = lax.broadcasted_iota(jnp.int32, v2all.shape, 0).astype(F32)
        cnt = jnp.zeros(v2all.shape, F32)
        front = v1all + v2[0]
        top = v1[0] + v2[0]
        z = jnp.zeros_like(top)
        for _ in range(kk):
            m = jnp.max(front, axis=0, keepdims=True)
            first = jnp.min(jnp.where(front == m, arow, float(kk)), axis=0, keepdims=True)
            sel = arow == first
            cnt = cnt + jnp.where(sel, 1.0, 0.0)
            z = z + jnp.exp(m - top)
            taken = jnp.max(jnp.where(sel, cnt, -1.0), axis=0, keepdims=True)
            nxt = jnp.max(jnp.where(arow == taken, v2all, -jnp.inf), axis=0, keepdims=True)
            front = jnp.where(sel, v1all + nxt, front)
        nfull = jnp.zeros(s1.shape, F32)
        for a in range(kk):
            nfull = jnp.where(rank1 == float(a), cnt[a:a + 1, :], nfull)
        n_ref[h] = nfull
        e1_ref[h] = jnp.exp(s1 - v1[0]) / z
        r2_ref[h] = rank2
        e2_ref[h] = jnp.exp(s2 - v2[0])
        return carry

    lax.fori_loop(0, PEER_HEADS, head, 0)


def _peer_topk(q, k1, k2, latent_only=False):
    tt = TK_TT
    t0 = _first_tile(tt, latent_only)
    kspec = pl.BlockSpec((PEER_HEADS, N_KEYS, HD), lambda i: (0, 0, 0))
    ospec = pl.BlockSpec((PEER_HEADS, N_KEYS, tt), lambda i: (0, 0, i + t0))
    oshape = jax.ShapeDtypeStruct((PEER_HEADS, N_KEYS, T), F32)
    return pl.pallas_call(
        _peer_topk_kernel,
        grid=(T // tt - t0,),
        in_specs=[pl.BlockSpec((tt, 2 * HD * PEER_HEADS), lambda i: (i + t0, 0)), kspec, kspec],
        out_specs=[ospec] * 4,
        out_shape=[oshape] * 4,
        compiler_params=_cparams("arbitrary"),
        name="peer_topk",
    )(q, k1, k2)


EX_TM = 512
EX_TN = 512
EX_SUB = 256
EX_NC = 1024
EX_LC = 128
EX_RC = 32
EX_I1 = EX_TN // N_KEYS
EX_STEPS_PER_GROUP = 8 // EX_I1
assert EX_I1 * EX_STEPS_PER_GROUP == 8 and EX_STEPS_PER_GROUP == 2


def _gelu_tanh(x):
    cdf = 0.5 * (1.0 + jnp.tanh(math.sqrt(2.0 / math.pi) * (x + 0.044715 * (x * x * x))))
    return x * cdf


def _experts_kernel(h_ref, u_ref, v_ref, n_ref, e1_ref, r2_ref, e2_ref, o_ref):
    j = pl.program_id(1)

    @pl.when(j == 0)
    def _():
        o_ref[...] = jnp.zeros_like(o_ref)

    grp = pl.multiple_of((j // EX_STEPS_PER_GROUP) * 8, 8)
    upper = (j % EX_STEPS_PER_GROUP) == 1
    key_rows = {}
    for hh in range(PEER_HEADS):
        for c in range(EX_TM // EX_LC):
            cs = slice(c * EX_LC, (c + 1) * EX_LC)
            n8 = n_ref[hh, pl.ds(grp, 8), cs]
            e8 = e1_ref[hh, pl.ds(grp, 8), cs]
            for k in range(EX_I1):
                key_rows[(k, hh, c)] = (jnp.where(upper, n8[EX_I1 + k:EX_I1 + k + 1], n8[k:k + 1]),
                                        jnp.where(upper, e8[EX_I1 + k:EX_I1 + k + 1], e8[k:k + 1]))

    def gate_tile(k, rows, c):
        cs = slice(c * EX_LC, (c + 1) * EX_LC)
        gate = jnp.zeros((EX_RC, EX_LC), F32)
        for hh in range(PEER_HEADS):
            nrow, e1row = key_rows[(k, hh, c)]
            gate = gate + jnp.where(r2_ref[hh, rows, cs] < nrow, e2_ref[hh, rows, cs], 0.0) * e1row
        return gate

    w_parts = []
    for s in range(EX_TN // EX_SUB):
        es = slice(s * EX_SUB, (s + 1) * EX_SUB)
        act = _gelu_tanh(jnp.dot(u_ref[es, :], h_ref[...], preferred_element_type=F32))
        for ii in range(EX_SUB // N_KEYS):
            k = s * (EX_SUB // N_KEYS) + ii
            for rc in range(N_KEYS // EX_RC):
                rows = slice(rc * EX_RC, (rc + 1) * EX_RC)
                arows = slice(ii * N_KEYS + rc * EX_RC, ii * N_KEYS + (rc + 1) * EX_RC)
                row = []
                for c in range(EX_TM // EX_LC):
                    cs = slice(c * EX_LC, (c + 1) * EX_LC)
                    row.append((gate_tile(k, rows, c) * act[arows, cs]).astype(BF16))
                w_parts.append(jnp.concatenate(row, axis=1))
    w_t = jnp.concatenate(w_parts, axis=0)
    for nc in range(D // EX_NC):
        ns = slice(nc * EX_NC, (nc + 1) * EX_NC)
        o_ref[:, ns] += lax.dot_general(w_t, v_ref[:, ns], (((0,), (0,)), ((), ())),
                                        preferred_element_type=F32)


def _experts(h_t, u, v, tables, latent_only=False):
    n, e1, r2, e2 = tables
    t0 = _first_tile(EX_TM, latent_only)
    tspec = pl.BlockSpec((PEER_HEADS, N_KEYS, EX_TM), lambda i, j: (0, 0, i + t0))
    return pl.pallas_call(
        _experts_kernel,
        grid=(T // EX_TM - t0, N_EXPERTS // EX_TN),
        in_specs=[pl.BlockSpec((D, EX_TM), lambda i, j: (0, i + t0)),
                  pl.BlockSpec((EX_TN, D), lambda i, j: (j, 0)),
                  pl.BlockSpec((EX_TN, D), lambda i, j: (j, 0)),
                  tspec, tspec, tspec, tspec],
        out_specs=pl.BlockSpec((EX_TM, D), lambda i, j: (i + t0, 0), pipeline_mode=pl.Buffered(1)),
        out_shape=jax.ShapeDtypeStruct((T, D), F32),
        compiler_params=_cparams("arbitrary", "arbitrary"),
        name="peer_experts",
    )(h_t, u, v, n, e1, r2, e2)


FR_TM = 256


def _final_kernel(x_ref, f_ref, g_ref, o_ref):
    o_ref[...] = x_ref[...] + g_ref[0, 0] * f_ref[...]


def _final_residual(x, f, modt, layer, g_k):
    off = CTX_ROWS // FR_TM
    lat = pl.BlockSpec((FR_TM, D), lambda i: (i + off, 0))
    return pl.pallas_call(
        _final_kernel,
        grid=(B * S // FR_TM,),
        in_specs=[lat, lat,
                  pl.BlockSpec((1, 1, 1, D), lambda i: (layer, _group_of_row((i + off) * FR_TM), 0, g_k))],
        out_specs=pl.BlockSpec((FR_TM, D), lambda i: (i, 0)),
        out_shape=jax.ShapeDtypeStruct((B * S, D), F32),
        compiler_params=_cparams("arbitrary"),
        name="final_residual",
    )(x, f, modt)


SH1, SC1, G1, SH2, SC2, G2 = range(6)


def kernel(x, c, ctx, c_ctx, w_ada, b_ada, norm1_g, norm2_g, w_in, conv_w, conv_b, conv_ln_g, conv_ln_b,
           w_conv_out, win_qn_g, win_kn_g, win_sink, w_win_out, na_qn_g, na_kn_g, na_rpb, w_na_out, w_out,
           peer_wq, peer_k1, peer_k2, peer_u, peer_v):
    xs = jnp.concatenate([ctx.reshape(CTX_ROWS, D), x.reshape(B * S, D)], axis=0)
    cvec = jnp.concatenate([c_ctx[None], c, jnp.zeros((N_GROUPS - 1 - B, D), F32)], axis=0)
    modt = _mods(cvec, w_ada, b_ada).reshape(DEPTH, N_GROUPS, 1, 6 * D)
    cosf, sinf = _rope_tables()

    f = None
    for l in range(DEPTH):
        if l == 0:
            h1 = _normmod(xs, modt, l, norm1_g[l], SC1, SH1)
        else:
            xs, h1 = _normmod(xs, modt, l, norm1_g[l], SC1, SH1, f=f, g_k=G2)
        lat = l == DEPTH - 1
        u = _matmul_wstat(h1, w_in, l, F32, 1024)
        hconv = _conv_module(u, conv_w[l], conv_b[l], conv_ln_g[l], conv_ln_b[l], latent_only=lat)
        gains = jnp.concatenate([win_qn_g[l][None], win_kn_g[l][None], na_qn_g[l][None], na_kn_g[l][None],
                                 jnp.zeros((4, HD), F32)], axis=0)
        qw, qn, kvw, kn, vn = _headnorm(u, cosf, sinf, gains)
        aw = _win_attention(qw, kvw, win_sink[l], latent_only=lat)
        an = _na_attention(qn, kn, vn, _rpb_tiles(na_rpb[l]), latent_only=lat)
        merged = _merge(hconv, aw, an, _cast_bf16(w_conv_out, l), _cast_bf16(w_win_out, l),
                        _cast_bf16(w_na_out, l), u, latent_only=lat)
        xs = _matmul_resid(merged, _cast_bf16(w_out, l), xs, modt, l, G1, 1024, latent_only=lat)
        h2, h2_t = _normmod(xs, modt, l, norm2_g[l], SC2, SH2, want_t=True, latent_only=lat)
        q = _matmul(h2, _cast_bf16(peer_wq, l), BF16, 1024, latent_only=lat)
        tables = _peer_topk(q, peer_k1[l], peer_k2[l], latent_only=lat)
        f = _experts(h2_t, _cast_bf16(peer_u, l), _cast_bf16(peer_v, l), tables, latent_only=lat)
    out = _final_residual(xs, f, modt, DEPTH - 1, G2)
    return out.reshape(B, S, D)
```

```python
import functools
import math

import jax
import jax.numpy as jnp
import numpy as np
from jax import lax
from jax.experimental import pallas as pl
from jax.experimental.pallas import tpu as pltpu

F32 = jnp.float32
BF16 = jnp.bfloat16

D = 4096
B = 2
S = 4096
L = 256
DEPTH = 2
GRID_W = 64
HD = 128
EPS = 1e-6
NEG = -1e30
ROPE_BASE = 10000.0

CONV_CH = D // 4
CONV_K = 31
WIN_HQ = 16
WIN_HKV = 4
WIN_G = WIN_HQ // WIN_HKV
NA_H = 8
NA_KH = 8
NA_KW = 16
PEER_HEADS = 8
N_KEYS = 128
N_EXPERTS = N_KEYS * N_KEYS
PEER_TOPK = 16

W_CONV_IN = 2 * CONV_CH
W_WIN_Q = WIN_HQ * HD
W_NA_Q = NA_H * HD
W_GATE = 3 * D
W_WIN_KV = WIN_HKV * HD
W_NA_KV = NA_H * HD
Q_WIN_OFF = W_CONV_IN
Q_NA_OFF = Q_WIN_OFF + W_WIN_Q
GATE_OFF = Q_NA_OFF + W_NA_Q
KV_OFF = GATE_OFF + W_GATE
IN_COLS = KV_OFF + 2 * W_WIN_KV + 2 * W_NA_KV

CTX_ROWS = B * L
T = CTX_ROWS + B * S
N_GROUPS = 8
ATT_SCALE = HD ** -0.5

VMEM_LIMIT = 56 * 1024 * 1024


def _cparams(*sem):
    return pltpu.CompilerParams(dimension_semantics=sem, vmem_limit_bytes=VMEM_LIMIT)


def _group_of_row(row0):
    return jnp.where(row0 < CTX_ROWS, 0, 1 + (row0 - CTX_ROWS) // S)


def _sigmoid(z):
    return 0.5 * jnp.tanh(0.5 * z) + 0.5


def _dot_nt(a, b):
    return lax.dot_general(a, b, (((1,), (1,)), ((), ())), preferred_element_type=F32)


def _first_tile(tile, latent_only):
    return CTX_ROWS // tile if latent_only else 0


CAST_BLOCK_BYTES = 8 * 1024 * 1024


def _cast_kernel(w_ref, o_ref):
    o_ref[...] = w_ref[0].astype(BF16)


def _cast_bf16(w, layer):
    _, m, n = w.shape
    rb = m
    while rb * n * 4 > CAST_BLOCK_BYTES:
        rb //= 2
    return pl.pallas_call(
        _cast_kernel,
        grid=(m // rb,),
        in_specs=[pl.BlockSpec((1, rb, n), lambda i: (layer, i, 0))],
        out_specs=pl.BlockSpec((rb, n), lambda i: (i, 0)),
        out_shape=jax.ShapeDtypeStruct((m, n), BF16),
        compiler_params=_cparams("arbitrary"),
        name="cast_bf16",
    )(w)


MOD_TN = 512


def _mods_kernel(c_ref, w_ref, b_ref, o_ref):
    cv = c_ref[...]
    a = (cv * _sigmoid(cv)).astype(BF16)
    w = w_ref[0].astype(BF16)
    o_ref[0] = jnp.dot(a, w, preferred_element_type=F32) + b_ref[0]


def _mods(cvec, w_ada, b_ada):
    n = 6 * D
    return pl.pallas_call(
        _mods_kernel,
        grid=(DEPTH, n // MOD_TN),
        in_specs=[
            pl.BlockSpec((N_GROUPS, D), lambda l, j: (0, 0)),
            pl.BlockSpec((1, D, MOD_TN), lambda l, j: (l, 0, j)),
            pl.BlockSpec((1, 1, MOD_TN), lambda l, j: (l, 0, j)),
        ],
        out_specs=pl.BlockSpec((1, N_GROUPS, MOD_TN), lambda l, j: (l, 0, j)),
        out_shape=jax.ShapeDtypeStruct((DEPTH, N_GROUPS, n), F32),
        compiler_params=_cparams("arbitrary", "arbitrary"),
        name="adaln_mods",
    )(cvec, w_ada, b_ada.reshape(DEPTH, 1, n))


NM_TM = 256


def _normmod_kernel(has_f, want_t, *refs):
    refs = list(refs)
    ht_ref = refs.pop() if want_t else None
    if has_f:
        x_ref, f_ref, g_ref, ng_ref, sc_ref, sh_ref, xo_ref, h_ref = refs
        x = x_ref[...] + g_ref[0, 0] * f_ref[...]
        xo_ref[...] = x
    else:
        x_ref, ng_ref, sc_ref, sh_ref, h_ref = refs
        x = x_ref[...]
    ms = jnp.mean(x * x, axis=-1, keepdims=True)
    y = x * lax.rsqrt(ms + EPS) * ng_ref[...]
    h = y * (1.0 + sc_ref[0, 0]) + sh_ref[0, 0]
    h_ref[...] = h.astype(BF16)
    if want_t:
        ht_ref[...] = h.T.astype(BF16)


def _mod_spec(layer, k, tm, t0=0):
    return pl.BlockSpec((1, 1, 1, D), lambda i: (layer, _group_of_row((i + t0) * tm), 0, k))


def _normmod(x, modt, layer, norm_g, sc_k, sh_k, f=None, g_k=None, want_t=False, latent_only=False):
    tm = NM_TM
    t0 = _first_tile(tm, latent_only)
    row = pl.BlockSpec((tm, D), lambda i: (i + t0, 0))
    has_f = f is not None
    in_specs = [row]
    args = [x]
    if has_f:
        in_specs += [row, _mod_spec(layer - 1, g_k, tm, t0)]
        args += [f, modt]
    in_specs += [pl.BlockSpec((1, D), lambda i: (0, 0)), _mod_spec(layer, sc_k, tm, t0),
                 _mod_spec(layer, sh_k, tm, t0)]
    args += [norm_g.reshape(1, D), modt, modt]
    h_shape = jax.ShapeDtypeStruct((T, D), BF16)
    if has_f:
        out_specs = [row, row]
        out_shape = [jax.ShapeDtypeStruct((T, D), F32), h_shape]
    else:
        out_specs = [row]
        out_shape = [h_shape]
    if want_t:
        out_specs = out_specs + [pl.BlockSpec((D, tm), lambda i: (0, i + t0))]
        out_shape = out_shape + [jax.ShapeDtypeStruct((D, T), BF16)]
    outs = pl.pallas_call(
        functools.partial(_normmod_kernel, has_f, want_t),
        grid=(T // tm - t0,),
        in_specs=in_specs,
        out_specs=out_specs,
        out_shape=out_shape,
        compiler_params=_cparams("arbitrary"),
        name="normmod_resid" if has_f else "normmod",
    )(*args)
    return outs[0] if len(outs) == 1 else tuple(outs)


MM_TM = 512


def _mm_kernel(a_ref, w_ref, o_ref):
    o_ref[...] = jnp.dot(a_ref[...], w_ref[...], preferred_element_type=F32).astype(o_ref.dtype)


def _mm_resid_kernel(a_ref, w_ref, r_ref, g_ref, o_ref):
    acc = jnp.dot(a_ref[...], w_ref[...], preferred_element_type=F32)
    o_ref[...] = r_ref[...] + g_ref[0, 0] * acc


def _mm_wstat_kernel(a_ref, w_ref, o_ref, wb_ref):
    @pl.when(pl.program_id(1) == 0)
    def _():
        wb_ref[...] = w_ref[0].astype(BF16)

    o_ref[...] = jnp.dot(a_ref[...], wb_ref[...], preferred_element_type=F32).astype(o_ref.dtype)


def _matmul_wstat(a, w, layer, out_dtype, tn):
    m, k = a.shape
    n = w.shape[2]
    return pl.pallas_call(
        _mm_wstat_kernel,
        grid=(n // tn, m // MM_TM),
        in_specs=[pl.BlockSpec((MM_TM, k), lambda j, i: (i, 0)),
                  pl.BlockSpec((1, k, tn), lambda j, i: (layer, 0, j))],
        out_specs=pl.BlockSpec((MM_TM, tn), lambda j, i: (i, j)),
        out_shape=jax.ShapeDtypeStruct((m, n), out_dtype),
        scratch_shapes=[pltpu.VMEM((k, tn), BF16)],
        compiler_params=_cparams("arbitrary", "arbitrary"),
        name="matmul_wstat",
    )(a, w)


def _matmul(a, w, out_dtype, tn, latent_only=False):
    m, k = a.shape
    n = w.shape[1]
    t0 = _first_tile(MM_TM, latent_only)
    return pl.pallas_call(
        _mm_kernel,
        grid=(m // MM_TM - t0, n // tn),
        in_specs=[pl.BlockSpec((MM_TM, k), lambda i, j: (i + t0, 0)),
                  pl.BlockSpec((k, tn), lambda i, j: (0, j))],
        out_specs=pl.BlockSpec((MM_TM, tn), lambda i, j: (i + t0, j)),
        out_shape=jax.ShapeDtypeStruct((m, n), out_dtype),
        compiler_params=_cparams("arbitrary", "arbitrary"),
        name="matmul",
    )(a, w)


def _matmul_resid(a, w, resid, modt, layer, g_k, tn, latent_only=False):
    m, k = a.shape
    n = w.shape[1]
    nj = n // tn
    t0 = _first_tile(MM_TM, latent_only)
    return pl.pallas_call(
        _mm_resid_kernel,
        grid=(m // MM_TM - t0, nj),
        in_specs=[pl.BlockSpec((MM_TM, k), lambda i, j: (i + t0, 0)),
                  pl.BlockSpec((k, tn), lambda i, j: (0, j)),
                  pl.BlockSpec((MM_TM, tn), lambda i, j: (i + t0, j)),
                  pl.BlockSpec((1, 1, 1, tn),
                               lambda i, j: (layer, _group_of_row((i + t0) * MM_TM), 0, g_k * nj + j))],
        out_specs=pl.BlockSpec((MM_TM, tn), lambda i, j: (i + t0, j)),
        out_shape=jax.ShapeDtypeStruct((m, n), F32),
        compiler_params=_cparams("arbitrary", "arbitrary"),
        name="matmul_resid",
    )(a, w, resid, modt)


CV_TM = 256
CV_HALO = 16
CV_RC = 64
CV_CC = 128


def _conv_kernel(t0, prev_ref, cur_ref, next_ref, w_ref, cb_ref, lg_ref, lb_ref, o_ref, buf_ref, acc_ref):
    i = pl.program_id(0) + t0
    row0 = i * CV_TM
    lat = row0 - CTX_ROWS
    seq_start = jnp.where(row0 < CTX_ROWS, row0 % L == 0, lat % S == 0)
    seq_end = jnp.where(row0 < CTX_ROWS, (row0 + CV_TM) % L == 0, (lat + CV_TM) % S == 0)

    def glu(u):
        return u[:, :CONV_CH] * _sigmoid(u[:, CONV_CH:])

    buf_ref[0:CV_HALO, :] = glu(prev_ref[...]) * jnp.where(seq_start, 0.0, 1.0)
    buf_ref[CV_HALO:CV_HALO + CV_TM, :] = glu(cur_ref[...])
    buf_ref[CV_HALO + CV_TM:, :] = glu(next_ref[...]) * jnp.where(seq_end, 0.0, 1.0)

    tap0 = CV_HALO - CONV_K // 2

    def col_chunk(c, carry):
        c0 = pl.multiple_of(c * CV_CC, CV_CC)
        for r in range(CV_TM // CV_RC):
            acc = jnp.zeros((CV_RC, CV_CC), F32)
            for k in range(CONV_K):
                r0 = r * CV_RC + k + tap0
                acc = acc + w_ref[k:k + 1, pl.ds(c0, CV_CC)] * buf_ref[r0:r0 + CV_RC, pl.ds(c0, CV_CC)]
            acc_ref[r * CV_RC:(r + 1) * CV_RC, pl.ds(c0, CV_CC)] = acc
        return carry

    lax.fori_loop(0, CONV_CH // CV_CC, col_chunk, 0)

    h = acc_ref[...] + cb_ref[...]
    mu = jnp.mean(h, axis=-1, keepdims=True)
    xc = h - mu
    var = jnp.mean(xc * xc, axis=-1, keepdims=True)
    y = xc * lax.rsqrt(var + EPS) * lg_ref[...] + lb_ref[...]
    o_ref[...] = (y * _sigmoid(y)).astype(BF16)


def _conv_module(u, conv_w, conv_b, ln_g, ln_b, latent_only=False):
    hb = CV_TM // CV_HALO
    last = T // CV_HALO - 1
    t0 = _first_tile(CV_TM, latent_only)
    vec = pl.BlockSpec((1, CONV_CH), lambda i: (0, 0))
    return pl.pallas_call(
        functools.partial(_conv_kernel, t0),
        grid=(T // CV_TM - t0,),
        in_specs=[
            pl.BlockSpec((CV_HALO, W_CONV_IN), lambda i: (jnp.maximum((i + t0) * hb - 1, 0), 0)),
            pl.BlockSpec((CV_TM, W_CONV_IN), lambda i: (i + t0, 0)),
            pl.BlockSpec((CV_HALO, W_CONV_IN), lambda i: (jnp.minimum((i + t0 + 1) * hb, last), 0)),
            pl.BlockSpec((CONV_K, CONV_CH), lambda i: (0, 0)),
            vec, vec, vec,
        ],
        out_specs=pl.BlockSpec((CV_TM, CONV_CH), lambda i: (i + t0, 0)),
        out_shape=jax.ShapeDtypeStruct((T, CONV_CH), BF16),
        scratch_shapes=[pltpu.VMEM((CV_TM + 2 * CV_HALO, CONV_CH), F32),
                        pltpu.VMEM((CV_TM, CONV_CH), F32)],
        compiler_params=_cparams("arbitrary"),
        name="conv_module",
    )(u, u, u, conv_w, conv_b.reshape(1, CONV_CH), ln_g.reshape(1, CONV_CH), ln_b.reshape(1, CONV_CH))


HN_TM = 256


def _headnorm_kernel(qw_ref, qn_ref, kvw_ref, kn_ref, vn_ref, cos_ref, sin_ref, g_ref,
                     qw_o, qn_o, kvw_o, kn_o, vn_o):
    cosf = cos_ref[...]
    sinf = sin_ref[...]
    lane = lax.broadcasted_iota(jnp.int32, (HN_TM, HD), 1)
    first_half = (lane % (HD // 2)) < (HD // 4)

    def norm(x, gi):
        ms = jnp.mean(x * x, axis=-1, keepdims=True)
        return x * lax.rsqrt(ms + EPS) * g_ref[gi:gi + 1, :]

    def rope(x):
        partner = jnp.where(first_half, pltpu.roll(x, HD - HD // 4, 1), pltpu.roll(x, HD // 4, 1))
        return x * cosf + partner * sinf

    for hd in range(WIN_HQ):
        sl = slice(hd * HD, (hd + 1) * HD)
        qw_o[:, sl] = rope(norm(qw_ref[:, sl], 0)).astype(BF16)
    for hd in range(WIN_HKV):
        sl = slice(hd * HD, (hd + 1) * HD)
        kvw_o[:, sl] = rope(norm(kvw_ref[:, sl], 1)).astype(BF16)
    kvw_o[:, W_WIN_KV:] = kvw_ref[:, W_WIN_KV:].astype(BF16)
    for hd in range(NA_H):
        sl = slice(hd * HD, (hd + 1) * HD)
        qn_o[:, sl] = norm(qn_ref[:, sl], 2).astype(BF16)
        kn_o[:, sl] = norm(kn_ref[:, sl], 3).astype(BF16)
    vn_o[...] = vn_ref[...].astype(BF16)


def _headnorm(u, cosf, sinf, gains):
    tm = HN_TM
    kvb = KV_OFF // W_NA_KV
    row128 = pl.BlockSpec((tm, HD), lambda i: (i, 0))

    def ospec(w):
        return pl.BlockSpec((tm, w), lambda i: (i, 0))

    def oshape(w):
        return jax.ShapeDtypeStruct((T, w), BF16)

    return pl.pallas_call(
        _headnorm_kernel,
        grid=(T // tm,),
        in_specs=[
            pl.BlockSpec((tm, W_WIN_Q), lambda i: (i, Q_WIN_OFF // W_WIN_Q)),
            pl.BlockSpec((tm, W_NA_Q), lambda i: (i, Q_NA_OFF // W_NA_Q)),
            pl.BlockSpec((tm, 2 * W_WIN_KV), lambda i: (i, kvb)),
            pl.BlockSpec((tm, W_NA_KV), lambda i: (i, kvb + 1)),
            pl.BlockSpec((tm, W_NA_KV), lambda i: (i, kvb + 2)),
            row128, row128,
            pl.BlockSpec((8, HD), lambda i: (0, 0)),
        ],
        out_specs=[ospec(W_WIN_Q), ospec(W_NA_Q), ospec(2 * W_WIN_KV), ospec(W_NA_KV), ospec(W_NA_KV)],
        out_shape=[oshape(W_WIN_Q), oshape(W_NA_Q), oshape(2 * W_WIN_KV), oshape(W_NA_KV), oshape(W_NA_KV)],
        compiler_params=_cparams("arbitrary"),
        name="headnorm_rope",
    )(u, u, u, u, u, cosf, sinf, gains)


def _rope_tables():
    t = jnp.arange(S)
    row = (t // GRID_W).astype(F32)
    col = (t % GRID_W).astype(F32)
    axis_dim = HD // 2
    inv = ROPE_BASE ** (-jnp.arange(0, axis_dim, 2, dtype=F32) / axis_dim)
    ar = row[:, None] * inv
    ac = col[:, None] * inv
    cosl = jnp.concatenate([jnp.cos(ar), jnp.cos(ar), jnp.cos(ac), jnp.cos(ac)], axis=1)
    sinl = jnp.concatenate([-jnp.sin(ar), jnp.sin(ar), -jnp.sin(ac), jnp.sin(ac)], axis=1)
    cosf = jnp.concatenate([jnp.ones((CTX_ROWS, HD), F32), cosl, cosl], axis=0)
    sinf = jnp.concatenate([jnp.zeros((CTX_ROWS, HD), F32), sinl, sinl], axis=0)
    return cosf, sinf


WA_TQ = 128
WA_CTX_TILES = CTX_ROWS // WA_TQ
WA_LAT_TILES = S // WA_TQ


def _win_attn_kernel(t0, sink_ref, q_ref, kp_ref, kc_ref, kn_ref, kx_ref, o_ref):
    i = pl.program_id(0) + t0
    is_ctx = i < WA_CTX_TILES
    n = (i - WA_CTX_TILES) % WA_LAT_TILES
    far = 4 * WA_TQ
    off_prev = jnp.where(jnp.logical_or(is_ctx, n == 0), far, 0)
    off_cur = jnp.where(is_ctx, far, 0)
    off_next = jnp.where(jnp.logical_or(is_ctx, n == WA_LAT_TILES - 1), far, 0)
    rows = WIN_G * WA_TQ
    r = lax.broadcasted_iota(jnp.int32, (rows, WA_TQ), 0) % WA_TQ
    c = lax.broadcasted_iota(jnp.int32, (rows, WA_TQ), 1)
    m_prev = c >= r + off_prev
    m_cur = c >= off_cur
    m_next = c + off_next <= r
    grp = lax.broadcasted_iota(jnp.int32, (rows, 1), 0) // WA_TQ

    for h in range(WIN_HKV):
        q = jnp.concatenate([q_ref[:, (h * WIN_G + g) * HD:(h * WIN_G + g + 1) * HD] for g in range(WIN_G)], axis=0)
        ks = slice(h * HD, (h + 1) * HD)
        vs = slice(W_WIN_KV + h * HD, W_WIN_KV + (h + 1) * HD)
        s_p = jnp.where(m_prev, _dot_nt(q, kp_ref[:, ks]) * ATT_SCALE, NEG)
        s_c = jnp.where(m_cur, _dot_nt(q, kc_ref[:, ks]) * ATT_SCALE, NEG)
        s_n = jnp.where(m_next, _dot_nt(q, kn_ref[:, ks]) * ATT_SCALE, NEG)
        s_x = _dot_nt(q, kx_ref[:, ks]) * ATT_SCALE
        snk = jnp.zeros((rows, 1), F32)
        for g in range(WIN_G):
            snk = jnp.where(grp == g, sink_ref[h * WIN_G + g], snk)
        m = jnp.maximum(jnp.maximum(jnp.max(s_p, axis=-1, keepdims=True), jnp.max(s_c, axis=-1, keepdims=True)),
                        jnp.maximum(jnp.max(s_n, axis=-1, keepdims=True), jnp.max(s_x, axis=-1, keepdims=True)))
        m = jnp.maximum(m, snk)
        p_p = jnp.exp(s_p - m)
        p_c = jnp.exp(s_c - m)
        p_n = jnp.exp(s_n - m)
        p_x = jnp.exp(s_x - m)
        den = (jnp.sum(p_p, axis=-1, keepdims=True) + jnp.sum(p_c, axis=-1, keepdims=True)
               + jnp.sum(p_n, axis=-1, keepdims=True) + jnp.sum(p_x, axis=-1, keepdims=True) + jnp.exp(snk - m))
        o = (jnp.dot(p_p.astype(BF16), kp_ref[:, vs], preferred_element_type=F32)
             + jnp.dot(p_c.astype(BF16), kc_ref[:, vs], preferred_element_type=F32)
             + jnp.dot(p_n.astype(BF16), kn_ref[:, vs], preferred_element_type=F32)
             + jnp.dot(p_x.astype(BF16), kx_ref[:, vs], preferred_element_type=F32))
        o = o / den
        for g in range(WIN_G):
            o_ref[:, (h * WIN_G + g) * HD:(h * WIN_G + g + 1) * HD] = o[g * WA_TQ:(g + 1) * WA_TQ].astype(BF16)


def _win_attention(qw, kvw, sink, latent_only=False):
    nt = T // WA_TQ
    t0 = _first_tile(WA_TQ, latent_only)

    def bounds(i):
        is_ctx = i < WA_CTX_TILES
        b = (i - WA_CTX_TILES) // WA_LAT_TILES
        lo = jnp.where(is_ctx, 0, WA_CTX_TILES + b * WA_LAT_TILES)
        hi = jnp.where(is_ctx, nt - 1, WA_CTX_TILES + (b + 1) * WA_LAT_TILES - 1)
        return lo, hi

    def prev_map(g):
        i = g + t0
        lo, _ = bounds(i)
        return (jnp.maximum(i - 1, lo), 0)

    def next_map(g):
        i = g + t0
        _, hi = bounds(i)
        return (jnp.minimum(i + 1, hi), 0)

    def ctx_map(g):
        i = g + t0
        b = jnp.where(i < WA_CTX_TILES, i // (L // WA_TQ), (i - WA_CTX_TILES) // WA_LAT_TILES)
        return (b, 0)

    kvw_w = 2 * W_WIN_KV
    return pl.pallas_call(
        functools.partial(_win_attn_kernel, t0),
        grid=(nt - t0,),
        in_specs=[
            pl.BlockSpec(memory_space=pltpu.SMEM),
            pl.BlockSpec((WA_TQ, W_WIN_Q), lambda g: (g + t0, 0)),
            pl.BlockSpec((WA_TQ, kvw_w), prev_map),
            pl.BlockSpec((WA_TQ, kvw_w), lambda g: (g + t0, 0)),
            pl.BlockSpec((WA_TQ, kvw_w), next_map),
            pl.BlockSpec((L, kvw_w), ctx_map),
        ],
        out_specs=pl.BlockSpec((WA_TQ, W_WIN_Q), lambda g: (g + t0, 0)),
        out_shape=jax.ShapeDtypeStruct((T, W_WIN_Q), BF16),
        compiler_params=_cparams("arbitrary"),
        name="window_attention",
    )(sink, qw, kvw, kvw, kvw, kvw)


NA_TQ = 256
NA_QROWS = NA_TQ // GRID_W
NA_KROWS = 3 * NA_QROWS
NA_NKEY = NA_KROWS * GRID_W
NA_TYPES = 4
RPB_R = 2 * NA_KH - 1
RPB_C = 2 * NA_KW - 1


def _na_row_valid(ty, a, j):
    if ty == 0:
        return NA_QROWS <= j < NA_QROWS + NA_KH
    if ty == 1:
        return a <= j < a + NA_KH
    if ty == 2:
        return j < NA_KH
    return False


def _rpb_kernel(rpb_ref, o_ref):
    h = pl.program_id(0)
    shp = (GRID_W, 2 * GRID_W)
    qc = lax.broadcasted_iota(jnp.int32, shp, 0)
    lane = lax.broadcasted_iota(jnp.int32, shp, 1)
    kc = lane % GRID_W
    second = lane >= GRID_W
    dcol = kc - qc + (NA_KW - 1)
    cs = jnp.clip(qc - NA_KW // 2, 0, GRID_W - NA_KW)
    colmask = jnp.logical_and(kc >= cs, kc < cs + NA_KW)
    neg = jnp.full(shp, NEG, F32)
    base = h * (RPB_R * RPB_C)
    pair = []
    for dr in range(RPB_R - 1):
        acc = jnp.zeros(shp, F32)
        for dd in range(RPB_C):
            v0 = rpb_ref[base + dr * RPB_C + dd]
            v1 = rpb_ref[base + (dr + 1) * RPB_C + dd]
            acc = jnp.where(dcol == dd, jnp.where(second, v1, v0), acc)
        pair.append(jnp.where(colmask, acc, neg))
    for ty in range(NA_TYPES):
        for a in range(NA_QROWS):
            for jp in range(NA_KROWS // 2):
                j = 2 * jp
                ok0 = _na_row_valid(ty, a, j)
                ok1 = _na_row_valid(ty, a, j + 1)
                dr = j - a + NA_QROWS - 1
                if ok0 and ok1:
                    tile = pair[dr]
                elif ok0:
                    tile = jnp.where(second, neg, pair[dr])
                elif ok1:
                    tile = jnp.where(second, pair[dr], neg)
                else:
                    tile = neg
                o_ref[ty, 0, a * GRID_W:(a + 1) * GRID_W, jp * 2 * GRID_W:(jp + 1) * 2 * GRID_W] = tile


def _rpb_tiles(rpb):
    return pl.pallas_call(
        _rpb_kernel,
        grid=(NA_H,),
        in_specs=[pl.BlockSpec(memory_space=pltpu.SMEM)],
        out_specs=pl.BlockSpec((NA_TYPES, 1, NA_TQ, NA_NKEY), lambda h: (0, h, 0, 0)),
        out_shape=jax.ShapeDtypeStruct((NA_TYPES, NA_H, NA_TQ, NA_NKEY), F32),
        compiler_params=_cparams("arbitrary"),
        name="rpb_tiles",
    )(rpb.reshape(-1))


NA_CTX_TILES = CTX_ROWS // NA_TQ
NA_LAT_TILES = S // NA_TQ


def _na_attn_kernel(q_ref, kp_ref, kc_ref, kn_ref, vp_ref, vc_ref, vn_ref, kx_ref, vx_ref, bias_ref, o_ref):
    for h in range(NA_H):
        hs = slice(h * HD, (h + 1) * HD)
        q = q_ref[:, hs]
        s_p = _dot_nt(q, kp_ref[:, hs]) * ATT_SCALE + bias_ref[0, h, :, 0:NA_TQ]
        s_c = _dot_nt(q, kc_ref[:, hs]) * ATT_SCALE + bias_ref[0, h, :, NA_TQ:2 * NA_TQ]
        s_n = _dot_nt(q, kn_ref[:, hs]) * ATT_SCALE + bias_ref[0, h, :, 2 * NA_TQ:3 * NA_TQ]
        s_x = _dot_nt(q, kx_ref[:, hs]) * ATT_SCALE
        m = jnp.maximum(jnp.maximum(jnp.max(s_p, axis=-1, keepdims=True), jnp.max(s_c, axis=-1, keepdims=True)),
                        jnp.maximum(jnp.max(s_n, axis=-1, keepdims=True), jnp.max(s_x, axis=-1, keepdims=True)))
        p_p = jnp.exp(s_p - m)
        p_c = jnp.exp(s_c - m)
        p_n = jnp.exp(s_n - m)
        p_x = jnp.exp(s_x - m)
        den = (jnp.sum(p_p, axis=-1, keepdims=True) + jnp.sum(p_c, axis=-1, keepdims=True)
               + jnp.sum(p_n, axis=-1, keepdims=True) + jnp.sum(p_x, axis=-1, keepdims=True))
        o = (jnp.dot(p_p.astype(BF16), vp_ref[:, hs], preferred_element_type=F32)
             + jnp.dot(p_c.astype(BF16), vc_ref[:, hs], preferred_element_type=F32)
             + jnp.dot(p_n.astype(BF16), vn_ref[:, hs], preferred_element_type=F32)
             + jnp.dot(p_x.astype(BF16), vx_ref[:, hs], preferred_element_type=F32))
        o_ref[:, hs] = (o / den).astype(BF16)


def _na_attention(qn, kn, vn, bias, latent_only=False):
    nt = T // NA_TQ
    t0 = _first_tile(NA_TQ, latent_only)

    def bounds(i):
        is_ctx = i < NA_CTX_TILES
        b = (i - NA_CTX_TILES) // NA_LAT_TILES
        lo = jnp.where(is_ctx, 0, NA_CTX_TILES + b * NA_LAT_TILES)
        hi = jnp.where(is_ctx, nt - 1, NA_CTX_TILES + (b + 1) * NA_LAT_TILES - 1)
        return lo, hi

    def prev_map(g):
        i = g + t0
        lo, _ = bounds(i)
        return (jnp.maximum(i - 1, lo), 0)

    def next_map(g):
        i = g + t0
        _, hi = bounds(i)
        return (jnp.minimum(i + 1, hi), 0)

    def ctx_map(g):
        i = g + t0
        return (jnp.where(i < NA_CTX_TILES, i, (i - NA_CTX_TILES) // NA_LAT_TILES), 0)

    def bias_map(g):
        i = g + t0
        n = (i - NA_CTX_TILES) % NA_LAT_TILES
        ty = jnp.where(i < NA_CTX_TILES, 3, jnp.where(n == 0, 0, jnp.where(n == NA_LAT_TILES - 1, 2, 1)))
        return (ty, 0, 0, 0)

    w = W_NA_KV
    cur = pl.BlockSpec((NA_TQ, w), lambda g: (g + t0, 0))
    prv = pl.BlockSpec((NA_TQ, w), prev_map)
    nxt = pl.BlockSpec((NA_TQ, w), next_map)
    ctx = pl.BlockSpec((L, w), ctx_map)
    return pl.pallas_call(
        _na_attn_kernel,
        grid=(nt - t0,),
        in_specs=[cur, prv, cur, nxt, prv, cur, nxt, ctx, ctx,
                  pl.BlockSpec((1, NA_H, NA_TQ, NA_NKEY), bias_map)],
        out_specs=cur,
        out_shape=jax.ShapeDtypeStruct((T, W_NA_Q), BF16),
        compiler_params=_cparams("arbitrary"),
        name="neighborhood_attention",
    )(qn, kn, kn, kn, vn, vn, vn, kn, vn, bias)


MG_TM = 512
MG_TN = 1024


def _merge_kernel(ca_ref, aw_ref, an_ref, wc_ref, ww_ref, wn_ref, ga_ref, gb_ref, gc_ref, o_ref):
    ya = jnp.dot(ca_ref[...], wc_ref[...], preferred_element_type=F32)
    yb = jnp.dot(aw_ref[...], ww_ref[...], preferred_element_type=F32)
    yc = jnp.dot(an_ref[...], wn_ref[...], preferred_element_type=F32)
    o = _sigmoid(ga_ref[...]) * ya + _sigmoid(gb_ref[...]) * yb + _sigmoid(gc_ref[...]) * yc
    o_ref[...] = o.astype(BF16)


def _merge(hconv, aw, an, w_conv_out, w_win_out, w_na_out, u, latent_only=False):
    gb0 = GATE_OFF // MG_TN
    gstep = D // MG_TN
    t0 = _first_tile(MG_TM, latent_only)

    def a_spec(k):
        return pl.BlockSpec((MG_TM, k), lambda i, j: (i + t0, 0))

    def w_spec(k):
        return pl.BlockSpec((k, MG_TN), lambda i, j: (0, j))

    def g_spec(which):
        return pl.BlockSpec((MG_TM, MG_TN), lambda i, j: (i + t0, gb0 + which * gstep + j))

    return pl.pallas_call(
        _merge_kernel,
        grid=(T // MG_TM - t0, D // MG_TN),
        in_specs=[a_spec(CONV_CH), a_spec(W_WIN_Q), a_spec(W_NA_Q),
                  w_spec(CONV_CH), w_spec(W_WIN_Q), w_spec(W_NA_Q),
                  g_spec(0), g_spec(1), g_spec(2)],
        out_specs=pl.BlockSpec((MG_TM, MG_TN), lambda i, j: (i + t0, j)),
        out_shape=jax.ShapeDtypeStruct((T, D), BF16),
        compiler_params=_cparams("arbitrary", "arbitrary"),
        name="gated_merge",
    )(hconv, aw, an, w_conv_out, w_win_out, w_na_out, u, u, u)


TK_TT = 256


def _topk_rounds(scores, n_rounds):
    nrow = scores.shape[0]
    idx = lax.broadcasted_iota(jnp.int32, scores.shape, 0).astype(F32)
    work = scores
    rank = jnp.full(scores.shape, float(n_rounds), F32)
    vals = []
    for a in range(n_rounds):
        m = jnp.max(work, axis=0, keepdims=True)
        first = jnp.min(jnp.where(work == m, idx, float(nrow)), axis=0, keepdims=True)
        sel = idx == first
        rank = jnp.where(sel, float(a), rank)
        work = jnp.where(sel, -jnp.inf, work)
        vals.append(m)
    return jnp.concatenate(vals, axis=0), rank


def _topk_rounds_no_ties(scores, n_rounds):
    work = scores
    rank = jnp.full(scores.shape, float(n_rounds), F32)
    vals = []
    for a in range(n_rounds):
        m = jnp.max(work, axis=0, keepdims=True)
        sel = work == m
        rank = jnp.where(sel, float(a), rank)
        work = jnp.where(sel, -jnp.inf, work)
        vals.append(m)
    taken = jnp.sum(jnp.where(rank < float(n_rounds), 1.0, 0.0), axis=0, keepdims=True)
    return jnp.concatenate(vals, axis=0), rank, taken


def _peer_topk_kernel(q_ref, k1_ref, k2_ref, n_ref, e1_ref, r2_ref, e2_ref):
    kk = PEER_TOPK

    def head(h, carry):
        c1 = pl.multiple_of(h * 2 * HD, 2 * HD)
        q1 = q_ref[:, pl.ds(c1, HD)]
        q2 = q_ref[:, pl.ds(c1 + HD, HD)]
        s1 = _dot_nt(k1_ref[h].astype(BF16), q1)
        s2 = _dot_nt(k2_ref[h].astype(BF16), q2)
        f1, fr1, t1 = _topk_rounds_no_ties(s1, kk)
        f2, fr2, t2 = _topk_rounds_no_ties(s2, kk)
        tied = jnp.max(jnp.abs(t1 - float(kk)) + jnp.abs(t2 - float(kk))) > 0.0

        def exact(_):
            return _topk_rounds(s1, kk) + _topk_rounds(s2, kk)

        def keep(_):
            return f1, fr1, f2, fr2

        v1all, rank1, v2all, rank2 = lax.cond(tied, exact, keep, None)
        v1 = [v1all[0:1, :]]
        v2 = [v2all[0:1, :]]
        arow = lax.broadcasted_iota(jnp.int32, v2all.shape, 0).astype(F32)
        cnt = jnp.zeros(v2all.shape, F32)
        front = v1all + v2[0]
        top = v1[0] + v2[0]
        z = jnp.zeros_like(top)
        for _ in range(kk):
            m = jnp.max(front, axis=0, keepdims=True)
            first = jnp.min(jnp.where(front == m, arow, float(kk)), axis=0, keepdims=True)
            sel = arow == first
            cnt = cnt + jnp.where(sel, 1.0, 0.0)
            z = z + jnp.exp(m - top)
            taken = jnp.max(jnp.where(sel, cnt, -1.0), axis=0, keepdims=True)
            nxt = jnp.max(jnp.where(arow == taken, v2all, -jnp.inf), axis=0, keepdims=True)
            front = jnp.where(sel, v1all + nxt, front)
        nfull = jnp.zeros(s1.shape, F32)
        for a in range(kk):
            nfull = jnp.where(rank1 == float(a), cnt[a:a + 1, :], nfull)
        n_ref[h] = nfull
        e1_ref[h] = jnp.exp(s1 - v1[0]) / z
        r2_ref[h] = rank2.astype(BF16)
        e2_ref[h] = jnp.exp(s2 - v2[0]).astype(BF16)
        return carry

    lax.fori_loop(0, PEER_HEADS, head, 0)


def _peer_topk(q, k1, k2, latent_only=False):
    tt = TK_TT
    t0 = _first_tile(tt, latent_only)
    kspec = pl.BlockSpec((PEER_HEADS, N_KEYS, HD), lambda i: (0, 0, 0))
    ospec = pl.BlockSpec((PEER_HEADS, N_KEYS, tt), lambda i: (0, 0, i + t0))
    oshape = jax.ShapeDtypeStruct((PEER_HEADS, N_KEYS, T), F32)
    oshape_b = jax.ShapeDtypeStruct((PEER_HEADS, N_KEYS, T), BF16)
    return pl.pallas_call(
        _peer_topk_kernel,
        grid=(T // tt - t0,),
        in_specs=[pl.BlockSpec((tt, 2 * HD * PEER_HEADS), lambda i: (i + t0, 0)), kspec, kspec],
        out_specs=[ospec] * 4,
        out_shape=[oshape, oshape, oshape_b, oshape_b],
        compiler_params=_cparams("arbitrary"),
        name="peer_topk",
    )(q, k1, k2)


EX_TM = 512
EX_TN = 512
EX_SUB = 256
EX_NC = 1024
EX_LC = 128
EX_RC = 16
EX_I1 = EX_TN // N_KEYS
EX_STEPS_PER_GROUP = 8 // EX_I1
assert EX_I1 * EX_STEPS_PER_GROUP == 8 and EX_STEPS_PER_GROUP == 2


def _gelu_tanh(x):
    cdf = 0.5 * (1.0 + jnp.tanh(math.sqrt(2.0 / math.pi) * (x + 0.044715 * (x * x * x))))
    return x * cdf


def _experts_kernel(h_ref, u_ref, v_ref, n_ref, e1_ref, r2_ref, e2_ref, o_ref):
    j = pl.program_id(1)

    @pl.when(j == 0)
    def _():
        o_ref[...] = jnp.zeros_like(o_ref)

    grp = pl.multiple_of((j // EX_STEPS_PER_GROUP) * 8, 8)
    upper = (j % EX_STEPS_PER_GROUP) == 1
    key_rows = {}
    for hh in range(PEER_HEADS):
        for c in range(EX_TM // EX_LC):
            cs = slice(c * EX_LC, (c + 1) * EX_LC)
            n8 = n_ref[hh, pl.ds(grp, 8), cs]
            e8 = e1_ref[hh, pl.ds(grp, 8), cs]
            for k in range(EX_I1):
                nrow = jnp.where(upper, n8[EX_I1 + k:EX_I1 + k + 1], n8[k:k + 1])
                e1row = jnp.where(upper, e8[EX_I1 + k:EX_I1 + k + 1], e8[k:k + 1])
                key_rows[(k, hh, c)] = (jnp.broadcast_to(nrow, (EX_RC, EX_LC)).astype(BF16),
                                        jnp.broadcast_to(e1row, (EX_RC, EX_LC)).astype(BF16))

    def gate_tile(k, rows, c):
        cs = slice(c * EX_LC, (c + 1) * EX_LC)
        gate = jnp.zeros((EX_RC, EX_LC), BF16)
        for hh in range(PEER_HEADS):
            nrow, e1row = key_rows[(k, hh, c)]
            gate = gate + jnp.where(r2_ref[hh, rows, cs] < nrow, e2_ref[hh, rows, cs], 0.0) * e1row
        return gate

    w_parts = []
    for s in range(EX_TN // EX_SUB):
        es = slice(s * EX_SUB, (s + 1) * EX_SUB)
        act = _gelu_tanh(jnp.dot(u_ref[es, :], h_ref[...], preferred_element_type=F32)).astype(BF16)
        for ii in range(EX_SUB // N_KEYS):
            k = s * (EX_SUB // N_KEYS) + ii
            for rc in range(N_KEYS // EX_RC):
                rows = slice(rc * EX_RC, (rc + 1) * EX_RC)
                arows = slice(ii * N_KEYS + rc * EX_RC, ii * N_KEYS + (rc + 1) * EX_RC)
                row = []
                for c in range(EX_TM // EX_LC):
                    cs = slice(c * EX_LC, (c + 1) * EX_LC)
                    row.append(gate_tile(k, rows, c) * act[arows, cs])
                w_parts.append(jnp.concatenate(row, axis=1))
    w_t = jnp.concatenate(w_parts, axis=0)
    for nc in range(D // EX_NC):
        ns = slice(nc * EX_NC, (nc + 1) * EX_NC)
        o_ref[:, ns] += lax.dot_general(w_t, v_ref[:, ns], (((0,), (0,)), ((), ())),
                                        preferred_element_type=F32)


def _experts(h_t, u, v, tables, latent_only=False):
    n, e1, r2, e2 = tables
    t0 = _first_tile(EX_TM, latent_only)
    tspec = pl.BlockSpec((PEER_HEADS, N_KEYS, EX_TM), lambda i, j: (0, 0, i + t0))
    return pl.pallas_call(
        _experts_kernel,
        grid=(T // EX_TM - t0, N_EXPERTS // EX_TN),
        in_specs=[pl.BlockSpec((D, EX_TM), lambda i, j: (0, i + t0)),
                  pl.BlockSpec((EX_TN, D), lambda i, j: (j, 0)),
                  pl.BlockSpec((EX_TN, D), lambda i, j: (j, 0)),
                  tspec, tspec, tspec, tspec],
        out_specs=pl.BlockSpec((EX_TM, D), lambda i, j: (i + t0, 0), pipeline_mode=pl.Buffered(1)),
        out_shape=jax.ShapeDtypeStruct((T, D), F32),
        compiler_params=_cparams("arbitrary", "arbitrary"),
        name="peer_experts",
    )(h_t, u, v, n, e1, r2, e2)


FR_TM = 256


def _final_kernel(x_ref, f_ref, g_ref, o_ref):
    o_ref[...] = x_ref[...] + g_ref[0, 0] * f_ref[...]


def _final_residual(x, f, modt, layer, g_k):
    off = CTX_ROWS // FR_TM
    lat = pl.BlockSpec((FR_TM, D), lambda i: (i + off, 0))
    return pl.pallas_call(
        _final_kernel,
        grid=(B * S // FR_TM,),
        in_specs=[lat, lat,
                  pl.BlockSpec((1, 1, 1, D), lambda i: (layer, _group_of_row((i + off) * FR_TM), 0, g_k))],
        out_specs=pl.BlockSpec((FR_TM, D), lambda i: (i, 0)),
        out_shape=jax.ShapeDtypeStruct((B * S, D), F32),
        compiler_params=_cparams("arbitrary"),
        name="final_residual",
    )(x, f, modt)


SH1, SC1, G1, SH2, SC2, G2 = range(6)


def kernel(x, c, ctx, c_ctx, w_ada, b_ada, norm1_g, norm2_g, w_in, conv_w, conv_b, conv_ln_g, conv_ln_b,
           w_conv_out, win_qn_g, win_kn_g, win_sink, w_win_out, na_qn_g, na_kn_g, na_rpb, w_na_out, w_out,
           peer_wq, peer_k1, peer_k2, peer_u, peer_v):
    xs = jnp.concatenate([ctx.reshape(CTX_ROWS, D), x.reshape(B * S, D)], axis=0)
    cvec = jnp.concatenate([c_ctx[None], c, jnp.zeros((N_GROUPS - 1 - B, D), F32)], axis=0)
    modt = _mods(cvec, w_ada, b_ada).reshape(DEPTH, N_GROUPS, 1, 6 * D)
    cosf, sinf = _rope_tables()

    f = None
    for l in range(DEPTH):
        if l == 0:
            h1 = _normmod(xs, modt, l, norm1_g[l], SC1, SH1)
        else:
            xs, h1 = _normmod(xs, modt, l, norm1_g[l], SC1, SH1, f=f, g_k=G2)
        lat = l == DEPTH - 1
        u = _matmul_wstat(h1, w_in, l, F32, 1024)
        hconv = _conv_module(u, conv_w[l], conv_b[l], conv_ln_g[l], conv_ln_b[l], latent_only=lat)
        gains = jnp.concatenate([win_qn_g[l][None], win_kn_g[l][None], na_qn_g[l][None], na_kn_g[l][None],
                                 jnp.zeros((4, HD), F32)], axis=0)
        qw, qn, kvw, kn, vn = _headnorm(u, cosf, sinf, gains)
        aw = _win_attention(qw, kvw, win_sink[l], latent_only=lat)
        an = _na_attention(qn, kn, vn, _rpb_tiles(na_rpb[l]), latent_only=lat)
        merged = _merge(hconv, aw, an, _cast_bf16(w_conv_out, l), _cast_bf16(w_win_out, l),
                        _cast_bf16(w_na_out, l), u, latent_only=lat)
        xs = _matmul_resid(merged, _cast_bf16(w_out, l), xs, modt, l, G1, 1024, latent_only=lat)
        h2, h2_t = _normmod(xs, modt, l, norm2_g[l], SC2, SH2, want_t=True, latent_only=lat)
        q = _matmul(h2, _cast_bf16(peer_wq, l), BF16, 1024, latent_only=lat)
        tables = _peer_topk(q, peer_k1[l], peer_k2[l], latent_only=lat)
        f = _experts(h2_t, _cast_bf16(peer_u, l), _cast_bf16(peer_v, l), tables, latent_only=lat)
    out = _final_residual(xs, f, modt, DEPTH - 1, G2)
    return out.reshape(B, S, D)
```

```python
import functools
import math

import jax
import jax.numpy as jnp
import numpy as np
from jax import lax
from jax.experimental import pallas as pl
from jax.experimental.pallas import tpu as pltpu

F32 = jnp.float32
BF16 = jnp.bfloat16

D = 4096
B = 2
S = 4096
L = 256
DEPTH = 2
GRID_W = 64
HD = 128
EPS = 1e-6
NEG = -1e30
ROPE_BASE = 10000.0

CONV_CH = D // 4
CONV_K = 31
WIN_HQ = 16
WIN_HKV = 4
WIN_G = WIN_HQ // WIN_HKV
NA_H = 8
NA_KH = 8
NA_KW = 16
PEER_HEADS = 8
N_KEYS = 128
N_EXPERTS = N_KEYS * N_KEYS
PEER_TOPK = 16

W_CONV_IN = 2 * CONV_CH
W_WIN_Q = WIN_HQ * HD
W_NA_Q = NA_H * HD
W_GATE = 3 * D
W_WIN_KV = WIN_HKV * HD
W_NA_KV = NA_H * HD
Q_WIN_OFF = W_CONV_IN
Q_NA_OFF = Q_WIN_OFF + W_WIN_Q
GATE_OFF = Q_NA_OFF + W_NA_Q
KV_OFF = GATE_OFF + W_GATE
IN_COLS = KV_OFF + 2 * W_WIN_KV + 2 * W_NA_KV

CTX_ROWS = B * L
T = CTX_ROWS + B * S
N_GROUPS = 8
ATT_SCALE = HD ** -0.5

VMEM_LIMIT = 56 * 1024 * 1024


def _cparams(*sem):
    return pltpu.CompilerParams(dimension_semantics=sem, vmem_limit_bytes=VMEM_LIMIT)


def _group_of_row(row0):
    return jnp.where(row0 < CTX_ROWS, 0, 1 + (row0 - CTX_ROWS) // S)


def _sigmoid(z):
    return 0.5 * jnp.tanh(0.5 * z) + 0.5


def _dot_nt(a, b):
    return lax.dot_general(a, b, (((1,), (1,)), ((), ())), preferred_element_type=F32)


def _first_tile(tile, latent_only):
    return CTX_ROWS // tile if latent_only else 0


CAST_BLOCK_BYTES = 8 * 1024 * 1024


def _cast_kernel(w_ref, o_ref):
    o_ref[...] = w_ref[0].astype(BF16)


def _cast_bf16(w, layer):
    _, m, n = w.shape
    rb = m
    while rb * n * 4 > CAST_BLOCK_BYTES:
        rb //= 2
    return pl.pallas_call(
        _cast_kernel,
        grid=(m // rb,),
        in_specs=[pl.BlockSpec((1, rb, n), lambda i: (layer, i, 0))],
        out_specs=pl.BlockSpec((rb, n), lambda i: (i, 0)),
        out_shape=jax.ShapeDtypeStruct((m, n), BF16),
        compiler_params=_cparams("arbitrary"),
        name="cast_bf16",
    )(w)


MOD_TN = 512


def _mods_kernel(c_ref, w_ref, b_ref, o_ref):
    cv = c_ref[...]
    a = (cv * _sigmoid(cv)).astype(BF16)
    w = w_ref[0].astype(BF16)
    o_ref[0] = jnp.dot(a, w, preferred_element_type=F32) + b_ref[0]


def _mods(cvec, w_ada, b_ada):
    n = 6 * D
    return pl.pallas_call(
        _mods_kernel,
        grid=(DEPTH, n // MOD_TN),
        in_specs=[
            pl.BlockSpec((N_GROUPS, D), lambda l, j: (0, 0)),
            pl.BlockSpec((1, D, MOD_TN), lambda l, j: (l, 0, j)),
            pl.BlockSpec((1, 1, MOD_TN), lambda l, j: (l, 0, j)),
        ],
        out_specs=pl.BlockSpec((1, N_GROUPS, MOD_TN), lambda l, j: (l, 0, j)),
        out_shape=jax.ShapeDtypeStruct((DEPTH, N_GROUPS, n), F32),
        compiler_params=_cparams("arbitrary", "arbitrary"),
        name="adaln_mods",
    )(cvec, w_ada, b_ada.reshape(DEPTH, 1, n))


NM_TM = 256


def _normmod_kernel(has_f, want_t, ctx_tiles, *refs):
    refs = list(refs)
    ht_ref = refs.pop() if want_t else None
    if ctx_tiles:
        c_ref, x_ref, ng_ref, sc_ref, sh_ref, h_ref = refs
        x = jnp.where(pl.program_id(0) < ctx_tiles, c_ref[...], x_ref[...])
    elif has_f:
        x_ref, f_ref, g_ref, ng_ref, sc_ref, sh_ref, xo_ref, h_ref = refs
        x = x_ref[...] + g_ref[0, 0] * f_ref[...]
        xo_ref[...] = x
    else:
        x_ref, ng_ref, sc_ref, sh_ref, h_ref = refs
        x = x_ref[...]
    ms = jnp.mean(x * x, axis=-1, keepdims=True)
    y = x * lax.rsqrt(ms + EPS) * ng_ref[...]
    h = y * (1.0 + sc_ref[0, 0]) + sh_ref[0, 0]
    h_ref[...] = h.astype(BF16)
    if want_t:
        ht_ref[...] = h.T.astype(BF16)


def _mod_spec(layer, k, tm, t0=0):
    return pl.BlockSpec((1, 1, 1, D), lambda i: (layer, _group_of_row((i + t0) * tm), 0, k))


def _normmod(x, modt, layer, norm_g, sc_k, sh_k, f=None, g_k=None, want_t=False, latent_only=False):
    tm = NM_TM
    t0 = _first_tile(tm, latent_only)
    row = pl.BlockSpec((tm, D), lambda i: (i + t0, 0))
    has_f = f is not None
    ctx_tiles = 0
    if isinstance(x, tuple):
        ctx_tiles = CTX_ROWS // tm
        in_specs = [pl.BlockSpec((tm, D), lambda i: (jnp.minimum(i, ctx_tiles - 1), 0)),
                    pl.BlockSpec((tm, D), lambda i: (jnp.maximum(i - ctx_tiles, 0), 0))]
        args = list(x)
    else:
        in_specs = [row]
        args = [x]
    if has_f:
        in_specs += [row, _mod_spec(layer - 1, g_k, tm, t0)]
        args += [f, modt]
    in_specs += [pl.BlockSpec((1, D), lambda i: (0, 0)), _mod_spec(layer, sc_k, tm, t0),
                 _mod_spec(layer, sh_k, tm, t0)]
    args += [norm_g.reshape(1, D), modt, modt]
    h_shape = jax.ShapeDtypeStruct((T, D), BF16)
    if has_f:
        out_specs = [row, row]
        out_shape = [jax.ShapeDtypeStruct((T, D), F32), h_shape]
    else:
        out_specs = [row]
        out_shape = [h_shape]
    if want_t:
        out_specs = out_specs + [pl.BlockSpec((D, tm), lambda i: (0, i + t0))]
        out_shape = out_shape + [jax.ShapeDtypeStruct((D, T), BF16)]
    outs = pl.pallas_call(
        functools.partial(_normmod_kernel, has_f, want_t, ctx_tiles),
        grid=(T // tm - t0,),
        in_specs=in_specs,
        out_specs=out_specs,
        out_shape=out_shape,
        compiler_params=_cparams("arbitrary"),
        name="normmod_resid" if has_f else "normmod",
    )(*args)
    return outs[0] if len(outs) == 1 else tuple(outs)


MM_TM = 512


def _mm_kernel(a_ref, w_ref, o_ref):
    o_ref[...] = jnp.dot(a_ref[...], w_ref[...], preferred_element_type=F32).astype(o_ref.dtype)


def _mm_resid_kernel(ctx_tiles, a_ref, w_ref, *refs):
    if ctx_tiles:
        rc_ref, rx_ref, g_ref, o_ref = refs
        resid = jnp.where(pl.program_id(0) < ctx_tiles, rc_ref[...], rx_ref[...])
    else:
        r_ref, g_ref, o_ref = refs
        resid = r_ref[...]
    acc = jnp.dot(a_ref[...], w_ref[...], preferred_element_type=F32)
    o_ref[...] = resid + g_ref[0, 0] * acc


def _mm_wstat_kernel(a_ref, w_ref, o_ref, wb_ref):
    @pl.when(pl.program_id(1) == 0)
    def _():
        wb_ref[...] = w_ref[0].astype(BF16)

    o_ref[...] = jnp.dot(a_ref[...], wb_ref[...], preferred_element_type=F32).astype(o_ref.dtype)


def _matmul_wstat(a, w, layer, out_dtype, tn):
    m, k = a.shape
    n = w.shape[2]
    return pl.pallas_call(
        _mm_wstat_kernel,
        grid=(n // tn, m // MM_TM),
        in_specs=[pl.BlockSpec((MM_TM, k), lambda j, i: (i, 0)),
                  pl.BlockSpec((1, k, tn), lambda j, i: (layer, 0, j))],
        out_specs=pl.BlockSpec((MM_TM, tn), lambda j, i: (i, j)),
        out_shape=jax.ShapeDtypeStruct((m, n), out_dtype),
        scratch_shapes=[pltpu.VMEM((k, tn), BF16)],
        compiler_params=_cparams("arbitrary", "arbitrary"),
        name="matmul_wstat",
    )(a, w)


def _matmul(a, w, out_dtype, tn, latent_only=False):
    m, k = a.shape
    n = w.shape[1]
    t0 = _first_tile(MM_TM, latent_only)
    return pl.pallas_call(
        _mm_kernel,
        grid=(m // MM_TM - t0, n // tn),
        in_specs=[pl.BlockSpec((MM_TM, k), lambda i, j: (i + t0, 0)),
                  pl.BlockSpec((k, tn), lambda i, j: (0, j))],
        out_specs=pl.BlockSpec((MM_TM, tn), lambda i, j: (i + t0, j)),
        out_shape=jax.ShapeDtypeStruct((m, n), out_dtype),
        compiler_params=_cparams("arbitrary", "arbitrary"),
        name="matmul",
    )(a, w)


def _matmul_resid(a, w, resid, modt, layer, g_k, tn, latent_only=False):
    m, k = a.shape
    n = w.shape[1]
    nj = n // tn
    t0 = _first_tile(MM_TM, latent_only)
    ctx_tiles = 0
    if isinstance(resid, tuple):
        assert t0 == 0
        ctx_tiles = CTX_ROWS // MM_TM
        r_specs = [pl.BlockSpec((MM_TM, tn), lambda i, j: (jnp.minimum(i, ctx_tiles - 1), j)),
                   pl.BlockSpec((MM_TM, tn), lambda i, j: (jnp.maximum(i - ctx_tiles, 0), j))]
        r_args = list(resid)
    else:
        r_specs = [pl.BlockSpec((MM_TM, tn), lambda i, j: (i + t0, j))]
        r_args = [resid]
    return pl.pallas_call(
        functools.partial(_mm_resid_kernel, ctx_tiles),
        grid=(m // MM_TM - t0, nj),
        in_specs=[pl.BlockSpec((MM_TM, k), lambda i, j: (i + t0, 0)),
                  pl.BlockSpec((k, tn), lambda i, j: (0, j))] + r_specs + [
                  pl.BlockSpec((1, 1, 1, tn),
                               lambda i, j: (layer, _group_of_row((i + t0) * MM_TM), 0, g_k * nj + j))],
        out_specs=pl.BlockSpec((MM_TM, tn), lambda i, j: (i + t0, j)),
        out_shape=jax.ShapeDtypeStruct((m, n), F32),
        compiler_params=_cparams("arbitrary", "arbitrary"),
        name="matmul_resid",
    )(a, w, *r_args, modt)


CV_TM = 256
CV_HALO = 16
CV_RC = 64
CV_CC = 128
CV_SUB = 8
CV_SHROWS = CV_TM + 2 * CV_HALO - CV_SUB


def _conv_kernel(t0, prev_ref, cur_ref, next_ref, w_ref, cb_ref, lg_ref, lb_ref, o_ref, buf_ref, acc_ref):
    i = pl.program_id(0) + t0
    row0 = i * CV_TM
    lat = row0 - CTX_ROWS
    seq_start = jnp.where(row0 < CTX_ROWS, row0 % L == 0, lat % S == 0)
    seq_end = jnp.where(row0 < CTX_ROWS, (row0 + CV_TM) % L == 0, (lat + CV_TM) % S == 0)

    def glu(u):
        return u[:, :CONV_CH] * _sigmoid(u[:, CONV_CH:])

    buf_ref[0, 0:CV_HALO, :] = glu(prev_ref[...]) * jnp.where(seq_start, 0.0, 1.0)
    buf_ref[0, CV_HALO:CV_HALO + CV_TM, :] = glu(cur_ref[...])
    buf_ref[0, CV_HALO + CV_TM:, :] = glu(next_ref[...]) * jnp.where(seq_end, 0.0, 1.0)
    for s in range(1, CV_SUB):
        buf_ref[s, 0:CV_SHROWS, :] = buf_ref[0, s:s + CV_SHROWS, :]

    tap0 = CV_HALO - CONV_K // 2

    def col_chunk(c, carry):
        c0 = pl.multiple_of(c * CV_CC, CV_CC)
        for r in range(CV_TM // CV_RC):
            acc = jnp.zeros((CV_RC, CV_CC), F32)
            for k in range(CONV_K):
                q, s = divmod(k + tap0, CV_SUB)
                r0 = r * CV_RC + q * CV_SUB
                acc = acc + w_ref[k:k + 1, pl.ds(c0, CV_CC)] * buf_ref[s, r0:r0 + CV_RC, pl.ds(c0, CV_CC)]
            acc_ref[r * CV_RC:(r + 1) * CV_RC, pl.ds(c0, CV_CC)] = acc
        return carry

    lax.fori_loop(0, CONV_CH // CV_CC, col_chunk, 0)

    h = acc_ref[...] + cb_ref[...]
    mu = jnp.mean(h, axis=-1, keepdims=True)
    xc = h - mu
    var = jnp.mean(xc * xc, axis=-1, keepdims=True)
    y = xc * lax.rsqrt(var + EPS) * lg_ref[...] + lb_ref[...]
    o_ref[...] = (y * _sigmoid(y)).astype(BF16)


def _conv_module(u, conv_w, conv_b, ln_g, ln_b, latent_only=False):
    hb = CV_TM // CV_HALO
    last = T // CV_HALO - 1
    t0 = _first_tile(CV_TM, latent_only)
    vec = pl.BlockSpec((1, CONV_CH), lambda i: (0, 0))
    return pl.pallas_call(
        functools.partial(_conv_kernel, t0),
        grid=(T // CV_TM - t0,),
        in_specs=[
            pl.BlockSpec((CV_HALO, W_CONV_IN), lambda i: (jnp.maximum((i + t0) * hb - 1, 0), 0)),
            pl.BlockSpec((CV_TM, W_CONV_IN), lambda i: (i + t0, 0)),
            pl.BlockSpec((CV_HALO, W_CONV_IN), lambda i: (jnp.minimum((i + t0 + 1) * hb, last), 0)),
            pl.BlockSpec((CONV_K, CONV_CH), lambda i: (0, 0)),
            vec, vec, vec,
        ],
        out_specs=pl.BlockSpec((CV_TM, CONV_CH), lambda i: (i + t0, 0)),
        out_shape=jax.ShapeDtypeStruct((T, CONV_CH), BF16),
        scratch_shapes=[pltpu.VMEM((CV_SUB, CV_TM + 2 * CV_HALO, CONV_CH), F32),
                        pltpu.VMEM((CV_TM, CONV_CH), F32)],
        compiler_params=_cparams("arbitrary"),
        name="conv_module",
    )(u, u, u, conv_w, conv_b.reshape(1, CONV_CH), ln_g.reshape(1, CONV_CH), ln_b.reshape(1, CONV_CH))


HN_TM = 256


def _headnorm_kernel(qw_ref, qn_ref, kvw_ref, kn_ref, vn_ref, cos_ref, sin_ref, g_ref,
                     qw_o, qn_o, kvw_o, kn_o, vn_o):
    cosf = cos_ref[...]
    sinf = sin_ref[...]
    lane = lax.broadcasted_iota(jnp.int32, (HN_TM, HD), 1)
    first_half = (lane % (HD // 2)) < (HD // 4)

    def norm(x, gi):
        ms = jnp.mean(x * x, axis=-1, keepdims=True)
        return x * lax.rsqrt(ms + EPS) * g_ref[gi:gi + 1, :]

    def rope(x):
        partner = jnp.where(first_half, pltpu.roll(x, HD - HD // 4, 1), pltpu.roll(x, HD // 4, 1))
        return x * cosf + partner * sinf

    for hd in range(WIN_HQ):
        sl = slice(hd * HD, (hd + 1) * HD)
        qw_o[:, sl] = rope(norm(qw_ref[:, sl], 0)).astype(BF16)
    for hd in range(WIN_HKV):
        sl = slice(hd * HD, (hd + 1) * HD)
        kvw_o[:, sl] = rope(norm(kvw_ref[:, sl], 1)).astype(BF16)
    kvw_o[:, W_WIN_KV:] = kvw_ref[:, W_WIN_KV:].astype(BF16)
    for hd in range(NA_H):
        sl = slice(hd * HD, (hd + 1) * HD)
        qn_o[:, sl] = norm(qn_ref[:, sl], 2).astype(BF16)
        kn_o[:, sl] = norm(kn_ref[:, sl], 3).astype(BF16)
    vn_o[...] = vn_ref[...].astype(BF16)


def _headnorm(u, cosf, sinf, gains):
    tm = HN_TM
    kvb = KV_OFF // W_NA_KV
    row128 = pl.BlockSpec((tm, HD), lambda i: (i, 0))

    def ospec(w):
        return pl.BlockSpec((tm, w), lambda i: (i, 0))

    def oshape(w):
        return jax.ShapeDtypeStruct((T, w), BF16)

    return pl.pallas_call(
        _headnorm_kernel,
        grid=(T // tm,),
        in_specs=[
            pl.BlockSpec((tm, W_WIN_Q), lambda i: (i, Q_WIN_OFF // W_WIN_Q)),
            pl.BlockSpec((tm, W_NA_Q), lambda i: (i, Q_NA_OFF // W_NA_Q)),
            pl.BlockSpec((tm, 2 * W_WIN_KV), lambda i: (i, kvb)),
            pl.BlockSpec((tm, W_NA_KV), lambda i: (i, kvb + 1)),
            pl.BlockSpec((tm, W_NA_KV), lambda i: (i, kvb + 2)),
            row128, row128,
            pl.BlockSpec((8, HD), lambda i: (0, 0)),
        ],
        out_specs=[ospec(W_WIN_Q), ospec(W_NA_Q), ospec(2 * W_WIN_KV), ospec(W_NA_KV), ospec(W_NA_KV)],
        out_shape=[oshape(W_WIN_Q), oshape(W_NA_Q), oshape(2 * W_WIN_KV), oshape(W_NA_KV), oshape(W_NA_KV)],
        compiler_params=_cparams("arbitrary"),
        name="headnorm_rope",
    )(u, u, u, u, u, cosf, sinf, gains)


def _rope_tables():
    t = jnp.arange(S)
    row = (t // GRID_W).astype(F32)
    col = (t % GRID_W).astype(F32)
    axis_dim = HD // 2
    inv = ROPE_BASE ** (-jnp.arange(0, axis_dim, 2, dtype=F32) / axis_dim)
    ar = row[:, None] * inv
    ac = col[:, None] * inv
    cosl = jnp.concatenate([jnp.cos(ar), jnp.cos(ar), jnp.cos(ac), jnp.cos(ac)], axis=1)
    sinl = jnp.concatenate([-jnp.sin(ar), jnp.sin(ar), -jnp.sin(ac), jnp.sin(ac)], axis=1)
    cosf = jnp.concatenate([jnp.ones((CTX_ROWS, HD), F32), cosl, cosl], axis=0)
    sinf = jnp.concatenate([jnp.zeros((CTX_ROWS, HD), F32), sinl, sinl], axis=0)
    return cosf, sinf


WA_TQ = 128
WA_CTX_TILES = CTX_ROWS // WA_TQ
WA_LAT_TILES = S // WA_TQ


def _win_attn_kernel(t0, sink_ref, q_ref, kp_ref, kc_ref, kn_ref, kx_ref, o_ref):
    i = pl.program_id(0) + t0
    is_ctx = i < WA_CTX_TILES
    n = (i - WA_CTX_TILES) % WA_LAT_TILES
    far = 4 * WA_TQ
    off_prev = jnp.where(jnp.logical_or(is_ctx, n == 0), far, 0)
    off_cur = jnp.where(is_ctx, far, 0)
    off_next = jnp.where(jnp.logical_or(is_ctx, n == WA_LAT_TILES - 1), far, 0)
    rows = WIN_G * WA_TQ
    r = lax.broadcasted_iota(jnp.int32, (rows, WA_TQ), 0) % WA_TQ
    c = lax.broadcasted_iota(jnp.int32, (rows, WA_TQ), 1)
    m_prev = c >= r + off_prev
    m_cur = c >= off_cur
    m_next = c + off_next <= r
    grp = lax.broadcasted_iota(jnp.int32, (rows, 1), 0) // WA_TQ

    for h in range(WIN_HKV):
        q = jnp.concatenate([q_ref[:, (h * WIN_G + g) * HD:(h * WIN_G + g + 1) * HD] for g in range(WIN_G)], axis=0)
        ks = slice(h * HD, (h + 1) * HD)
        vs = slice(W_WIN_KV + h * HD, W_WIN_KV + (h + 1) * HD)
        s_p = jnp.where(m_prev, _dot_nt(q, kp_ref[:, ks]) * ATT_SCALE, NEG)
        s_c = jnp.where(m_cur, _dot_nt(q, kc_ref[:, ks]) * ATT_SCALE, NEG)
        s_n = jnp.where(m_next, _dot_nt(q, kn_ref[:, ks]) * ATT_SCALE, NEG)
        s_x = _dot_nt(q, kx_ref[:, ks]) * ATT_SCALE
        snk = jnp.zeros((rows, 1), F32)
        for g in range(WIN_G):
            snk = jnp.where(grp == g, sink_ref[h * WIN_G + g], snk)
        m = jnp.maximum(jnp.maximum(jnp.max(s_p, axis=-1, keepdims=True), jnp.max(s_c, axis=-1, keepdims=True)),
                        jnp.maximum(jnp.max(s_n, axis=-1, keepdims=True), jnp.max(s_x, axis=-1, keepdims=True)))
        m = jnp.maximum(m, snk)
        p_p = jnp.exp(s_p - m)
        p_c = jnp.exp(s_c - m)
        p_n = jnp.exp(s_n - m)
        p_x = jnp.exp(s_x - m)
        den = (jnp.sum(p_p, axis=-1, keepdims=True) + jnp.sum(p_c, axis=-1, keepdims=True)
               + jnp.sum(p_n, axis=-1, keepdims=True) + jnp.sum(p_x, axis=-1, keepdims=True) + jnp.exp(snk - m))
        o = (jnp.dot(p_p.astype(BF16), kp_ref[:, vs], preferred_element_type=F32)
             + jnp.dot(p_c.astype(BF16), kc_ref[:, vs], preferred_element_type=F32)
             + jnp.dot(p_n.astype(BF16), kn_ref[:, vs], preferred_element_type=F32)
             + jnp.dot(p_x.astype(BF16), kx_ref[:, vs], preferred_element_type=F32))
        o = o / den
        for g in range(WIN_G):
            o_ref[:, (h * WIN_G + g) * HD:(h * WIN_G + g + 1) * HD] = o[g * WA_TQ:(g + 1) * WA_TQ].astype(BF16)


def _win_attention(qw, kvw, sink, latent_only=False):
    nt = T // WA_TQ
    t0 = _first_tile(WA_TQ, latent_only)

    def bounds(i):
        is_ctx = i < WA_CTX_TILES
        b = (i - WA_CTX_TILES) // WA_LAT_TILES
        lo = jnp.where(is_ctx, 0, WA_CTX_TILES + b * WA_LAT_TILES)
        hi = jnp.where(is_ctx, nt - 1, WA_CTX_TILES + (b + 1) * WA_LAT_TILES - 1)
        return lo, hi

    def prev_map(g):
        i = g + t0
        lo, _ = bounds(i)
        return (jnp.maximum(i - 1, lo), 0)

    def next_map(g):
        i = g + t0
        _, hi = bounds(i)
        return (jnp.minimum(i + 1, hi), 0)

    def ctx_map(g):
        i = g + t0
        b = jnp.where(i < WA_CTX_TILES, i // (L // WA_TQ), (i - WA_CTX_TILES) // WA_LAT_TILES)
        return (b, 0)

    kvw_w = 2 * W_WIN_KV
    return pl.pallas_call(
        functools.partial(_win_attn_kernel, t0),
        grid=(nt - t0,),
        in_specs=[
            pl.BlockSpec(memory_space=pltpu.SMEM),
            pl.BlockSpec((WA_TQ, W_WIN_Q), lambda g: (g + t0, 0)),
            pl.BlockSpec((WA_TQ, kvw_w), prev_map),
            pl.BlockSpec((WA_TQ, kvw_w), lambda g: (g + t0, 0)),
            pl.BlockSpec((WA_TQ, kvw_w), next_map),
            pl.BlockSpec((L, kvw_w), ctx_map),
        ],
        out_specs=pl.BlockSpec((WA_TQ, W_WIN_Q), lambda g: (g + t0, 0)),
        out_shape=jax.ShapeDtypeStruct((T, W_WIN_Q), BF16),
        compiler_params=_cparams("arbitrary"),
        name="window_attention",
    )(sink, qw, kvw, kvw, kvw, kvw)


NA_TQ = 256
NA_QROWS = NA_TQ // GRID_W
NA_KROWS = 3 * NA_QROWS
NA_NKEY = NA_KROWS * GRID_W
NA_TYPES = 4
RPB_R = 2 * NA_KH - 1
RPB_C = 2 * NA_KW - 1


def _na_row_valid(ty, a, j):
    if ty == 0:
        return NA_QROWS <= j < NA_QROWS + NA_KH
    if ty == 1:
        return a <= j < a + NA_KH
    if ty == 2:
        return j < NA_KH
    return False


def _rpb_kernel(rpb_ref, o_ref):
    h = pl.program_id(0)
    shp = (GRID_W, 2 * GRID_W)
    qc = lax.broadcasted_iota(jnp.int32, shp, 0)
    lane = lax.broadcasted_iota(jnp.int32, shp, 1)
    kc = lane % GRID_W
    second = lane >= GRID_W
    dcol = kc - qc + (NA_KW - 1)
    cs = jnp.clip(qc - NA_KW // 2, 0, GRID_W - NA_KW)
    colmask = jnp.logical_and(kc >= cs, kc < cs + NA_KW)
    neg = jnp.full(shp, NEG, F32)
    base = h * (RPB_R * RPB_C)
    pair = []
    for dr in range(RPB_R - 1):
        acc = jnp.zeros(shp, F32)
        for dd in range(RPB_C):
            v0 = rpb_ref[base + dr * RPB_C + dd]
            v1 = rpb_ref[base + (dr + 1) * RPB_C + dd]
            acc = jnp.where(dcol == dd, jnp.where(second, v1, v0), acc)
        pair.append(jnp.where(colmask, acc, neg))
    for ty in range(NA_TYPES):
        for a in range(NA_QROWS):
            for jp in range(NA_KROWS // 2):
                j = 2 * jp
                ok0 = _na_row_valid(ty, a, j)
                ok1 = _na_row_valid(ty, a, j + 1)
                dr = j - a + NA_QROWS - 1
                if ok0 and ok1:
                    tile = pair[dr]
                elif ok0:
                    tile = jnp.where(second, neg, pair[dr])
                elif ok1:
                    tile = jnp.where(second, pair[dr], neg)
                else:
                    tile = neg
                o_ref[ty, 0, a * GRID_W:(a + 1) * GRID_W, jp * 2 * GRID_W:(jp + 1) * 2 * GRID_W] = tile


def _rpb_tiles(rpb):
    return pl.pallas_call(
        _rpb_kernel,
        grid=(NA_H,),
        in_specs=[pl.BlockSpec(memory_space=pltpu.SMEM)],
        out_specs=pl.BlockSpec((NA_TYPES, 1, NA_TQ, NA_NKEY), lambda h: (0, h, 0, 0)),
        out_shape=jax.ShapeDtypeStruct((NA_TYPES, NA_H, NA_TQ, NA_NKEY), F32),
        compiler_params=_cparams("arbitrary"),
        name="rpb_tiles",
    )(rpb.reshape(-1))


NA_CTX_TILES = CTX_ROWS // NA_TQ
NA_LAT_TILES = S // NA_TQ


def _na_attn_kernel(q_ref, kp_ref, kc_ref, kn_ref, vp_ref, vc_ref, vn_ref, kx_ref, vx_ref, bias_ref, o_ref):
    for h in range(NA_H):
        hs = slice(h * HD, (h + 1) * HD)
        q = q_ref[:, hs]
        s_p = _dot_nt(q, kp_ref[:, hs]) * ATT_SCALE + bias_ref[0, h, :, 0:NA_TQ]
        s_c = _dot_nt(q, kc_ref[:, hs]) * ATT_SCALE + bias_ref[0, h, :, NA_TQ:2 * NA_TQ]
        s_n = _dot_nt(q, kn_ref[:, hs]) * ATT_SCALE + bias_ref[0, h, :, 2 * NA_TQ:3 * NA_TQ]
        s_x = _dot_nt(q, kx_ref[:, hs]) * ATT_SCALE
        m = jnp.maximum(jnp.maximum(jnp.max(s_p, axis=-1, keepdims=True), jnp.max(s_c, axis=-1, keepdims=True)),
                        jnp.maximum(jnp.max(s_n, axis=-1, keepdims=True), jnp.max(s_x, axis=-1, keepdims=True)))
        p_p = jnp.exp(s_p - m)
        p_c = jnp.exp(s_c - m)
        p_n = jnp.exp(s_n - m)
        p_x = jnp.exp(s_x - m)
        den = (jnp.sum(p_p, axis=-1, keepdims=True) + jnp.sum(p_c, axis=-1, keepdims=True)
               + jnp.sum(p_n, axis=-1, keepdims=True) + jnp.sum(p_x, axis=-1, keepdims=True))
        o = (jnp.dot(p_p.astype(BF16), vp_ref[:, hs], preferred_element_type=F32)
             + jnp.dot(p_c.astype(BF16), vc_ref[:, hs], preferred_element_type=F32)
             + jnp.dot(p_n.astype(BF16), vn_ref[:, hs], preferred_element_type=F32)
             + jnp.dot(p_x.astype(BF16), vx_ref[:, hs], preferred_element_type=F32))
        o_ref[:, hs] = (o / den).astype(BF16)


def _na_attention(qn, kn, vn, bias, latent_only=False):
    nt = T // NA_TQ
    t0 = _first_tile(NA_TQ, latent_only)

    def bounds(i):
        is_ctx = i < NA_CTX_TILES
        b = (i - NA_CTX_TILES) // NA_LAT_TILES
        lo = jnp.where(is_ctx, 0, NA_CTX_TILES + b * NA_LAT_TILES)
        hi = jnp.where(is_ctx, nt - 1, NA_CTX_TILES + (b + 1) * NA_LAT_TILES - 1)
        return lo, hi

    def prev_map(g):
        i = g + t0
        lo, _ = bounds(i)
        return (jnp.maximum(i - 1, lo), 0)

    def next_map(g):
        i = g + t0
        _, hi = bounds(i)
        return (jnp.minimum(i + 1, hi), 0)

    def ctx_map(g):
        i = g + t0
        return (jnp.where(i < NA_CTX_TILES, i, (i - NA_CTX_TILES) // NA_LAT_TILES), 0)

    def bias_map(g):
        i = g + t0
        n = (i - NA_CTX_TILES) % NA_LAT_TILES
        ty = jnp.where(i < NA_CTX_TILES, 3, jnp.where(n == 0, 0, jnp.where(n == NA_LAT_TILES - 1, 2, 1)))
        return (ty, 0, 0, 0)

    w = W_NA_KV
    cur = pl.BlockSpec((NA_TQ, w), lambda g: (g + t0, 0))
    prv = pl.BlockSpec((NA_TQ, w), prev_map)
    nxt = pl.BlockSpec((NA_TQ, w), next_map)
    ctx = pl.BlockSpec((L, w), ctx_map)
    return pl.pallas_call(
        _na_attn_kernel,
        grid=(nt - t0,),
        in_specs=[cur, prv, cur, nxt, prv, cur, nxt, ctx, ctx,
                  pl.BlockSpec((1, NA_H, NA_TQ, NA_NKEY), bias_map)],
        out_specs=cur,
        out_shape=jax.ShapeDtypeStruct((T, W_NA_Q), BF16),
        compiler_params=_cparams("arbitrary"),
        name="neighborhood_attention",
    )(qn, kn, kn, kn, vn, vn, vn, kn, vn, bias)


MG_TM = 512
MG_TN = 1024


def _merge_kernel(ca_ref, aw_ref, an_ref, wc_ref, ww_ref, wn_ref, ga_ref, gb_ref, gc_ref, o_ref):
    ya = jnp.dot(ca_ref[...], wc_ref[...], preferred_element_type=F32)
    yb = jnp.dot(aw_ref[...], ww_ref[...], preferred_element_type=F32)
    yc = jnp.dot(an_ref[...], wn_ref[...], preferred_element_type=F32)
    o = _sigmoid(ga_ref[...]) * ya + _sigmoid(gb_ref[...]) * yb + _sigmoid(gc_ref[...]) * yc
    o_ref[...] = o.astype(BF16)


def _merge(hconv, aw, an, w_conv_out, w_win_out, w_na_out, u, latent_only=False):
    gb0 = GATE_OFF // MG_TN
    gstep = D // MG_TN
    t0 = _first_tile(MG_TM, latent_only)

    def a_spec(k):
        return pl.BlockSpec((MG_TM, k), lambda i, j: (i + t0, 0))

    def w_spec(k):
        return pl.BlockSpec((k, MG_TN), lambda i, j: (0, j))

    def g_spec(which):
        return pl.BlockSpec((MG_TM, MG_TN), lambda i, j: (i + t0, gb0 + which * gstep + j))

    return pl.pallas_call(
        _merge_kernel,
        grid=(T // MG_TM - t0, D // MG_TN),
        in_specs=[a_spec(CONV_CH), a_spec(W_WIN_Q), a_spec(W_NA_Q),
                  w_spec(CONV_CH), w_spec(W_WIN_Q), w_spec(W_NA_Q),
                  g_spec(0), g_spec(1), g_spec(2)],
        out_specs=pl.BlockSpec((MG_TM, MG_TN), lambda i, j: (i + t0, j)),
        out_shape=jax.ShapeDtypeStruct((T, D), BF16),
        compiler_params=_cparams("arbitrary", "arbitrary"),
        name="gated_merge",
    )(hconv, aw, an, w_conv_out, w_win_out, w_na_out, u, u, u)


TK_TT = 256


def _topk_rounds(scores, n_rounds):
    nrow = scores.shape[0]
    idx = lax.broadcasted_iota(jnp.int32, scores.shape, 0).astype(F32)
    work = scores
    rank = jnp.full(scores.shape, float(n_rounds), F32)
    vals = []
    for a in range(n_rounds):
        m = jnp.max(work, axis=0, keepdims=True)
        first = jnp.min(jnp.where(work == m, idx, float(nrow)), axis=0, keepdims=True)
        sel = idx == first
        rank = jnp.where(sel, float(a), rank)
        work = jnp.where(sel, -jnp.inf, work)
        vals.append(m)
    return jnp.concatenate(vals, axis=0), rank


def _topk_rounds_no_ties(scores, n_rounds):
    work = scores
    rank = jnp.full(scores.shape, float(n_rounds), F32)
    vals = []
    for a in range(n_rounds):
        m = jnp.max(work, axis=0, keepdims=True)
        sel = work == m
        rank = jnp.where(sel, float(a), rank)
        work = jnp.where(sel, -jnp.inf, work)
        vals.append(m)
    taken = jnp.sum(jnp.where(rank < float(n_rounds), 1.0, 0.0), axis=0, keepdims=True)
    return jnp.concatenate(vals, axis=0), rank, taken


def _peer_topk_kernel(q_ref, k1_ref, k2_ref, n_ref, e1_ref, r2_ref, e2_ref):
    kk = PEER_TOPK

    def head(h, carry):
        c1 = pl.multiple_of(h * 2 * HD, 2 * HD)
        q1 = q_ref[:, pl.ds(c1, HD)]
        q2 = q_ref[:, pl.ds(c1 + HD, HD)]
        s1 = _dot_nt(k1_ref[h].astype(BF16), q1)
        s2 = _dot_nt(k2_ref[h].astype(BF16), q2)
        f1, fr1, t1 = _topk_rounds_no_ties(s1, kk)
        f2, fr2, t2 = _topk_rounds_no_ties(s2, kk)
        tied = jnp.max(jnp.abs(t1 - float(kk)) + jnp.abs(t2 - float(kk))) > 0.0

        def exact(_):
            return _topk_rounds(s1, kk) + _topk_rounds(s2, kk)

        def keep(_):
            return f1, fr1, f2, fr2

        v1all, rank1, v2all, rank2 = lax.cond(tied, exact, keep, None)
        v1 = [v1all[0:1, :]]
        v2 = [v2all[0:1, :]]
        arow = lax.broadcasted_iota(jnp.int32, v2all.shape, 0).astype(F32)
        cnt = jnp.zeros(v2all.shape, F32)
        front = v1all + v2[0]
        top = v1[0] + v2[0]
        z = jnp.zeros_like(top)
        for _ in range(kk):
            m = jnp.max(front, axis=0, keepdims=True)
            first = jnp.min(jnp.where(front == m, arow, float(kk)), axis=0, keepdims=True)
            sel = arow == first
            cnt = cnt + jnp.where(sel, 1.0, 0.0)
            z = z + jnp.exp(m - top)
            taken = jnp.max(jnp.where(sel, cnt, -1.0), axis=0, keepdims=True)
            nxt = jnp.max(jnp.where(arow == taken, v2all, -jnp.inf), axis=0, keepdims=True)
            front = jnp.where(sel, v1all + nxt, front)
        nfull = jnp.zeros(s1.shape, F32)
        for a in range(kk):
            nfull = jnp.where(rank1 == float(a), cnt[a:a + 1, :], nfull)
        n_ref[h] = nfull
        e1_ref[h] = jnp.exp(s1 - v1[0]) / z
        r2_ref[h] = rank2.astype(BF16)
        e2_ref[h] = jnp.exp(s2 - v2[0]).astype(BF16)
        return carry

    lax.fori_loop(0, PEER_HEADS, head, 0)


def _peer_topk(q, k1, k2, latent_only=False):
    tt = TK_TT
    t0 = _first_tile(tt, latent_only)
    kspec = pl.BlockSpec((PEER_HEADS, N_KEYS, HD), lambda i: (0, 0, 0))
    ospec = pl.BlockSpec((PEER_HEADS, N_KEYS, tt), lambda i: (0, 0, i + t0))
    oshape = jax.ShapeDtypeStruct((PEER_HEADS, N_KEYS, T), F32)
    oshape_b = jax.ShapeDtypeStruct((PEER_HEADS, N_KEYS, T), BF16)
    return pl.pallas_call(
        _peer_topk_kernel,
        grid=(T // tt - t0,),
        in_specs=[pl.BlockSpec((tt, 2 * HD * PEER_HEADS), lambda i: (i + t0, 0)), kspec, kspec],
        out_specs=[ospec] * 4,
        out_shape=[oshape, oshape, oshape_b, oshape_b],
        compiler_params=_cparams("arbitrary"),
        name="peer_topk",
    )(q, k1, k2)


EX_TM = 512
EX_TN = 512
EX_SUB = 256
EX_NC = 1024
EX_LC = 128
EX_RC = 16
EX_I1 = EX_TN // N_KEYS
EX_STEPS_PER_GROUP = 8 // EX_I1
assert EX_I1 * EX_STEPS_PER_GROUP == 8 and EX_STEPS_PER_GROUP == 2


def _gelu_tanh(x):
    c = math.sqrt(2.0 / math.pi)
    t = jnp.tanh(x * (c + (c * 0.044715) * (x * x)))
    hx = 0.5 * x
    return hx + hx * t


def _experts_kernel(h_ref, u_ref, v_ref, n_ref, e1_ref, r2_ref, e2_ref, o_ref):
    j = pl.program_id(1)

    @pl.when(j == 0)
    def _():
        o_ref[...] = jnp.zeros_like(o_ref)

    grp = pl.multiple_of((j // EX_STEPS_PER_GROUP) * 8, 8)
    upper = (j % EX_STEPS_PER_GROUP) == 1
    key_rows = {}
    for hh in range(PEER_HEADS):
        for c in range(EX_TM // EX_LC):
            cs = slice(c * EX_LC, (c + 1) * EX_LC)
            n8 = n_ref[hh, pl.ds(grp, 8), cs]
            e8 = e1_ref[hh, pl.ds(grp, 8), cs]
            for k in range(EX_I1):
                nrow = jnp.where(upper, n8[EX_I1 + k:EX_I1 + k + 1], n8[k:k + 1])
                e1row = jnp.where(upper, e8[EX_I1 + k:EX_I1 + k + 1], e8[k:k + 1])
                key_rows[(k, hh, c)] = (jnp.broadcast_to(nrow, (EX_RC, EX_LC)).astype(BF16),
                                        jnp.broadcast_to(e1row, (EX_RC, EX_LC)).astype(BF16))

    def gate_tile(k, rows, c):
        cs = slice(c * EX_LC, (c + 1) * EX_LC)
        gate = jnp.zeros((EX_RC, EX_LC), BF16)
        for hh in range(PEER_HEADS):
            nrow, e1row = key_rows[(k, hh, c)]
            gate = gate + jnp.where(r2_ref[hh, rows, cs] < nrow, e2_ref[hh, rows, cs], 0.0) * e1row
        return gate

    w_parts = []
    for s in range(EX_TN // EX_SUB):
        es = slice(s * EX_SUB, (s + 1) * EX_SUB)
        act = _gelu_tanh(jnp.dot(u_ref[es, :], h_ref[...], preferred_element_type=F32)).astype(BF16)
        for ii in range(EX_SUB // N_KEYS):
            k = s * (EX_SUB // N_KEYS) + ii
            for rc in range(N_KEYS // EX_RC):
                rows = slice(rc * EX_RC, (rc + 1) * EX_RC)
                arows = slice(ii * N_KEYS + rc * EX_RC, ii * N_KEYS + (rc + 1) * EX_RC)
                row = []
                for c in range(EX_TM // EX_LC):
                    cs = slice(c * EX_LC, (c + 1) * EX_LC)
                    row.append(gate_tile(k, rows, c) * act[arows, cs])
                w_parts.append(jnp.concatenate(row, axis=1))
    w_t = jnp.concatenate(w_parts, axis=0)
    for nc in range(D // EX_NC):
        ns = slice(nc * EX_NC, (nc + 1) * EX_NC)
        o_ref[:, ns] += lax.dot_general(w_t, v_ref[:, ns], (((0,), (0,)), ((), ())),
                                        preferred_element_type=F32)


def _experts(h_t, u, v, tables, latent_only=False):
    n, e1, r2, e2 = tables
    t0 = _first_tile(EX_TM, latent_only)
    tspec = pl.BlockSpec((PEER_HEADS, N_KEYS, EX_TM), lambda i, j: (0, 0, i + t0))
    return pl.pallas_call(
        _experts_kernel,
        grid=(T // EX_TM - t0, N_EXPERTS // EX_TN),
        in_specs=[pl.BlockSpec((D, EX_TM), lambda i, j: (0, i + t0)),
                  pl.BlockSpec((EX_TN, D), lambda i, j: (j, 0)),
                  pl.BlockSpec((EX_TN, D), lambda i, j: (j, 0)),
                  tspec, tspec, tspec, tspec],
        out_specs=pl.BlockSpec((EX_TM, D), lambda i, j: (i + t0, 0), pipeline_mode=pl.Buffered(1)),
        out_shape=jax.ShapeDtypeStruct((T, D), F32),
        compiler_params=_cparams("arbitrary", "arbitrary"),
        name="peer_experts",
    )(h_t, u, v, n, e1, r2, e2)


FR_TM = 256


def _final_kernel(x_ref, f_ref, g_ref, o_ref):
    o_ref[...] = x_ref[...] + g_ref[0, 0] * f_ref[...]


def _final_residual(x, f, modt, layer, g_k):
    off = CTX_ROWS // FR_TM
    lat = pl.BlockSpec((FR_TM, D), lambda i: (i + off, 0))
    return pl.pallas_call(
        _final_kernel,
        grid=(B * S // FR_TM,),
        in_specs=[lat, lat,
                  pl.BlockSpec((1, 1, 1, D), lambda i: (layer, _group_of_row((i + off) * FR_TM), 0, g_k))],
        out_specs=pl.BlockSpec((FR_TM, D), lambda i: (i, 0)),
        out_shape=jax.ShapeDtypeStruct((B * S, D), F32),
        compiler_params=_cparams("arbitrary"),
        name="final_residual",
    )(x, f, modt)


SH1, SC1, G1, SH2, SC2, G2 = range(6)


def kernel(x, c, ctx, c_ctx, w_ada, b_ada, norm1_g, norm2_g, w_in, conv_w, conv_b, conv_ln_g, conv_ln_b,
           w_conv_out, win_qn_g, win_kn_g, win_sink, w_win_out, na_qn_g, na_kn_g, na_rpb, w_na_out, w_out,
           peer_wq, peer_k1, peer_k2, peer_u, peer_v):
    xs = (ctx.reshape(CTX_ROWS, D), x.reshape(B * S, D))
    cvec = jnp.concatenate([c_ctx[None], c, jnp.zeros((N_GROUPS - 1 - B, D), F32)], axis=0)
    modt = _mods(cvec, w_ada, b_ada).reshape(DEPTH, N_GROUPS, 1, 6 * D)
    cosf, sinf = _rope_tables()

    f = None
    for l in range(DEPTH):
        if l == 0:
            h1 = _normmod(xs, modt, l, norm1_g[l], SC1, SH1)
        else:
            xs, h1 = _normmod(xs, modt, l, norm1_g[l], SC1, SH1, f=f, g_k=G2)
        lat = l == DEPTH - 1
        u = _matmul_wstat(h1, w_in, l, F32, 1024)
        hconv = _conv_module(u, conv_w[l], conv_b[l], conv_ln_g[l], conv_ln_b[l], latent_only=lat)
        gains = jnp.concatenate([win_qn_g[l][None], win_kn_g[l][None], na_qn_g[l][None], na_kn_g[l][None],
                                 jnp.zeros((4, HD), F32)], axis=0)
        qw, qn, kvw, kn, vn = _headnorm(u, cosf, sinf, gains)
        aw = _win_attention(qw, kvw, win_sink[l], latent_only=lat)
        an = _na_attention(qn, kn, vn, _rpb_tiles(na_rpb[l]), latent_only=lat)
        merged = _merge(hconv, aw, an, _cast_bf16(w_conv_out, l), _cast_bf16(w_win_out, l),
                        _cast_bf16(w_na_out, l), u, latent_only=lat)
        xs = _matmul_resid(merged, _cast_bf16(w_out, l), xs, modt, l, G1, 1024, latent_only=lat)
        h2, h2_t = _normmod(xs, modt, l, norm2_g[l], SC2, SH2, want_t=True, latent_only=lat)
        q = _matmul(h2, _cast_bf16(peer_wq, l), BF16, 1024, latent_only=lat)
        tables = _peer_topk(q, peer_k1[l], peer_k2[l], latent_only=lat)
        f = _experts(h2_t, _cast_bf16(peer_u, l), _cast_bf16(peer_v, l), tables, latent_only=lat)
    out = _final_residual(xs, f, modt, DEPTH - 1, G2)
    return out.reshape(B, S, D)
```

```python
import functools
import math

import jax
import jax.numpy as jnp
import numpy as np
from jax import lax
from jax.experimental import pallas as pl
from jax.experimental.pallas import tpu as pltpu

F32 = jnp.float32
BF16 = jnp.bfloat16

D = 4096
B = 2
S = 4096
L = 256
DEPTH = 2
GRID_W = 64
HD = 128
EPS = 1e-6
NEG = -1e30
ROPE_BASE = 10000.0

CONV_CH = D // 4
CONV_K = 31
WIN_HQ = 16
WIN_HKV = 4
WIN_G = WIN_HQ // WIN_HKV
NA_H = 8
NA_KH = 8
NA_KW = 16
PEER_HEADS = 8
N_KEYS = 128
N_EXPERTS = N_KEYS * N_KEYS
PEER_TOPK = 16

W_CONV_IN = 2 * CONV_CH
W_WIN_Q = WIN_HQ * HD
W_NA_Q = NA_H * HD
W_GATE = 3 * D
W_WIN_KV = WIN_HKV * HD
W_NA_KV = NA_H * HD
Q_WIN_OFF = W_CONV_IN
Q_NA_OFF = Q_WIN_OFF + W_WIN_Q
GATE_OFF = Q_NA_OFF + W_NA_Q
KV_OFF = GATE_OFF + W_GATE
IN_COLS = KV_OFF + 2 * W_WIN_KV + 2 * W_NA_KV

SUBLANES = 8

CTX_ROWS = B * L
T = CTX_ROWS + B * S
N_GROUPS = SUBLANES
ATT_SCALE = HD ** -0.5

VMEM_LIMIT = 56 * 1024 * 1024


def _cparams(*sem):
    return pltpu.CompilerParams(dimension_semantics=sem, vmem_limit_bytes=VMEM_LIMIT)


def _group_of_row(row0):
    return jnp.where(row0 < CTX_ROWS, 0, 1 + (row0 - CTX_ROWS) // S)


def _sigmoid(z):
    return 0.5 * jnp.tanh(0.5 * z) + 0.5


def _dot_nt(a, b):
    return lax.dot_general(a, b, (((1,), (1,)), ((), ())), preferred_element_type=F32)


def _first_tile(tile, latent_only):
    return CTX_ROWS // tile if latent_only else 0


CAST_BLOCK_BYTES = 8 * 1024 * 1024


def _cast_kernel(w_ref, o_ref):
    o_ref[...] = w_ref[0].astype(BF16)


def _cast_bf16(w, layer):
    _, m, n = w.shape
    rb = m
    while rb * n * 4 > CAST_BLOCK_BYTES:
        rb //= 2
    return pl.pallas_call(
        _cast_kernel,
        grid=(m // rb,),
        in_specs=[pl.BlockSpec((1, rb, n), lambda i: (layer, i, 0))],
        out_specs=pl.BlockSpec((rb, n), lambda i: (i, 0)),
        out_shape=jax.ShapeDtypeStruct((m, n), BF16),
        compiler_params=_cparams("arbitrary"),
        name="cast_bf16",
    )(w)


MOD_TN = 512


def _mods_kernel(c_ref, w_ref, b_ref, o_ref):
    cv = c_ref[...]
    a = (cv * _sigmoid(cv)).astype(BF16)
    w = w_ref[0].astype(BF16)
    o_ref[0] = jnp.dot(a, w, preferred_element_type=F32) + b_ref[0]


def _mods(cvec, w_ada, b_ada):
    n = 6 * D
    return pl.pallas_call(
        _mods_kernel,
        grid=(DEPTH, n // MOD_TN),
        in_specs=[
            pl.BlockSpec((N_GROUPS, D), lambda l, j: (0, 0)),
            pl.BlockSpec((1, D, MOD_TN), lambda l, j: (l, 0, j)),
            pl.BlockSpec((1, 1, MOD_TN), lambda l, j: (l, 0, j)),
        ],
        out_specs=pl.BlockSpec((1, N_GROUPS, MOD_TN), lambda l, j: (l, 0, j)),
        out_shape=jax.ShapeDtypeStruct((DEPTH, N_GROUPS, n), F32),
        compiler_params=_cparams("arbitrary", "arbitrary"),
        name="adaln_mods",
    )(cvec, w_ada, b_ada.reshape(DEPTH, 1, n))


NM_TM = 256


def _normmod_kernel(has_f, want_t, ctx_tiles, *refs):
    refs = list(refs)
    ht_ref = refs.pop() if want_t else None
    if ctx_tiles:
        c_ref, x_ref, ng_ref, sc_ref, sh_ref, h_ref = refs
        x = jnp.where(pl.program_id(0) < ctx_tiles, c_ref[...], x_ref[...])
    elif has_f:
        x_ref, f_ref, g_ref, ng_ref, sc_ref, sh_ref, xo_ref, h_ref = refs
        x = x_ref[...] + g_ref[0, 0] * f_ref[...]
        xo_ref[...] = x
    else:
        x_ref, ng_ref, sc_ref, sh_ref, h_ref = refs
        x = x_ref[...]
    ms = jnp.mean(x * x, axis=-1, keepdims=True)
    y = x * lax.rsqrt(ms + EPS) * ng_ref[...]
    h = y * (1.0 + sc_ref[0, 0]) + sh_ref[0, 0]
    h_ref[...] = h.astype(BF16)
    if want_t:
        ht_ref[...] = h.T.astype(BF16)


def _mod_spec(layer, k, tm, t0=0):
    return pl.BlockSpec((1, 1, 1, D), lambda i: (layer, _group_of_row((i + t0) * tm), 0, k))


def _normmod(x, modt, layer, norm_g, sc_k, sh_k, f=None, g_k=None, want_t=False, latent_only=False):
    tm = NM_TM
    t0 = _first_tile(tm, latent_only)
    row = pl.BlockSpec((tm, D), lambda i: (i + t0, 0))
    has_f = f is not None
    ctx_tiles = 0
    if isinstance(x, tuple):
        ctx_tiles = CTX_ROWS // tm
        in_specs = [pl.BlockSpec((tm, D), lambda i: (jnp.minimum(i, ctx_tiles - 1), 0)),
                    pl.BlockSpec((tm, D), lambda i: (jnp.maximum(i - ctx_tiles, 0), 0))]
        args = list(x)
    else:
        in_specs = [row]
        args = [x]
    if has_f:
        in_specs += [row, _mod_spec(layer - 1, g_k, tm, t0)]
        args += [f, modt]
    in_specs += [pl.BlockSpec((1, D), lambda i: (0, 0)), _mod_spec(layer, sc_k, tm, t0),
                 _mod_spec(layer, sh_k, tm, t0)]
    args += [norm_g.reshape(1, D), modt, modt]
    h_shape = jax.ShapeDtypeStruct((T, D), BF16)
    if has_f:
        out_specs = [row, row]
        out_shape = [jax.ShapeDtypeStruct((T, D), F32), h_shape]
    else:
        out_specs = [row]
        out_shape = [h_shape]
    if want_t:
        out_specs = out_specs + [pl.BlockSpec((D, tm), lambda i: (0, i + t0))]
        out_shape = out_shape + [jax.ShapeDtypeStruct((D, T), BF16)]
    outs = pl.pallas_call(
        functools.partial(_normmod_kernel, has_f, want_t, ctx_tiles),
        grid=(T // tm - t0,),
        in_specs=in_specs,
        out_specs=out_specs,
        out_shape=out_shape,
        compiler_params=_cparams("arbitrary"),
        name="normmod_resid" if has_f else "normmod",
    )(*args)
    return outs[0] if len(outs) == 1 else tuple(outs)


MM_TM = 512


def _mm_kernel(a_ref, w_ref, o_ref):
    o_ref[...] = jnp.dot(a_ref[...], w_ref[...], preferred_element_type=F32).astype(o_ref.dtype)


def _mm_resid_kernel(ctx_tiles, a_ref, w_ref, *refs):
    acc = jnp.dot(a_ref[...], w_ref[...], preferred_element_type=F32)
    if ctx_tiles:
        rc_ref, rx_ref, g_ref, o_ref = refs
        is_ctx = pl.program_id(0) < ctx_tiles

        @pl.when(is_ctx)
        def _():
            o_ref[...] = rc_ref[...] + g_ref[0, 0] * acc

        @pl.when(jnp.logical_not(is_ctx))
        def _():
            o_ref[...] = rx_ref[...] + g_ref[0, 0] * acc
    else:
        r_ref, g_ref, o_ref = refs
        o_ref[...] = r_ref[...] + g_ref[0, 0] * acc


def _mm_wstat_kernel(a_ref, w_ref, o_ref, wb_ref):
    @pl.when(pl.program_id(1) == 0)
    def _():
        wb_ref[...] = w_ref[0].astype(BF16)

    o_ref[...] = jnp.dot(a_ref[...], wb_ref[...], preferred_element_type=F32).astype(o_ref.dtype)


def _matmul_wstat(a, w, layer, out_dtype, tn):
    m, k = a.shape
    n = w.shape[2]
    return pl.pallas_call(
        _mm_wstat_kernel,
        grid=(n // tn, m // MM_TM),
        in_specs=[pl.BlockSpec((MM_TM, k), lambda j, i: (i, 0)),
                  pl.BlockSpec((1, k, tn), lambda j, i: (layer, 0, j))],
        out_specs=pl.BlockSpec((MM_TM, tn), lambda j, i: (i, j)),
        out_shape=jax.ShapeDtypeStruct((m, n), out_dtype),
        scratch_shapes=[pltpu.VMEM((k, tn), BF16)],
        compiler_params=_cparams("arbitrary", "arbitrary"),
        name="matmul_wstat",
    )(a, w)


def _matmul(a, w, out_dtype, tn, latent_only=False):
    m, k = a.shape
    n = w.shape[1]
    t0 = _first_tile(MM_TM, latent_only)
    return pl.pallas_call(
        _mm_kernel,
        grid=(m // MM_TM - t0, n // tn),
        in_specs=[pl.BlockSpec((MM_TM, k), lambda i, j: (i + t0, 0)),
                  pl.BlockSpec((k, tn), lambda i, j: (0, j))],
        out_specs=pl.BlockSpec((MM_TM, tn), lambda i, j: (i + t0, j)),
        out_shape=jax.ShapeDtypeStruct((m, n), out_dtype),
        compiler_params=_cparams("arbitrary", "arbitrary"),
        name="matmul",
    )(a, w)


def _matmul_resid(a, w, resid, modt, layer, g_k, tn, latent_only=False):
    m, k = a.shape
    n = w.shape[1]
    nj = n // tn
    t0 = _first_tile(MM_TM, latent_only)
    ctx_tiles = 0
    if isinstance(resid, tuple):
        assert t0 == 0
        ctx_tiles = CTX_ROWS // MM_TM
        r_specs = [pl.BlockSpec((MM_TM, tn), lambda i, j: (jnp.minimum(i, ctx_tiles - 1), j)),
                   pl.BlockSpec((MM_TM, tn), lambda i, j: (jnp.maximum(i - ctx_tiles, 0), j))]
        r_args = list(resid)
    else:
        r_specs = [pl.BlockSpec((MM_TM, tn), lambda i, j: (i + t0, j))]
        r_args = [resid]
    return pl.pallas_call(
        functools.partial(_mm_resid_kernel, ctx_tiles),
        grid=(m // MM_TM - t0, nj),
        in_specs=[pl.BlockSpec((MM_TM, k), lambda i, j: (i + t0, 0)),
                  pl.BlockSpec((k, tn), lambda i, j: (0, j))] + r_specs + [
                  pl.BlockSpec((1, 1, 1, tn),
                               lambda i, j: (layer, _group_of_row((i + t0) * MM_TM), 0, g_k * nj + j))],
        out_specs=pl.BlockSpec((MM_TM, tn), lambda i, j: (i + t0, j)),
        out_shape=jax.ShapeDtypeStruct((m, n), F32),
        compiler_params=_cparams("arbitrary", "arbitrary"),
        name="matmul_resid",
    )(a, w, *r_args, modt)


CV_TM = 256
CV_HALO = 16
CV_RC = 64
CV_CC = 128
CV_SUB = SUBLANES
CV_SHROWS = CV_TM + 2 * CV_HALO - CV_SUB


def _conv_kernel(t0, prev_ref, cur_ref, next_ref, w_ref, cb_ref, lg_ref, lb_ref, o_ref, buf_ref, acc_ref):
    i = pl.program_id(0) + t0
    row0 = i * CV_TM
    lat = row0 - CTX_ROWS
    seq_start = jnp.where(row0 < CTX_ROWS, row0 % L == 0, lat % S == 0)
    seq_end = jnp.where(row0 < CTX_ROWS, (row0 + CV_TM) % L == 0, (lat + CV_TM) % S == 0)

    def glu(u):
        return u[:, :CONV_CH] * _sigmoid(u[:, CONV_CH:])

    buf_ref[0, 0:CV_HALO, :] = glu(prev_ref[...]) * jnp.where(seq_start, 0.0, 1.0)
    buf_ref[0, CV_HALO:CV_HALO + CV_TM, :] = glu(cur_ref[...])
    buf_ref[0, CV_HALO + CV_TM:, :] = glu(next_ref[...]) * jnp.where(seq_end, 0.0, 1.0)
    for s in range(1, CV_SUB):
        buf_ref[s, 0:CV_SHROWS, :] = buf_ref[0, s:s + CV_SHROWS, :]

    tap0 = CV_HALO - CONV_K // 2

    def col_chunk(c, carry):
        c0 = pl.multiple_of(c * CV_CC, CV_CC)
        for r in range(CV_TM // CV_RC):
            acc = jnp.zeros((CV_RC, CV_CC), F32)
            for k in range(CONV_K):
                q, s = divmod(k + tap0, CV_SUB)
                r0 = r * CV_RC + q * CV_SUB
                acc = acc + w_ref[k:k + 1, pl.ds(c0, CV_CC)] * buf_ref[s, r0:r0 + CV_RC, pl.ds(c0, CV_CC)]
            acc_ref[r * CV_RC:(r + 1) * CV_RC, pl.ds(c0, CV_CC)] = acc
        return carry

    lax.fori_loop(0, CONV_CH // CV_CC, col_chunk, 0)

    h = acc_ref[...] + cb_ref[...]
    mu = jnp.mean(h, axis=-1, keepdims=True)
    xc = h - mu
    var = jnp.mean(xc * xc, axis=-1, keepdims=True)
    y = xc * lax.rsqrt(var + EPS) * lg_ref[...] + lb_ref[...]
    o_ref[...] = (y * _sigmoid(y)).astype(BF16)


def _conv_module(u, conv_w, conv_b, ln_g, ln_b, latent_only=False):
    hb = CV_TM // CV_HALO
    last = T // CV_HALO - 1
    t0 = _first_tile(CV_TM, latent_only)
    vec = pl.BlockSpec((1, CONV_CH), lambda i: (0, 0))
    return pl.pallas_call(
        functools.partial(_conv_kernel, t0),
        grid=(T // CV_TM - t0,),
        in_specs=[
            pl.BlockSpec((CV_HALO, W_CONV_IN), lambda i: (jnp.maximum((i + t0) * hb - 1, 0), 0)),
            pl.BlockSpec((CV_TM, W_CONV_IN), lambda i: (i + t0, 0)),
            pl.BlockSpec((CV_HALO, W_CONV_IN), lambda i: (jnp.minimum((i + t0 + 1) * hb, last), 0)),
            pl.BlockSpec((CONV_K, CONV_CH), lambda i: (0, 0)),
            vec, vec, vec,
        ],
        out_specs=pl.BlockSpec((CV_TM, CONV_CH), lambda i: (i + t0, 0)),
        out_shape=jax.ShapeDtypeStruct((T, CONV_CH), BF16),
        scratch_shapes=[pltpu.VMEM((CV_SUB, CV_TM + 2 * CV_HALO, CONV_CH), F32),
                        pltpu.VMEM((CV_TM, CONV_CH), F32)],
        compiler_params=_cparams("arbitrary"),
        name="conv_module",
    )(u, u, u, conv_w, conv_b.reshape(1, CONV_CH), ln_g.reshape(1, CONV_CH), ln_b.reshape(1, CONV_CH))


HN_TM = 256


def _headnorm_kernel(qw_ref, qn_ref, kvw_ref, kn_ref, vn_ref, cos_ref, sin_ref, g_ref,
                     qw_o, qn_o, kvw_o, kn_o, vn_o):
    cosf = cos_ref[...]
    sinf = sin_ref[...]
    lane = lax.broadcasted_iota(jnp.int32, (HN_TM, HD), 1)
    first_half = (lane % (HD // 2)) < (HD // 4)

    def norm(x, gi):
        ms = jnp.mean(x * x, axis=-1, keepdims=True)
        return x * lax.rsqrt(ms + EPS) * g_ref[gi:gi + 1, :]

    def rope(x):
        partner = jnp.where(first_half, pltpu.roll(x, HD - HD // 4, 1), pltpu.roll(x, HD // 4, 1))
        return x * cosf + partner * sinf

    for hd in range(WIN_HQ):
        sl = slice(hd * HD, (hd + 1) * HD)
        qw_o[:, sl] = rope(norm(qw_ref[:, sl], 0)).astype(BF16)
    for hd in range(WIN_HKV):
        sl = slice(hd * HD, (hd + 1) * HD)
        kvw_o[:, sl] = rope(norm(kvw_ref[:, sl], 1)).astype(BF16)
    kvw_o[:, W_WIN_KV:] = kvw_ref[:, W_WIN_KV:].astype(BF16)
    for hd in range(NA_H):
        sl = slice(hd * HD, (hd + 1) * HD)
        qn_o[:, sl] = norm(qn_ref[:, sl], 2).astype(BF16)
        kn_o[:, sl] = norm(kn_ref[:, sl], 3).astype(BF16)
    vn_o[...] = vn_ref[...].astype(BF16)


def _headnorm(u, cosf, sinf, gains):
    tm = HN_TM
    kvb = KV_OFF // W_NA_KV
    row128 = pl.BlockSpec((tm, HD), lambda i: (i, 0))

    def ospec(w):
        return pl.BlockSpec((tm, w), lambda i: (i, 0))

    def oshape(w):
        return jax.ShapeDtypeStruct((T, w), BF16)

    return pl.pallas_call(
        _headnorm_kernel,
        grid=(T // tm,),
        in_specs=[
            pl.BlockSpec((tm, W_WIN_Q), lambda i: (i, Q_WIN_OFF // W_WIN_Q)),
            pl.BlockSpec((tm, W_NA_Q), lambda i: (i, Q_NA_OFF // W_NA_Q)),
            pl.BlockSpec((tm, 2 * W_WIN_KV), lambda i: (i, kvb)),
            pl.BlockSpec((tm, W_NA_KV), lambda i: (i, kvb + 1)),
            pl.BlockSpec((tm, W_NA_KV), lambda i: (i, kvb + 2)),
            row128, row128,
            pl.BlockSpec((SUBLANES, HD), lambda i: (0, 0)),
        ],
        out_specs=[ospec(W_WIN_Q), ospec(W_NA_Q), ospec(2 * W_WIN_KV), ospec(W_NA_KV), ospec(W_NA_KV)],
        out_shape=[oshape(W_WIN_Q), oshape(W_NA_Q), oshape(2 * W_WIN_KV), oshape(W_NA_KV), oshape(W_NA_KV)],
        compiler_params=_cparams("arbitrary"),
        name="headnorm_rope",
    )(u, u, u, u, u, cosf, sinf, gains)


def _rope_tables():
    t = jnp.arange(S)
    row = (t // GRID_W).astype(F32)
    col = (t % GRID_W).astype(F32)
    axis_dim = HD // 2
    inv = ROPE_BASE ** (-jnp.arange(0, axis_dim, 2, dtype=F32) / axis_dim)
    ar = row[:, None] * inv
    ac = col[:, None] * inv
    cosl = jnp.concatenate([jnp.cos(ar), jnp.cos(ar), jnp.cos(ac), jnp.cos(ac)], axis=1)
    sinl = jnp.concatenate([-jnp.sin(ar), jnp.sin(ar), -jnp.sin(ac), jnp.sin(ac)], axis=1)
    cosf = jnp.concatenate([jnp.ones((CTX_ROWS, HD), F32), cosl, cosl], axis=0)
    sinf = jnp.concatenate([jnp.zeros((CTX_ROWS, HD), F32), sinl, sinl], axis=0)
    return cosf, sinf


WA_TQ = 128
WA_CTX_TILES = CTX_ROWS // WA_TQ
WA_LAT_TILES = S // WA_TQ


def _win_attn_kernel(t0, sink_ref, q_ref, kp_ref, kc_ref, kn_ref, kx_ref, o_ref):
    i = pl.program_id(0) + t0
    is_ctx = i < WA_CTX_TILES
    n = (i - WA_CTX_TILES) % WA_LAT_TILES
    far = 4 * WA_TQ
    off_prev = jnp.where(jnp.logical_or(is_ctx, n == 0), far, 0)
    off_cur = jnp.where(is_ctx, far, 0)
    off_next = jnp.where(jnp.logical_or(is_ctx, n == WA_LAT_TILES - 1), far, 0)
    rows = WIN_G * WA_TQ
    r = lax.broadcasted_iota(jnp.int32, (rows, WA_TQ), 0) % WA_TQ
    c = lax.broadcasted_iota(jnp.int32, (rows, WA_TQ), 1)
    m_prev = c >= r + off_prev
    m_cur = c >= off_cur
    m_next = c + off_next <= r
    grp = lax.broadcasted_iota(jnp.int32, (rows, 1), 0) // WA_TQ

    for h in range(WIN_HKV):
        q = jnp.concatenate([q_ref[:, (h * WIN_G + g) * HD:(h * WIN_G + g + 1) * HD] for g in range(WIN_G)], axis=0)
        ks = slice(h * HD, (h + 1) * HD)
        vs = slice(W_WIN_KV + h * HD, W_WIN_KV + (h + 1) * HD)
        s_p = jnp.where(m_prev, _dot_nt(q, kp_ref[:, ks]) * ATT_SCALE, NEG)
        s_c = jnp.where(m_cur, _dot_nt(q, kc_ref[:, ks]) * ATT_SCALE, NEG)
        s_n = jnp.where(m_next, _dot_nt(q, kn_ref[:, ks]) * ATT_SCALE, NEG)
        s_x = _dot_nt(q, kx_ref[:, ks]) * ATT_SCALE
        snk = jnp.zeros((rows, 1), F32)
        for g in range(WIN_G):
            snk = jnp.where(grp == g, sink_ref[h * WIN_G + g], snk)
        m = jnp.maximum(jnp.maximum(jnp.max(s_p, axis=-1, keepdims=True), jnp.max(s_c, axis=-1, keepdims=True)),
                        jnp.maximum(jnp.max(s_n, axis=-1, keepdims=True), jnp.max(s_x, axis=-1, keepdims=True)))
        m = jnp.maximum(m, snk)
        p_p = jnp.exp(s_p - m)
        p_c = jnp.exp(s_c - m)
        p_n = jnp.exp(s_n - m)
        p_x = jnp.exp(s_x - m)
        den = (jnp.sum(p_p, axis=-1, keepdims=True) + jnp.sum(p_c, axis=-1, keepdims=True)
               + jnp.sum(p_n, axis=-1, keepdims=True) + jnp.sum(p_x, axis=-1, keepdims=True) + jnp.exp(snk - m))
        o = (jnp.dot(p_p.astype(BF16), kp_ref[:, vs], preferred_element_type=F32)
             + jnp.dot(p_c.astype(BF16), kc_ref[:, vs], preferred_element_type=F32)
             + jnp.dot(p_n.astype(BF16), kn_ref[:, vs], preferred_element_type=F32)
             + jnp.dot(p_x.astype(BF16), kx_ref[:, vs], preferred_element_type=F32))
        o = o / den
        for g in range(WIN_G):
            o_ref[:, (h * WIN_G + g) * HD:(h * WIN_G + g + 1) * HD] = o[g * WA_TQ:(g + 1) * WA_TQ].astype(BF16)


def _win_attention(qw, kvw, sink, latent_only=False):
    nt = T // WA_TQ
    t0 = _first_tile(WA_TQ, latent_only)

    def bounds(i):
        is_ctx = i < WA_CTX_TILES
        b = (i - WA_CTX_TILES) // WA_LAT_TILES
        lo = jnp.where(is_ctx, 0, WA_CTX_TILES + b * WA_LAT_TILES)
        hi = jnp.where(is_ctx, nt - 1, WA_CTX_TILES + (b + 1) * WA_LAT_TILES - 1)
        return lo, hi

    def prev_map(g):
        i = g + t0
        lo, _ = bounds(i)
        return (jnp.maximum(i - 1, lo), 0)

    def next_map(g):
        i = g + t0
        _, hi = bounds(i)
        return (jnp.minimum(i + 1, hi), 0)

    def ctx_map(g):
        i = g + t0
        b = jnp.where(i < WA_CTX_TILES, i // (L // WA_TQ), (i - WA_CTX_TILES) // WA_LAT_TILES)
        return (b, 0)

    kvw_w = 2 * W_WIN_KV
    return pl.pallas_call(
        functools.partial(_win_attn_kernel, t0),
        grid=(nt - t0,),
        in_specs=[
            pl.BlockSpec(memory_space=pltpu.SMEM),
            pl.BlockSpec((WA_TQ, W_WIN_Q), lambda g: (g + t0, 0)),
            pl.BlockSpec((WA_TQ, kvw_w), prev_map),
            pl.BlockSpec((WA_TQ, kvw_w), lambda g: (g + t0, 0)),
            pl.BlockSpec((WA_TQ, kvw_w), next_map),
            pl.BlockSpec((L, kvw_w), ctx_map),
        ],
        out_specs=pl.BlockSpec((WA_TQ, W_WIN_Q), lambda g: (g + t0, 0)),
        out_shape=jax.ShapeDtypeStruct((T, W_WIN_Q), BF16),
        compiler_params=_cparams("arbitrary"),
        name="window_attention",
    )(sink, qw, kvw, kvw, kvw, kvw)


NA_TQ = 256
NA_QROWS = NA_TQ // GRID_W
NA_KROWS = 3 * NA_QROWS
NA_NKEY = NA_KROWS * GRID_W
NA_TYPES = 4
RPB_R = 2 * NA_KH - 1
RPB_C = 2 * NA_KW - 1


def _na_row_valid(ty, a, j):
    if ty == 0:
        return NA_QROWS <= j < NA_QROWS + NA_KH
    if ty == 1:
        return a <= j < a + NA_KH
    if ty == 2:
        return j < NA_KH
    return False


def _rpb_kernel(rpb_ref, o_ref):
    h = pl.program_id(0)
    shp = (GRID_W, 2 * GRID_W)
    qc = lax.broadcasted_iota(jnp.int32, shp, 0)
    lane = lax.broadcasted_iota(jnp.int32, shp, 1)
    kc = lane % GRID_W
    second = lane >= GRID_W
    dcol = kc - qc + (NA_KW - 1)
    cs = jnp.clip(qc - NA_KW // 2, 0, GRID_W - NA_KW)
    colmask = jnp.logical_and(kc >= cs, kc < cs + NA_KW)
    neg = jnp.full(shp, NEG, F32)
    base = h * (RPB_R * RPB_C)
    pair = []
    for dr in range(RPB_R - 1):
        acc = jnp.zeros(shp, F32)
        for dd in range(RPB_C):
            v0 = rpb_ref[base + dr * RPB_C + dd]
            v1 = rpb_ref[base + (dr + 1) * RPB_C + dd]
            acc = jnp.where(dcol == dd, jnp.where(second, v1, v0), acc)
        pair.append(jnp.where(colmask, acc, neg))
    for ty in range(NA_TYPES):
        for a in range(NA_QROWS):
            for jp in range(NA_KROWS // 2):
                j = 2 * jp
                ok0 = _na_row_valid(ty, a, j)
                ok1 = _na_row_valid(ty, a, j + 1)
                dr = j - a + NA_QROWS - 1
                if ok0 and ok1:
                    tile = pair[dr]
                elif ok0:
                    tile = jnp.where(second, neg, pair[dr])
                elif ok1:
                    tile = jnp.where(second, pair[dr], neg)
                else:
                    tile = neg
                o_ref[ty, 0, a * GRID_W:(a + 1) * GRID_W, jp * 2 * GRID_W:(jp + 1) * 2 * GRID_W] = tile


def _rpb_tiles(rpb):
    return pl.pallas_call(
        _rpb_kernel,
        grid=(NA_H,),
        in_specs=[pl.BlockSpec(memory_space=pltpu.SMEM)],
        out_specs=pl.BlockSpec((NA_TYPES, 1, NA_TQ, NA_NKEY), lambda h: (0, h, 0, 0)),
        out_shape=jax.ShapeDtypeStruct((NA_TYPES, NA_H, NA_TQ, NA_NKEY), F32),
        compiler_params=_cparams("arbitrary"),
        name="rpb_tiles",
    )(rpb.reshape(-1))


NA_CTX_TILES = CTX_ROWS // NA_TQ
NA_LAT_TILES = S // NA_TQ


def _na_attn_kernel(q_ref, kp_ref, kc_ref, kn_ref, vp_ref, vc_ref, vn_ref, kx_ref, vx_ref, bias_ref, o_ref):
    for h in range(NA_H):
        hs = slice(h * HD, (h + 1) * HD)
        q = q_ref[:, hs]
        s_p = _dot_nt(q, kp_ref[:, hs]) * ATT_SCALE + bias_ref[0, h, :, 0:NA_TQ]
        s_c = _dot_nt(q, kc_ref[:, hs]) * ATT_SCALE + bias_ref[0, h, :, NA_TQ:2 * NA_TQ]
        s_n = _dot_nt(q, kn_ref[:, hs]) * ATT_SCALE + bias_ref[0, h, :, 2 * NA_TQ:3 * NA_TQ]
        s_x = _dot_nt(q, kx_ref[:, hs]) * ATT_SCALE
        m = jnp.maximum(jnp.maximum(jnp.max(s_p, axis=-1, keepdims=True), jnp.max(s_c, axis=-1, keepdims=True)),
                        jnp.maximum(jnp.max(s_n, axis=-1, keepdims=True), jnp.max(s_x, axis=-1, keepdims=True)))
        p_p = jnp.exp(s_p - m)
        p_c = jnp.exp(s_c - m)
        p_n = jnp.exp(s_n - m)
        p_x = jnp.exp(s_x - m)
        den = (jnp.sum(p_p, axis=-1, keepdims=True) + jnp.sum(p_c, axis=-1, keepdims=True)
               + jnp.sum(p_n, axis=-1, keepdims=True) + jnp.sum(p_x, axis=-1, keepdims=True))
        o = (jnp.dot(p_p.astype(BF16), vp_ref[:, hs], preferred_element_type=F32)
             + jnp.dot(p_c.astype(BF16), vc_ref[:, hs], preferred_element_type=F32)
             + jnp.dot(p_n.astype(BF16), vn_ref[:, hs], preferred_element_type=F32)
             + jnp.dot(p_x.astype(BF16), vx_ref[:, hs], preferred_element_type=F32))
        o_ref[:, hs] = (o / den).astype(BF16)


def _na_attention(qn, kn, vn, bias, latent_only=False):
    nt = T // NA_TQ
    t0 = _first_tile(NA_TQ, latent_only)

    def bounds(i):
        is_ctx = i < NA_CTX_TILES
        b = (i - NA_CTX_TILES) // NA_LAT_TILES
        lo = jnp.where(is_ctx, 0, NA_CTX_TILES + b * NA_LAT_TILES)
        hi = jnp.where(is_ctx, nt - 1, NA_CTX_TILES + (b + 1) * NA_LAT_TILES - 1)
        return lo, hi

    def prev_map(g):
        i = g + t0
        lo, _ = bounds(i)
        return (jnp.maximum(i - 1, lo), 0)

    def next_map(g):
        i = g + t0
        _, hi = bounds(i)
        return (jnp.minimum(i + 1, hi), 0)

    def ctx_map(g):
        i = g + t0
        return (jnp.where(i < NA_CTX_TILES, i, (i - NA_CTX_TILES) // NA_LAT_TILES), 0)

    def bias_map(g):
        i = g + t0
        n = (i - NA_CTX_TILES) % NA_LAT_TILES
        ty = jnp.where(i < NA_CTX_TILES, 3, jnp.where(n == 0, 0, jnp.where(n == NA_LAT_TILES - 1, 2, 1)))
        return (ty, 0, 0, 0)

    w = W_NA_KV
    cur = pl.BlockSpec((NA_TQ, w), lambda g: (g + t0, 0))
    prv = pl.BlockSpec((NA_TQ, w), prev_map)
    nxt = pl.BlockSpec((NA_TQ, w), next_map)
    ctx = pl.BlockSpec((L, w), ctx_map)
    return pl.pallas_call(
        _na_attn_kernel,
        grid=(nt - t0,),
        in_specs=[cur, prv, cur, nxt, prv, cur, nxt, ctx, ctx,
                  pl.BlockSpec((1, NA_H, NA_TQ, NA_NKEY), bias_map)],
        out_specs=cur,
        out_shape=jax.ShapeDtypeStruct((T, W_NA_Q), BF16),
        compiler_params=_cparams("arbitrary"),
        name="neighborhood_attention",
    )(qn, kn, kn, kn, vn, vn, vn, kn, vn, bias)


MG_TM = 512
MG_TN = 1024


def _merge_kernel(ca_ref, aw_ref, an_ref, wc_ref, ww_ref, wn_ref, ga_ref, gb_ref, gc_ref, o_ref):
    ya = jnp.dot(ca_ref[...], wc_ref[...], preferred_element_type=F32)
    yb = jnp.dot(aw_ref[...], ww_ref[...], preferred_element_type=F32)
    yc = jnp.dot(an_ref[...], wn_ref[...], preferred_element_type=F32)
    o = _sigmoid(ga_ref[...]) * ya + _sigmoid(gb_ref[...]) * yb + _sigmoid(gc_ref[...]) * yc
    o_ref[...] = o.astype(BF16)


def _merge(hconv, aw, an, w_conv_out, w_win_out, w_na_out, u, latent_only=False):
    gb0 = GATE_OFF // MG_TN
    gstep = D // MG_TN
    t0 = _first_tile(MG_TM, latent_only)

    def a_spec(k):
        return pl.BlockSpec((MG_TM, k), lambda i, j: (i + t0, 0))

    def w_spec(k):
        return pl.BlockSpec((k, MG_TN), lambda i, j: (0, j))

    def g_spec(which):
        return pl.BlockSpec((MG_TM, MG_TN), lambda i, j: (i + t0, gb0 + which * gstep + j))

    return pl.pallas_call(
        _merge_kernel,
        grid=(T // MG_TM - t0, D // MG_TN),
        in_specs=[a_spec(CONV_CH), a_spec(W_WIN_Q), a_spec(W_NA_Q),
                  w_spec(CONV_CH), w_spec(W_WIN_Q), w_spec(W_NA_Q),
                  g_spec(0), g_spec(1), g_spec(2)],
        out_specs=pl.BlockSpec((MG_TM, MG_TN), lambda i, j: (i + t0, j)),
        out_shape=jax.ShapeDtypeStruct((T, D), BF16),
        compiler_params=_cparams("arbitrary", "arbitrary"),
        name="gated_merge",
    )(hconv, aw, an, w_conv_out, w_win_out, w_na_out, u, u, u)


TK_TT = 256


def _topk_rounds(scores, n_rounds):
    nrow = scores.shape[0]
    idx = lax.broadcasted_iota(jnp.int32, scores.shape, 0).astype(F32)
    work = scores
    rank = jnp.full(scores.shape, float(n_rounds), F32)
    vals = []
    for a in range(n_rounds):
        m = jnp.max(work, axis=0, keepdims=True)
        first = jnp.min(jnp.where(work == m, idx, float(nrow)), axis=0, keepdims=True)
        sel = idx == first
        rank = jnp.where(sel, float(a), rank)
        work = jnp.where(sel, -jnp.inf, work)
        vals.append(m)
    return jnp.concatenate(vals, axis=0), rank


def _topk_rounds_no_ties(scores, n_rounds):
    work = scores
    rank = jnp.full(scores.shape, float(n_rounds), F32)
    vals = []
    for a in range(n_rounds):
        m = jnp.max(work, axis=0, keepdims=True)
        sel = work == m
        rank = jnp.where(sel, float(a), rank)
        work = jnp.where(sel, -jnp.inf, work)
        vals.append(m)
    taken = jnp.sum(jnp.where(rank < float(n_rounds), 1.0, 0.0), axis=0, keepdims=True)
    return jnp.concatenate(vals, axis=0), rank, taken


def _peer_topk_kernel(q_ref, k1_ref, k2_ref, n_ref, e1_ref, r2_ref, e2_ref):
    kk = PEER_TOPK

    def head(h, carry):
        c1 = pl.multiple_of(h * 2 * HD, 2 * HD)
        q1 = q_ref[:, pl.ds(c1, HD)]
        q2 = q_ref[:, pl.ds(c1 + HD, HD)]
        s1 = _dot_nt(k1_ref[h].astype(BF16), q1)
        s2 = _dot_nt(k2_ref[h].astype(BF16), q2)
        f1, fr1, t1 = _topk_rounds_no_ties(s1, kk)
        f2, fr2, t2 = _topk_rounds_no_ties(s2, kk)
        tied = jnp.max(jnp.abs(t1 - float(kk)) + jnp.abs(t2 - float(kk))) > 0.0

        def exact(_):
            return _topk_rounds(s1, kk) + _topk_rounds(s2, kk)

        def keep(_):
            return f1, fr1, f2, fr2

        v1all, rank1, v2all, rank2 = lax.cond(tied, exact, keep, None)
        v1 = [v1all[0:1, :]]
        v2 = [v2all[0:1, :]]
        arow = lax.broadcasted_iota(jnp.int32, v2all.shape, 0).astype(F32)
        cnt = jnp.zeros(v2all.shape, F32)
        front = v1all + v2[0]
        top = v1[0] + v2[0]
        z = jnp.zeros_like(top)
        for _ in range(kk):
            m = jnp.max(front, axis=0, keepdims=True)
            first = jnp.min(jnp.where(front == m, arow, float(kk)), axis=0, keepdims=True)
            sel = arow == first
            cnt = cnt + jnp.where(sel, 1.0, 0.0)
            z = z + jnp.exp(m - top)
            taken = jnp.max(jnp.where(sel, cnt, -1.0), axis=0, keepdims=True)
            nxt = jnp.max(jnp.where(arow == taken, v2all, -jnp.inf), axis=0, keepdims=True)
            front = jnp.where(sel, v1all + nxt, front)
        nfull = jnp.zeros(s1.shape, F32)
        for a in range(kk):
            nfull = jnp.where(rank1 == float(a), cnt[a:a + 1, :], nfull)
        n_ref[h] = nfull
        e1_ref[h] = jnp.exp(s1 - v1[0]) / z
        r2_ref[h] = rank2.astype(BF16)
        e2_ref[h] = jnp.exp(s2 - v2[0]).astype(BF16)
        return carry

    lax.fori_loop(0, PEER_HEADS, head, 0)


def _peer_topk(q, k1, k2, latent_only=False):
    tt = TK_TT
    t0 = _first_tile(tt, latent_only)
    kspec = pl.BlockSpec((PEER_HEADS, N_KEYS, HD), lambda i: (0, 0, 0))
    ospec = pl.BlockSpec((PEER_HEADS, N_KEYS, tt), lambda i: (0, 0, i + t0))
    oshape = jax.ShapeDtypeStruct((PEER_HEADS, N_KEYS, T), F32)
    oshape_b = jax.ShapeDtypeStruct((PEER_HEADS, N_KEYS, T), BF16)
    return pl.pallas_call(
        _peer_topk_kernel,
        grid=(T // tt - t0,),
        in_specs=[pl.BlockSpec((tt, 2 * HD * PEER_HEADS), lambda i: (i + t0, 0)), kspec, kspec],
        out_specs=[ospec] * 4,
        out_shape=[oshape, oshape, oshape_b, oshape_b],
        compiler_params=_cparams("arbitrary"),
        name="peer_topk",
    )(q, k1, k2)


EX_TM = 512
EX_TN = 512
EX_SUB = 256
EX_NC = 1024
EX_LC = 128
EX_RC = 16
EX_I1 = EX_TN // N_KEYS
EX_STEPS_PER_GROUP = SUBLANES // EX_I1
assert EX_I1 * EX_STEPS_PER_GROUP == SUBLANES and EX_STEPS_PER_GROUP == 2


def _gelu_tanh(x):
    c = math.sqrt(2.0 / math.pi)
    t = jnp.tanh(x * (c + (c * 0.044715) * (x * x)))
    hx = 0.5 * x
    return hx + hx * t


def _experts_kernel(h_ref, u_ref, v_ref, n_ref, e1_ref, r2_ref, e2_ref, o_ref):
    j = pl.program_id(1)

    @pl.when(j == 0)
    def _():
        o_ref[...] = jnp.zeros_like(o_ref)

    grp = pl.multiple_of((j // EX_STEPS_PER_GROUP) * SUBLANES, SUBLANES)
    upper = (j % EX_STEPS_PER_GROUP) == 1
    key_rows = {}
    for hh in range(PEER_HEADS):
        for c in range(EX_TM // EX_LC):
            cs = slice(c * EX_LC, (c + 1) * EX_LC)
            n8 = n_ref[hh, pl.ds(grp, SUBLANES), cs]
            e8 = e1_ref[hh, pl.ds(grp, SUBLANES), cs]
            for k in range(EX_I1):
                nrow = jnp.where(upper, n8[EX_I1 + k:EX_I1 + k + 1], n8[k:k + 1])
                e1row = jnp.where(upper, e8[EX_I1 + k:EX_I1 + k + 1], e8[k:k + 1])
                key_rows[(k, hh, c)] = (jnp.broadcast_to(nrow, (EX_RC, EX_LC)).astype(BF16),
                                        jnp.broadcast_to(e1row, (EX_RC, EX_LC)).astype(BF16))

    def gate_tile(k, rows, c):
        cs = slice(c * EX_LC, (c + 1) * EX_LC)
        gate = jnp.zeros((EX_RC, EX_LC), BF16)
        for hh in range(PEER_HEADS):
            nrow, e1row = key_rows[(k, hh, c)]
            gate = gate + jnp.where(r2_ref[hh, rows, cs] < nrow, e2_ref[hh, rows, cs], 0.0) * e1row
        return gate

    w_parts = []
    for s in range(EX_TN // EX_SUB):
        es = slice(s * EX_SUB, (s + 1) * EX_SUB)
        act = _gelu_tanh(jnp.dot(u_ref[es, :], h_ref[...], preferred_element_type=F32)).astype(BF16)
        for ii in range(EX_SUB // N_KEYS):
            k = s * (EX_SUB // N_KEYS) + ii
            for rc in range(N_KEYS // EX_RC):
                rows = slice(rc * EX_RC, (rc + 1) * EX_RC)
                arows = slice(ii * N_KEYS + rc * EX_RC, ii * N_KEYS + (rc + 1) * EX_RC)
                row = []
                for c in range(EX_TM // EX_LC):
                    cs = slice(c * EX_LC, (c + 1) * EX_LC)
                    row.append(gate_tile(k, rows, c) * act[arows, cs])
                w_parts.append(jnp.concatenate(row, axis=1))
    w_t = jnp.concatenate(w_parts, axis=0)
    for nc in range(D // EX_NC):
        ns = slice(nc * EX_NC, (nc + 1) * EX_NC)
        o_ref[:, ns] += lax.dot_general(w_t, v_ref[:, ns], (((0,), (0,)), ((), ())),
                                        preferred_element_type=F32)


def _experts(h_t, u, v, tables, latent_only=False):
    n, e1, r2, e2 = tables
    t0 = _first_tile(EX_TM, latent_only)
    tspec = pl.BlockSpec((PEER_HEADS, N_KEYS, EX_TM), lambda i, j: (0, 0, i + t0))
    return pl.pallas_call(
        _experts_kernel,
        grid=(T // EX_TM - t0, N_EXPERTS // EX_TN),
        in_specs=[pl.BlockSpec((D, EX_TM), lambda i, j: (0, i + t0)),
                  pl.BlockSpec((EX_TN, D), lambda i, j: (j, 0)),
                  pl.BlockSpec((EX_TN, D), lambda i, j: (j, 0)),
                  tspec, tspec, tspec, tspec],
        out_specs=pl.BlockSpec((EX_TM, D), lambda i, j: (i + t0, 0), pipeline_mode=pl.Buffered(1)),
        out_shape=jax.ShapeDtypeStruct((T, D), F32),
        compiler_params=_cparams("arbitrary", "arbitrary"),
        name="peer_experts",
    )(h_t, u, v, n, e1, r2, e2)


FR_TM = 256


def _final_kernel(x_ref, f_ref, g_ref, o_ref):
    o_ref[...] = x_ref[...] + g_ref[0, 0] * f_ref[...]


def _final_residual(x, f, modt, layer, g_k):
    off = CTX_ROWS // FR_TM
    lat = pl.BlockSpec((FR_TM, D), lambda i: (i + off, 0))
    return pl.pallas_call(
        _final_kernel,
        grid=(B * S // FR_TM,),
        in_specs=[lat, lat,
                  pl.BlockSpec((1, 1, 1, D), lambda i: (layer, _group_of_row((i + off) * FR_TM), 0, g_k))],
        out_specs=pl.BlockSpec((FR_TM, D), lambda i: (i, 0)),
        out_shape=jax.ShapeDtypeStruct((B * S, D), F32),
        compiler_params=_cparams("arbitrary"),
        name="final_residual",
    )(x, f, modt)


SH1, SC1, G1, SH2, SC2, G2 = range(6)


def kernel(x, c, ctx, c_ctx, w_ada, b_ada, norm1_g, norm2_g, w_in, conv_w, conv_b, conv_ln_g, conv_ln_b,
           w_conv_out, win_qn_g, win_kn_g, win_sink, w_win_out, na_qn_g, na_kn_g, na_rpb, w_na_out, w_out,
           peer_wq, peer_k1, peer_k2, peer_u, peer_v):
    xs = (ctx.reshape(CTX_ROWS, D), x.reshape(B * S, D))
    cvec = jnp.concatenate([c_ctx[None], c, jnp.zeros((N_GROUPS - 1 - B, D), F32)], axis=0)
    modt = _mods(cvec, w_ada, b_ada).reshape(DEPTH, N_GROUPS, 1, 6 * D)
    cosf, sinf = _rope_tables()

    f = None
    for l in range(DEPTH):
        if l == 0:
            h1 = _normmod(xs, modt, l, norm1_g[l], SC1, SH1)
        else:
            xs, h1 = _normmod(xs, modt, l, norm1_g[l], SC1, SH1, f=f, g_k=G2)
        lat = l == DEPTH - 1
        u = _matmul_wstat(h1, w_in, l, F32, 1024)
        hconv = _conv_module(u, conv_w[l], conv_b[l], conv_ln_g[l], conv_ln_b[l], latent_only=lat)
        gains = jnp.concatenate([win_qn_g[l][None], win_kn_g[l][None], na_qn_g[l][None], na_kn_g[l][None],
                                 jnp.zeros((SUBLANES - 4, HD), F32)], axis=0)
        qw, qn, kvw, kn, vn = _headnorm(u, cosf, sinf, gains)
        aw = _win_attention(qw, kvw, win_sink[l], latent_only=lat)
        an = _na_attention(qn, kn, vn, _rpb_tiles(na_rpb[l]), latent_only=lat)
        merged = _merge(hconv, aw, an, _cast_bf16(w_conv_out, l), _cast_bf16(w_win_out, l),
                        _cast_bf16(w_na_out, l), u, latent_only=lat)
        xs = _matmul_resid(merged, _cast_bf16(w_out, l), xs, modt, l, G1, 1024, latent_only=lat)
        h2, h2_t = _normmod(xs, modt, l, norm2_g[l], SC2, SH2, want_t=True, latent_only=lat)
        q = _matmul(h2, _cast_bf16(peer_wq, l), BF16, 1024, latent_only=lat)
        tables = _peer_topk(q, peer_k1[l], peer_k2[l], latent_only=lat)
        f = _experts(h2_t, _cast_bf16(peer_u, l), _cast_bf16(peer_v, l), tables, latent_only=lat)
    out = _final_residual(xs, f, modt, DEPTH - 1, G2)
    return out.reshape(B, S, D)
```

```python
import functools
import math

import jax
import jax.numpy as jnp
import numpy as np
from jax import lax
from jax.experimental import pallas as pl
from jax.experimental.pallas import tpu as pltpu

F32 = jnp.float32
BF16 = jnp.bfloat16

D = 4096
B = 2
S = 4096
L = 256
DEPTH = 2
GRID_W = 64
HD = 128
EPS = 1e-6
NEG = -1e30
ROPE_BASE = 10000.0

CONV_CH = D // 4
CONV_K = 31
WIN_HQ = 16
WIN_HKV = 4
WIN_G = WIN_HQ // WIN_HKV
NA_H = 8
NA_KH = 8
NA_KW = 16
PEER_HEADS = 8
N_KEYS = 128
N_EXPERTS = N_KEYS * N_KEYS
PEER_TOPK = 16

W_CONV_IN = 2 * CONV_CH
W_WIN_Q = WIN_HQ * HD
W_NA_Q = NA_H * HD
W_GATE = 3 * D
W_WIN_KV = WIN_HKV * HD
W_NA_KV = NA_H * HD
Q_WIN_OFF = W_CONV_IN
Q_NA_OFF = Q_WIN_OFF + W_WIN_Q
GATE_OFF = Q_NA_OFF + W_NA_Q
KV_OFF = GATE_OFF + W_GATE
IN_COLS = KV_OFF + 2 * W_WIN_KV + 2 * W_NA_KV

SUBLANES = 8

CTX_ROWS = B * L
T = CTX_ROWS + B * S
N_GROUPS = SUBLANES
ATT_SCALE = HD ** -0.5

VMEM_LIMIT = 56 * 1024 * 1024


def _cparams(*sem):
    return pltpu.CompilerParams(dimension_semantics=sem, vmem_limit_bytes=VMEM_LIMIT)


def _group_of_row(row0):
    return jnp.where(row0 < CTX_ROWS, 0, 1 + (row0 - CTX_ROWS) // S)


def _sigmoid(z):
    return 0.5 * jnp.tanh(0.5 * z) + 0.5


def _dot_nt(a, b):
    return lax.dot_general(a, b, (((1,), (1,)), ((), ())), preferred_element_type=F32)


def _first_tile(tile, latent_only):
    return CTX_ROWS // tile if latent_only else 0


CAST_BLOCK_BYTES = 8 * 1024 * 1024


def _cast_kernel(w_ref, o_ref):
    o_ref[...] = w_ref[0].astype(BF16)


def _cast_bf16(w, layer):
    _, m, n = w.shape
    rb = m
    while rb * n * 4 > CAST_BLOCK_BYTES:
        rb //= 2
    return pl.pallas_call(
        _cast_kernel,
        grid=(m // rb,),
        in_specs=[pl.BlockSpec((1, rb, n), lambda i: (layer, i, 0))],
        out_specs=pl.BlockSpec((rb, n), lambda i: (i, 0)),
        out_shape=jax.ShapeDtypeStruct((m, n), BF16),
        compiler_params=_cparams("arbitrary"),
        name="cast_bf16",
    )(w)


MOD_TN = 512


def _mods_kernel(c_ref, w_ref, b_ref, o_ref):
    cv = c_ref[...]
    a = (cv * _sigmoid(cv)).astype(BF16)
    w = w_ref[0].astype(BF16)
    o_ref[0] = jnp.dot(a, w, preferred_element_type=F32) + b_ref[0]


def _mods(cvec, w_ada, b_ada):
    n = 6 * D
    return pl.pallas_call(
        _mods_kernel,
        grid=(DEPTH, n // MOD_TN),
        in_specs=[
            pl.BlockSpec((N_GROUPS, D), lambda l, j: (0, 0)),
            pl.BlockSpec((1, D, MOD_TN), lambda l, j: (l, 0, j)),
            pl.BlockSpec((1, 1, MOD_TN), lambda l, j: (l, 0, j)),
        ],
        out_specs=pl.BlockSpec((1, N_GROUPS, MOD_TN), lambda l, j: (l, 0, j)),
        out_shape=jax.ShapeDtypeStruct((DEPTH, N_GROUPS, n), F32),
        compiler_params=_cparams("arbitrary", "arbitrary"),
        name="adaln_mods",
    )(cvec, w_ada, b_ada.reshape(DEPTH, 1, n))


NM_TM = 256


def _normmod_kernel(has_f, want_t, ctx_tiles, *refs):
    refs = list(refs)
    ht_ref = refs.pop() if want_t else None
    if ctx_tiles:
        c_ref, x_ref, ng_ref, sc_ref, sh_ref, h_ref = refs
        x = jnp.where(pl.program_id(0) < ctx_tiles, c_ref[...], x_ref[...])
    elif has_f:
        x_ref, f_ref, g_ref, ng_ref, sc_ref, sh_ref, xo_ref, h_ref = refs
        x = x_ref[...] + g_ref[0, 0] * f_ref[...]
        xo_ref[...] = x
    else:
        x_ref, ng_ref, sc_ref, sh_ref, h_ref = refs
        x = x_ref[...]
    ms = jnp.mean(x * x, axis=-1, keepdims=True)
    y = x * lax.rsqrt(ms + EPS) * ng_ref[...]
    h = y * (1.0 + sc_ref[0, 0]) + sh_ref[0, 0]
    h_ref[...] = h.astype(BF16)
    if want_t:
        ht_ref[...] = h.T.astype(BF16)


def _mod_spec(layer, k, tm, t0=0):
    return pl.BlockSpec((1, 1, 1, D), lambda i: (layer, _group_of_row((i + t0) * tm), 0, k))


def _normmod(x, modt, layer, norm_g, sc_k, sh_k, f=None, g_k=None, want_t=False, latent_only=False):
    tm = NM_TM
    t0 = _first_tile(tm, latent_only)
    row = pl.BlockSpec((tm, D), lambda i: (i + t0, 0))
    has_f = f is not None
    ctx_tiles = 0
    if isinstance(x, tuple):
        ctx_tiles = CTX_ROWS // tm
        in_specs = [pl.BlockSpec((tm, D), lambda i: (jnp.minimum(i, ctx_tiles - 1), 0)),
                    pl.BlockSpec((tm, D), lambda i: (jnp.maximum(i - ctx_tiles, 0), 0))]
        args = list(x)
    else:
        in_specs = [row]
        args = [x]
    if has_f:
        in_specs += [row, _mod_spec(layer - 1, g_k, tm, t0)]
        args += [f, modt]
    in_specs += [pl.BlockSpec((1, D), lambda i: (0, 0)), _mod_spec(layer, sc_k, tm, t0),
                 _mod_spec(layer, sh_k, tm, t0)]
    args += [norm_g.reshape(1, D), modt, modt]
    h_shape = jax.ShapeDtypeStruct((T, D), BF16)
    if has_f:
        out_specs = [row, row]
        out_shape = [jax.ShapeDtypeStruct((T, D), F32), h_shape]
    else:
        out_specs = [row]
        out_shape = [h_shape]
    if want_t:
        out_specs = out_specs + [pl.BlockSpec((D, tm), lambda i: (0, i + t0))]
        out_shape = out_shape + [jax.ShapeDtypeStruct((D, T), BF16)]
    outs = pl.pallas_call(
        functools.partial(_normmod_kernel, has_f, want_t, ctx_tiles),
        grid=(T // tm - t0,),
        in_specs=in_specs,
        out_specs=out_specs,
        out_shape=out_shape,
        compiler_params=_cparams("arbitrary"),
        name="normmod_resid" if has_f else "normmod",
    )(*args)
    return outs[0] if len(outs) == 1 else tuple(outs)


MM_TM = 512


def _mm_kernel(a_ref, w_ref, o_ref):
    o_ref[...] = jnp.dot(a_ref[...], w_ref[...], preferred_element_type=F32).astype(o_ref.dtype)


def _mm_resid_kernel(ctx_tiles, a_ref, w_ref, *refs):
    acc = jnp.dot(a_ref[...], w_ref[...], preferred_element_type=F32)
    if ctx_tiles:
        rc_ref, rx_ref, g_ref, o_ref = refs
        is_ctx = pl.program_id(0) < ctx_tiles

        @pl.when(is_ctx)
        def _():
            o_ref[...] = rc_ref[...] + g_ref[0, 0] * acc

        @pl.when(jnp.logical_not(is_ctx))
        def _():
            o_ref[...] = rx_ref[...] + g_ref[0, 0] * acc
    else:
        r_ref, g_ref, o_ref = refs
        o_ref[...] = r_ref[...] + g_ref[0, 0] * acc


def _mm_wstat_kernel(a_ref, w_ref, o_ref, wb_ref):
    @pl.when(pl.program_id(1) == 0)
    def _():
        wb_ref[...] = w_ref[0].astype(BF16)

    o_ref[...] = jnp.dot(a_ref[...], wb_ref[...], preferred_element_type=F32).astype(o_ref.dtype)


def _matmul_wstat(a, w, layer, out_dtype, tn):
    m, k = a.shape
    n = w.shape[2]
    return pl.pallas_call(
        _mm_wstat_kernel,
        grid=(n // tn, m // MM_TM),
        in_specs=[pl.BlockSpec((MM_TM, k), lambda j, i: (i, 0)),
                  pl.BlockSpec((1, k, tn), lambda j, i: (layer, 0, j))],
        out_specs=pl.BlockSpec((MM_TM, tn), lambda j, i: (i, j)),
        out_shape=jax.ShapeDtypeStruct((m, n), out_dtype),
        scratch_shapes=[pltpu.VMEM((k, tn), BF16)],
        compiler_params=_cparams("arbitrary", "arbitrary"),
        name="matmul_wstat",
    )(a, w)


def _matmul(a, w, out_dtype, tn, latent_only=False):
    m, k = a.shape
    n = w.shape[1]
    t0 = _first_tile(MM_TM, latent_only)
    return pl.pallas_call(
        _mm_kernel,
        grid=(m // MM_TM - t0, n // tn),
        in_specs=[pl.BlockSpec((MM_TM, k), lambda i, j: (i + t0, 0)),
                  pl.BlockSpec((k, tn), lambda i, j: (0, j))],
        out_specs=pl.BlockSpec((MM_TM, tn), lambda i, j: (i + t0, j)),
        out_shape=jax.ShapeDtypeStruct((m, n), out_dtype),
        compiler_params=_cparams("arbitrary", "arbitrary"),
        name="matmul",
    )(a, w)


def _matmul_resid(a, w, resid, modt, layer, g_k, tn, latent_only=False):
    m, k = a.shape
    n = w.shape[1]
    nj = n // tn
    t0 = _first_tile(MM_TM, latent_only)
    ctx_tiles = 0
    if isinstance(resid, tuple):
        assert t0 == 0
        ctx_tiles = CTX_ROWS // MM_TM
        r_specs = [pl.BlockSpec((MM_TM, tn), lambda i, j: (jnp.minimum(i, ctx_tiles - 1), j)),
                   pl.BlockSpec((MM_TM, tn), lambda i, j: (jnp.maximum(i - ctx_tiles, 0), j))]
        r_args = list(resid)
    else:
        r_specs = [pl.BlockSpec((MM_TM, tn), lambda i, j: (i + t0, j))]
        r_args = [resid]
    return pl.pallas_call(
        functools.partial(_mm_resid_kernel, ctx_tiles),
        grid=(m // MM_TM - t0, nj),
        in_specs=[pl.BlockSpec((MM_TM, k), lambda i, j: (i + t0, 0)),
                  pl.BlockSpec((k, tn), lambda i, j: (0, j))] + r_specs + [
                  pl.BlockSpec((1, 1, 1, tn),
                               lambda i, j: (layer, _group_of_row((i + t0) * MM_TM), 0, g_k * nj + j))],
        out_specs=pl.BlockSpec((MM_TM, tn), lambda i, j: (i + t0, j)),
        out_shape=jax.ShapeDtypeStruct((m, n), F32),
        compiler_params=_cparams("arbitrary", "arbitrary"),
        name="matmul_resid",
    )(a, w, *r_args, modt)


CV_TM = 256
CV_HALO = 16
CV_RC = 64
CV_CC = 128
CV_SUB = SUBLANES
CV_SHROWS = CV_TM + 2 * CV_HALO - CV_SUB


def _conv_kernel(t0, prev_ref, cur_ref, next_ref, w_ref, cb_ref, lg_ref, lb_ref, o_ref, buf_ref, acc_ref):
    i = pl.program_id(0) + t0
    row0 = i * CV_TM
    lat = row0 - CTX_ROWS
    seq_start = jnp.where(row0 < CTX_ROWS, row0 % L == 0, lat % S == 0)
    seq_end = jnp.where(row0 < CTX_ROWS, (row0 + CV_TM) % L == 0, (lat + CV_TM) % S == 0)

    def glu(u):
        return u[:, :CONV_CH] * _sigmoid(u[:, CONV_CH:])

    buf_ref[0, 0:CV_HALO, :] = glu(prev_ref[...]) * jnp.where(seq_start, 0.0, 1.0)
    buf_ref[0, CV_HALO:CV_HALO + CV_TM, :] = glu(cur_ref[...])
    buf_ref[0, CV_HALO + CV_TM:, :] = glu(next_ref[...]) * jnp.where(seq_end, 0.0, 1.0)
    for s in range(1, CV_SUB):
        buf_ref[s, 0:CV_SHROWS, :] = buf_ref[0, s:s + CV_SHROWS, :]

    tap0 = CV_HALO - CONV_K // 2

    def col_chunk(c, carry):
        c0 = pl.multiple_of(c * CV_CC, CV_CC)
        for r in range(CV_TM // CV_RC):
            acc = jnp.zeros((CV_RC, CV_CC), F32)
            for k in range(CONV_K):
                q, s = divmod(k + tap0, CV_SUB)
                r0 = r * CV_RC + q * CV_SUB
                acc = acc + w_ref[k:k + 1, pl.ds(c0, CV_CC)] * buf_ref[s, r0:r0 + CV_RC, pl.ds(c0, CV_CC)]
            acc_ref[r * CV_RC:(r + 1) * CV_RC, pl.ds(c0, CV_CC)] = acc
        return carry

    lax.fori_loop(0, CONV_CH // CV_CC, col_chunk, 0)

    h = acc_ref[...] + cb_ref[...]
    mu = jnp.mean(h, axis=-1, keepdims=True)
    xc = h - mu
    var = jnp.mean(xc * xc, axis=-1, keepdims=True)
    y = xc * lax.rsqrt(var + EPS) * lg_ref[...] + lb_ref[...]
    o_ref[...] = (y * _sigmoid(y)).astype(BF16)


def _conv_module(u, conv_w, conv_b, ln_g, ln_b, latent_only=False):
    hb = CV_TM // CV_HALO
    last = T // CV_HALO - 1
    t0 = _first_tile(CV_TM, latent_only)
    vec = pl.BlockSpec((1, CONV_CH), lambda i: (0, 0))
    return pl.pallas_call(
        functools.partial(_conv_kernel, t0),
        grid=(T // CV_TM - t0,),
        in_specs=[
            pl.BlockSpec((CV_HALO, W_CONV_IN), lambda i: (jnp.maximum((i + t0) * hb - 1, 0), 0)),
            pl.BlockSpec((CV_TM, W_CONV_IN), lambda i: (i + t0, 0)),
            pl.BlockSpec((CV_HALO, W_CONV_IN), lambda i: (jnp.minimum((i + t0 + 1) * hb, last), 0)),
            pl.BlockSpec((CONV_K, CONV_CH), lambda i: (0, 0)),
            vec, vec, vec,
        ],
        out_specs=pl.BlockSpec((CV_TM, CONV_CH), lambda i: (i + t0, 0)),
        out_shape=jax.ShapeDtypeStruct((T, CONV_CH), BF16),
        scratch_shapes=[pltpu.VMEM((CV_SUB, CV_TM + 2 * CV_HALO, CONV_CH), F32),
                        pltpu.VMEM((CV_TM, CONV_CH), F32)],
        compiler_params=_cparams("arbitrary"),
        name="conv_module",
    )(u, u, u, conv_w, conv_b.reshape(1, CONV_CH), ln_g.reshape(1, CONV_CH), ln_b.reshape(1, CONV_CH))


HN_TM = 256


def _headnorm_kernel(qw_ref, qn_ref, kvw_ref, kn_ref, vn_ref, cos_ref, sin_ref, g_ref,
                     qw_o, qn_o, kvw_o, kn_o, vn_o):
    cosf = cos_ref[...]
    sinf = sin_ref[...]
    lane = lax.broadcasted_iota(jnp.int32, (HN_TM, HD), 1)
    first_half = (lane % (HD // 2)) < (HD // 4)

    def norm(x, gi):
        ms = jnp.mean(x * x, axis=-1, keepdims=True)
        return x * lax.rsqrt(ms + EPS) * g_ref[gi:gi + 1, :]

    def rope(x):
        partner = jnp.where(first_half, pltpu.roll(x, HD - HD // 4, 1), pltpu.roll(x, HD // 4, 1))
        return x * cosf + partner * sinf

    for hd in range(WIN_HQ):
        sl = slice(hd * HD, (hd + 1) * HD)
        qw_o[:, sl] = rope(norm(qw_ref[:, sl], 0)).astype(BF16)
    for hd in range(WIN_HKV):
        sl = slice(hd * HD, (hd + 1) * HD)
        kvw_o[:, sl] = rope(norm(kvw_ref[:, sl], 1)).astype(BF16)
    kvw_o[:, W_WIN_KV:] = kvw_ref[:, W_WIN_KV:].astype(BF16)
    for hd in range(NA_H):
        sl = slice(hd * HD, (hd + 1) * HD)
        qn_o[:, sl] = norm(qn_ref[:, sl], 2).astype(BF16)
        kn_o[:, sl] = norm(kn_ref[:, sl], 3).astype(BF16)
    vn_o[...] = vn_ref[...].astype(BF16)


def _headnorm(u, cosf, sinf, gains):
    tm = HN_TM
    kvb = KV_OFF // W_NA_KV
    row128 = pl.BlockSpec((tm, HD), lambda i: (i, 0))

    def ospec(w):
        return pl.BlockSpec((tm, w), lambda i: (i, 0))

    def oshape(w):
        return jax.ShapeDtypeStruct((T, w), BF16)

    return pl.pallas_call(
        _headnorm_kernel,
        grid=(T // tm,),
        in_specs=[
            pl.BlockSpec((tm, W_WIN_Q), lambda i: (i, Q_WIN_OFF // W_WIN_Q)),
            pl.BlockSpec((tm, W_NA_Q), lambda i: (i, Q_NA_OFF // W_NA_Q)),
            pl.BlockSpec((tm, 2 * W_WIN_KV), lambda i: (i, kvb)),
            pl.BlockSpec((tm, W_NA_KV), lambda i: (i, kvb + 1)),
            pl.BlockSpec((tm, W_NA_KV), lambda i: (i, kvb + 2)),
            row128, row128,
            pl.BlockSpec((SUBLANES, HD), lambda i: (0, 0)),
        ],
        out_specs=[ospec(W_WIN_Q), ospec(W_NA_Q), ospec(2 * W_WIN_KV), ospec(W_NA_KV), ospec(W_NA_KV)],
        out_shape=[oshape(W_WIN_Q), oshape(W_NA_Q), oshape(2 * W_WIN_KV), oshape(W_NA_KV), oshape(W_NA_KV)],
        compiler_params=_cparams("arbitrary"),
        name="headnorm_rope",
    )(u, u, u, u, u, cosf, sinf, gains)


def _rope_tables():
    t = jnp.arange(S)
    row = (t // GRID_W).astype(F32)
    col = (t % GRID_W).astype(F32)
    axis_dim = HD // 2
    inv = ROPE_BASE ** (-jnp.arange(0, axis_dim, 2, dtype=F32) / axis_dim)
    ar = row[:, None] * inv
    ac = col[:, None] * inv
    cosl = jnp.concatenate([jnp.cos(ar), jnp.cos(ar), jnp.cos(ac), jnp.cos(ac)], axis=1)
    sinl = jnp.concatenate([-jnp.sin(ar), jnp.sin(ar), -jnp.sin(ac), jnp.sin(ac)], axis=1)
    cosf = jnp.concatenate([jnp.ones((CTX_ROWS, HD), F32), cosl, cosl], axis=0)
    sinf = jnp.concatenate([jnp.zeros((CTX_ROWS, HD), F32), sinl, sinl], axis=0)
    return cosf, sinf


WA_TQ = 128
WA_CTX_TILES = CTX_ROWS // WA_TQ
WA_LAT_TILES = S // WA_TQ


def _win_attn_kernel(t0, sink_ref, q_ref, kp_ref, kc_ref, kn_ref, kx_ref, o_ref):
    i = pl.program_id(0) + t0
    is_ctx = i < WA_CTX_TILES
    n = (i - WA_CTX_TILES) % WA_LAT_TILES
    far = 4 * WA_TQ
    off_prev = jnp.where(jnp.logical_or(is_ctx, n == 0), far, 0)
    off_cur = jnp.where(is_ctx, far, 0)
    off_next = jnp.where(jnp.logical_or(is_ctx, n == WA_LAT_TILES - 1), far, 0)
    rows = WIN_G * WA_TQ
    r = lax.broadcasted_iota(jnp.int32, (rows, WA_TQ), 0) % WA_TQ
    c = lax.broadcasted_iota(jnp.int32, (rows, WA_TQ), 1)
    m_prev = c >= r + off_prev
    m_cur = c >= off_cur
    m_next = c + off_next <= r
    grp = lax.broadcasted_iota(jnp.int32, (rows, 1), 0) // WA_TQ

    for h in range(WIN_HKV):
        q = jnp.concatenate([q_ref[:, (h * WIN_G + g) * HD:(h * WIN_G + g + 1) * HD] for g in range(WIN_G)], axis=0)
        ks = slice(h * HD, (h + 1) * HD)
        vs = slice(W_WIN_KV + h * HD, W_WIN_KV + (h + 1) * HD)
        s_p = jnp.where(m_prev, _dot_nt(q, kp_ref[:, ks]) * ATT_SCALE, NEG)
        s_c = jnp.where(m_cur, _dot_nt(q, kc_ref[:, ks]) * ATT_SCALE, NEG)
        s_n = jnp.where(m_next, _dot_nt(q, kn_ref[:, ks]) * ATT_SCALE, NEG)
        s_x = _dot_nt(q, kx_ref[:, ks]) * ATT_SCALE
        snk = jnp.zeros((rows, 1), F32)
        for g in range(WIN_G):
            snk = jnp.where(grp == g, sink_ref[h * WIN_G + g], snk)
        m = jnp.maximum(jnp.maximum(jnp.max(s_p, axis=-1, keepdims=True), jnp.max(s_c, axis=-1, keepdims=True)),
                        jnp.maximum(jnp.max(s_n, axis=-1, keepdims=True), jnp.max(s_x, axis=-1, keepdims=True)))
        m = jnp.maximum(m, snk)
        p_p = jnp.exp(s_p - m)
        p_c = jnp.exp(s_c - m)
        p_n = jnp.exp(s_n - m)
        p_x = jnp.exp(s_x - m)
        den = (jnp.sum(p_p, axis=-1, keepdims=True) + jnp.sum(p_c, axis=-1, keepdims=True)
               + jnp.sum(p_n, axis=-1, keepdims=True) + jnp.sum(p_x, axis=-1, keepdims=True) + jnp.exp(snk - m))
        o = (jnp.dot(p_p.astype(BF16), kp_ref[:, vs], preferred_element_type=F32)
             + jnp.dot(p_c.astype(BF16), kc_ref[:, vs], preferred_element_type=F32)
             + jnp.dot(p_n.astype(BF16), kn_ref[:, vs], preferred_element_type=F32)
             + jnp.dot(p_x.astype(BF16), kx_ref[:, vs], preferred_element_type=F32))
        o = o / den
        for g in range(WIN_G):
            o_ref[:, (h * WIN_G + g) * HD:(h * WIN_G + g + 1) * HD] = o[g * WA_TQ:(g + 1) * WA_TQ].astype(BF16)


def _win_attention(qw, kvw, sink, latent_only=False):
    nt = T // WA_TQ
    t0 = _first_tile(WA_TQ, latent_only)

    def bounds(i):
        is_ctx = i < WA_CTX_TILES
        b = (i - WA_CTX_TILES) // WA_LAT_TILES
        lo = jnp.where(is_ctx, 0, WA_CTX_TILES + b * WA_LAT_TILES)
        hi = jnp.where(is_ctx, nt - 1, WA_CTX_TILES + (b + 1) * WA_LAT_TILES - 1)
        return lo, hi

    def prev_map(g):
        i = g + t0
        lo, _ = bounds(i)
        return (jnp.maximum(i - 1, lo), 0)

    def next_map(g):
        i = g + t0
        _, hi = bounds(i)
        return (jnp.minimum(i + 1, hi), 0)

    def ctx_map(g):
        i = g + t0
        b = jnp.where(i < WA_CTX_TILES, i // (L // WA_TQ), (i - WA_CTX_TILES) // WA_LAT_TILES)
        return (b, 0)

    kvw_w = 2 * W_WIN_KV
    return pl.pallas_call(
        functools.partial(_win_attn_kernel, t0),
        grid=(nt - t0,),
        in_specs=[
            pl.BlockSpec(memory_space=pltpu.SMEM),
            pl.BlockSpec((WA_TQ, W_WIN_Q), lambda g: (g + t0, 0)),
            pl.BlockSpec((WA_TQ, kvw_w), prev_map),
            pl.BlockSpec((WA_TQ, kvw_w), lambda g: (g + t0, 0)),
            pl.BlockSpec((WA_TQ, kvw_w), next_map),
            pl.BlockSpec((L, kvw_w), ctx_map),
        ],
        out_specs=pl.BlockSpec((WA_TQ, W_WIN_Q), lambda g: (g + t0, 0)),
        out_shape=jax.ShapeDtypeStruct((T, W_WIN_Q), BF16),
        compiler_params=_cparams("arbitrary"),
        name="window_attention",
    )(sink, qw, kvw, kvw, kvw, kvw)


NA_TQ = 256
NA_QROWS = NA_TQ // GRID_W
NA_KROWS = 3 * NA_QROWS
NA_NKEY = NA_KROWS * GRID_W
NA_TYPES = 4
RPB_R = 2 * NA_KH - 1
RPB_C = 2 * NA_KW - 1


def _na_row_valid(ty, a, j):
    if ty == 0:
        return NA_QROWS <= j < NA_QROWS + NA_KH
    if ty == 1:
        return a <= j < a + NA_KH
    if ty == 2:
        return j < NA_KH
    return False


def _rpb_kernel(rpb_ref, o_ref):
    h = pl.program_id(0)
    shp = (GRID_W, 2 * GRID_W)
    qc = lax.broadcasted_iota(jnp.int32, shp, 0)
    lane = lax.broadcasted_iota(jnp.int32, shp, 1)
    kc = lane % GRID_W
    second = lane >= GRID_W
    dcol = kc - qc + (NA_KW - 1)
    cs = jnp.clip(qc - NA_KW // 2, 0, GRID_W - NA_KW)
    colmask = jnp.logical_and(kc >= cs, kc < cs + NA_KW)
    neg = jnp.full(shp, NEG, F32)
    base = h * (RPB_R * RPB_C)
    pair = []
    for dr in range(RPB_R - 1):
        acc = jnp.zeros(shp, F32)
        for dd in range(RPB_C):
            v0 = rpb_ref[base + dr * RPB_C + dd]
            v1 = rpb_ref[base + (dr + 1) * RPB_C + dd]
            acc = jnp.where(dcol == dd, jnp.where(second, v1, v0), acc)
        pair.append(jnp.where(colmask, acc, neg))
    for ty in range(NA_TYPES):
        for a in range(NA_QROWS):
            for jp in range(NA_KROWS // 2):
                j = 2 * jp
                ok0 = _na_row_valid(ty, a, j)
                ok1 = _na_row_valid(ty, a, j + 1)
                dr = j - a + NA_QROWS - 1
                if ok0 and ok1:
                    tile = pair[dr]
                elif ok0:
                    tile = jnp.where(second, neg, pair[dr])
                elif ok1:
                    tile = jnp.where(second, pair[dr], neg)
                else:
                    tile = neg
                o_ref[ty, 0, a * GRID_W:(a + 1) * GRID_W, jp * 2 * GRID_W:(jp + 1) * 2 * GRID_W] = tile


def _rpb_tiles(rpb):
    return pl.pallas_call(
        _rpb_kernel,
        grid=(NA_H,),
        in_specs=[pl.BlockSpec(memory_space=pltpu.SMEM)],
        out_specs=pl.BlockSpec((NA_TYPES, 1, NA_TQ, NA_NKEY), lambda h: (0, h, 0, 0)),
        out_shape=jax.ShapeDtypeStruct((NA_TYPES, NA_H, NA_TQ, NA_NKEY), F32),
        compiler_params=_cparams("arbitrary"),
        name="rpb_tiles",
    )(rpb.reshape(-1))


NA_CTX_TILES = CTX_ROWS // NA_TQ
NA_LAT_TILES = S // NA_TQ


def _na_attn_kernel(q_ref, kp_ref, kc_ref, kn_ref, vp_ref, vc_ref, vn_ref, kx_ref, vx_ref, bias_ref, o_ref):
    for h in range(NA_H):
        hs = slice(h * HD, (h + 1) * HD)
        q = q_ref[:, hs]
        s_p = _dot_nt(q, kp_ref[:, hs]) * ATT_SCALE + bias_ref[0, h, :, 0:NA_TQ]
        s_c = _dot_nt(q, kc_ref[:, hs]) * ATT_SCALE + bias_ref[0, h, :, NA_TQ:2 * NA_TQ]
        s_n = _dot_nt(q, kn_ref[:, hs]) * ATT_SCALE + bias_ref[0, h, :, 2 * NA_TQ:3 * NA_TQ]
        s_x = _dot_nt(q, kx_ref[:, hs]) * ATT_SCALE
        m = jnp.maximum(jnp.maximum(jnp.max(s_p, axis=-1, keepdims=True), jnp.max(s_c, axis=-1, keepdims=True)),
                        jnp.maximum(jnp.max(s_n, axis=-1, keepdims=True), jnp.max(s_x, axis=-1, keepdims=True)))
        p_p = jnp.exp(s_p - m)
        p_c = jnp.exp(s_c - m)
        p_n = jnp.exp(s_n - m)
        p_x = jnp.exp(s_x - m)
        den = (jnp.sum(p_p, axis=-1, keepdims=True) + jnp.sum(p_c, axis=-1, keepdims=True)
               + jnp.sum(p_n, axis=-1, keepdims=True) + jnp.sum(p_x, axis=-1, keepdims=True))
        o = (jnp.dot(p_p.astype(BF16), vp_ref[:, hs], preferred_element_type=F32)
             + jnp.dot(p_c.astype(BF16), vc_ref[:, hs], preferred_element_type=F32)
             + jnp.dot(p_n.astype(BF16), vn_ref[:, hs], preferred_element_type=F32)
             + jnp.dot(p_x.astype(BF16), vx_ref[:, hs], preferred_element_type=F32))
        o_ref[:, hs] = (o / den).astype(BF16)


def _na_attention(qn, kn, vn, bias, latent_only=False):
    nt = T // NA_TQ
    t0 = _first_tile(NA_TQ, latent_only)

    def bounds(i):
        is_ctx = i < NA_CTX_TILES
        b = (i - NA_CTX_TILES) // NA_LAT_TILES
        lo = jnp.where(is_ctx, 0, NA_CTX_TILES + b * NA_LAT_TILES)
        hi = jnp.where(is_ctx, nt - 1, NA_CTX_TILES + (b + 1) * NA_LAT_TILES - 1)
        return lo, hi

    def prev_map(g):
        i = g + t0
        lo, _ = bounds(i)
        return (jnp.maximum(i - 1, lo), 0)

    def next_map(g):
        i = g + t0
        _, hi = bounds(i)
        return (jnp.minimum(i + 1, hi), 0)

    def ctx_map(g):
        i = g + t0
        return (jnp.where(i < NA_CTX_TILES, i, (i - NA_CTX_TILES) // NA_LAT_TILES), 0)

    def bias_map(g):
        i = g + t0
        n = (i - NA_CTX_TILES) % NA_LAT_TILES
        ty = jnp.where(i < NA_CTX_TILES, 3, jnp.where(n == 0, 0, jnp.where(n == NA_LAT_TILES - 1, 2, 1)))
        return (ty, 0, 0, 0)

    w = W_NA_KV
    cur = pl.BlockSpec((NA_TQ, w), lambda g: (g + t0, 0))
    prv = pl.BlockSpec((NA_TQ, w), prev_map)
    nxt = pl.BlockSpec((NA_TQ, w), next_map)
    ctx = pl.BlockSpec((L, w), ctx_map)
    return pl.pallas_call(
        _na_attn_kernel,
        grid=(nt - t0,),
        in_specs=[cur, prv, cur, nxt, prv, cur, nxt, ctx, ctx,
                  pl.BlockSpec((1, NA_H, NA_TQ, NA_NKEY), bias_map)],
        out_specs=cur,
        out_shape=jax.ShapeDtypeStruct((T, W_NA_Q), BF16),
        compiler_params=_cparams("arbitrary"),
        name="neighborhood_attention",
    )(qn, kn, kn, kn, vn, vn, vn, kn, vn, bias)


MG_TM = 512
MG_TN = 1024


def _merge_kernel(ca_ref, aw_ref, an_ref, wc_ref, ww_ref, wn_ref, ga_ref, gb_ref, gc_ref, o_ref):
    ya = jnp.dot(ca_ref[...], wc_ref[...], preferred_element_type=F32)
    yb = jnp.dot(aw_ref[...], ww_ref[...], preferred_element_type=F32)
    yc = jnp.dot(an_ref[...], wn_ref[...], preferred_element_type=F32)
    o = _sigmoid(ga_ref[...]) * ya + _sigmoid(gb_ref[...]) * yb + _sigmoid(gc_ref[...]) * yc
    o_ref[...] = o.astype(BF16)


def _merge(hconv, aw, an, w_conv_out, w_win_out, w_na_out, u, latent_only=False):
    gb0 = GATE_OFF // MG_TN
    gstep = D // MG_TN
    t0 = _first_tile(MG_TM, latent_only)

    def a_spec(k):
        return pl.BlockSpec((MG_TM, k), lambda i, j: (i + t0, 0))

    def w_spec(k):
        return pl.BlockSpec((k, MG_TN), lambda i, j: (0, j))

    def g_spec(which):
        return pl.BlockSpec((MG_TM, MG_TN), lambda i, j: (i + t0, gb0 + which * gstep + j))

    return pl.pallas_call(
        _merge_kernel,
        grid=(T // MG_TM - t0, D // MG_TN),
        in_specs=[a_spec(CONV_CH), a_spec(W_WIN_Q), a_spec(W_NA_Q),
                  w_spec(CONV_CH), w_spec(W_WIN_Q), w_spec(W_NA_Q),
                  g_spec(0), g_spec(1), g_spec(2)],
        out_specs=pl.BlockSpec((MG_TM, MG_TN), lambda i, j: (i + t0, j)),
        out_shape=jax.ShapeDtypeStruct((T, D), BF16),
        compiler_params=_cparams("arbitrary", "arbitrary"),
        name="gated_merge",
    )(hconv, aw, an, w_conv_out, w_win_out, w_na_out, u, u, u)


TK_TT = 512


def _topk_rounds(scores, n_rounds):
    nrow = scores.shape[0]
    idx = lax.broadcasted_iota(jnp.int32, scores.shape, 0).astype(F32)
    work = scores
    rank = jnp.full(scores.shape, float(n_rounds), F32)
    vals = []
    for a in range(n_rounds):
        m = jnp.max(work, axis=0, keepdims=True)
        first = jnp.min(jnp.where(work == m, idx, float(nrow)), axis=0, keepdims=True)
        sel = idx == first
        rank = jnp.where(sel, float(a), rank)
        work = jnp.where(sel, -jnp.inf, work)
        vals.append(m)
    return jnp.concatenate(vals, axis=0), rank


def _topk_rounds_no_ties(scores, n_rounds):
    work = scores
    rank = jnp.full(scores.shape, float(n_rounds), F32)
    vals = []
    for a in range(n_rounds):
        m = jnp.max(work, axis=0, keepdims=True)
        sel = work == m
        rank = jnp.where(sel, float(a), rank)
        work = jnp.where(sel, -jnp.inf, work)
        vals.append(m)
    taken = jnp.sum(jnp.where(rank < float(n_rounds), 1.0, 0.0), axis=0, keepdims=True)
    return jnp.concatenate(vals, axis=0), rank, taken


def _peer_topk_kernel(q_ref, k1_ref, k2_ref, n_ref, e1_ref, r2_ref, e2_ref):
    kk = PEER_TOPK

    def head(h, carry):
        c1 = pl.multiple_of(h * 2 * HD, 2 * HD)
        q1 = q_ref[:, pl.ds(c1, HD)]
        q2 = q_ref[:, pl.ds(c1 + HD, HD)]
        s1 = _dot_nt(k1_ref[h].astype(BF16), q1)
        s2 = _dot_nt(k2_ref[h].astype(BF16), q2)
        f1, fr1, t1 = _topk_rounds_no_ties(s1, kk)
        f2, fr2, t2 = _topk_rounds_no_ties(s2, kk)
        tied = jnp.max(jnp.abs(t1 - float(kk)) + jnp.abs(t2 - float(kk))) > 0.0

        def exact(_):
            return _topk_rounds(s1, kk) + _topk_rounds(s2, kk)

        def keep(_):
            return f1, fr1, f2, fr2

        v1all, rank1, v2all, rank2 = lax.cond(tied, exact, keep, None)
        v1 = [v1all[0:1, :]]
        v2 = [v2all[0:1, :]]
        arow = lax.broadcasted_iota(jnp.int32, v2all.shape, 0).astype(F32)
        cnt = jnp.zeros(v2all.shape, F32)
        front = v1all + v2[0]
        top = v1[0] + v2[0]
        z = jnp.zeros_like(top)
        for _ in range(kk):
            m = jnp.max(front, axis=0, keepdims=True)
            first = jnp.min(jnp.where(front == m, arow, float(kk)), axis=0, keepdims=True)
            sel = arow == first
            cnt = cnt + jnp.where(sel, 1.0, 0.0)
            z = z + jnp.exp(m - top)
            taken = jnp.max(jnp.where(sel, cnt, -1.0), axis=0, keepdims=True)
            nxt = jnp.max(jnp.where(arow == taken, v2all, -jnp.inf), axis=0, keepdims=True)
            front = jnp.where(sel, v1all + nxt, front)
        nfull = jnp.zeros(s1.shape, F32)
        for a in range(kk):
            nfull = jnp.where(rank1 == float(a), cnt[a:a + 1, :], nfull)
        n_ref[h] = nfull
        e1_ref[h] = jnp.exp(s1 - v1[0]) / z
        r2_ref[h] = rank2.astype(BF16)
        e2_ref[h] = jnp.exp(s2 - v2[0]).astype(BF16)
        return carry

    lax.fori_loop(0, PEER_HEADS, head, 0)


def _peer_topk(q, k1, k2, latent_only=False):
    tt = TK_TT
    t0 = _first_tile(tt, latent_only)
    kspec = pl.BlockSpec((PEER_HEADS, N_KEYS, HD), lambda i: (0, 0, 0))
    ospec = pl.BlockSpec((PEER_HEADS, N_KEYS, tt), lambda i: (0, 0, i + t0))
    oshape = jax.ShapeDtypeStruct((PEER_HEADS, N_KEYS, T), F32)
    oshape_b = jax.ShapeDtypeStruct((PEER_HEADS, N_KEYS, T), BF16)
    return pl.pallas_call(
        _peer_topk_kernel,
        grid=(T // tt - t0,),
        in_specs=[pl.BlockSpec((tt, 2 * HD * PEER_HEADS), lambda i: (i + t0, 0)), kspec, kspec],
        out_specs=[ospec] * 4,
        out_shape=[oshape, oshape, oshape_b, oshape_b],
        compiler_params=_cparams("arbitrary"),
        name="peer_topk",
    )(q, k1, k2)


EX_TM = 512
EX_TN = 512
EX_SUB = 256
EX_NC = 1024
EX_LC = 128
EX_RC = 16
EX_I1 = EX_TN // N_KEYS
EX_STEPS_PER_GROUP = SUBLANES // EX_I1
assert EX_I1 * EX_STEPS_PER_GROUP == SUBLANES and EX_STEPS_PER_GROUP == 2


def _gelu_tanh(x):
    c = math.sqrt(2.0 / math.pi)
    t = jnp.tanh(x * (c + (c * 0.044715) * (x * x)))
    hx = 0.5 * x
    return hx + hx * t


def _experts_kernel(h_ref, u_ref, v_ref, n_ref, e1_ref, r2_ref, e2_ref, o_ref):
    j = pl.program_id(1)

    @pl.when(j == 0)
    def _():
        o_ref[...] = jnp.zeros_like(o_ref)

    grp = pl.multiple_of((j // EX_STEPS_PER_GROUP) * SUBLANES, SUBLANES)
    upper = (j % EX_STEPS_PER_GROUP) == 1
    key_rows = {}
    for hh in range(PEER_HEADS):
        for c in range(EX_TM // EX_LC):
            cs = slice(c * EX_LC, (c + 1) * EX_LC)
            n8 = n_ref[hh, pl.ds(grp, SUBLANES), cs]
            e8 = e1_ref[hh, pl.ds(grp, SUBLANES), cs]
            for k in range(EX_I1):
                nrow = jnp.where(upper, n8[EX_I1 + k:EX_I1 + k + 1], n8[k:k + 1])
                e1row = jnp.where(upper, e8[EX_I1 + k:EX_I1 + k + 1], e8[k:k + 1])
                key_rows[(k, hh, c)] = (jnp.broadcast_to(nrow, (EX_RC, EX_LC)).astype(BF16),
                                        jnp.broadcast_to(e1row, (EX_RC, EX_LC)).astype(BF16))

    def gate_tile(k, rows, c):
        cs = slice(c * EX_LC, (c + 1) * EX_LC)
        gate = jnp.zeros((EX_RC, EX_LC), BF16)
        for hh in range(PEER_HEADS):
            nrow, e1row = key_rows[(k, hh, c)]
            gate = gate + jnp.where(r2_ref[hh, rows, cs] < nrow, e2_ref[hh, rows, cs], 0.0) * e1row
        return gate

    w_parts = []
    for s in range(EX_TN // EX_SUB):
        es = slice(s * EX_SUB, (s + 1) * EX_SUB)
        act = _gelu_tanh(jnp.dot(u_ref[es, :], h_ref[...], preferred_element_type=F32)).astype(BF16)
        for ii in range(EX_SUB // N_KEYS):
            k = s * (EX_SUB // N_KEYS) + ii
            for rc in range(N_KEYS // EX_RC):
                rows = slice(rc * EX_RC, (rc + 1) * EX_RC)
                arows = slice(ii * N_KEYS + rc * EX_RC, ii * N_KEYS + (rc + 1) * EX_RC)
                row = []
                for c in range(EX_TM // EX_LC):
                    cs = slice(c * EX_LC, (c + 1) * EX_LC)
                    row.append(gate_tile(k, rows, c) * act[arows, cs])
                w_parts.append(jnp.concatenate(row, axis=1))
    w_t = jnp.concatenate(w_parts, axis=0)
    for nc in range(D // EX_NC):
        ns = slice(nc * EX_NC, (nc + 1) * EX_NC)
        o_ref[:, ns] += lax.dot_general(w_t, v_ref[:, ns], (((0,), (0,)), ((), ())),
                                        preferred_element_type=F32)


def _experts(h_t, u, v, tables, latent_only=False):
    n, e1, r2, e2 = tables
    t0 = _first_tile(EX_TM, latent_only)
    tspec = pl.BlockSpec((PEER_HEADS, N_KEYS, EX_TM), lambda i, j: (0, 0, i + t0))
    return pl.pallas_call(
        _experts_kernel,
        grid=(T // EX_TM - t0, N_EXPERTS // EX_TN),
        in_specs=[pl.BlockSpec((D, EX_TM), lambda i, j: (0, i + t0)),
                  pl.BlockSpec((EX_TN, D), lambda i, j: (j, 0)),
                  pl.BlockSpec((EX_TN, D), lambda i, j: (j, 0)),
                  tspec, tspec, tspec, tspec],
        out_specs=pl.BlockSpec((EX_TM, D), lambda i, j: (i + t0, 0), pipeline_mode=pl.Buffered(1)),
        out_shape=jax.ShapeDtypeStruct((T, D), F32),
        compiler_params=_cparams("arbitrary", "arbitrary"),
        name="peer_experts",
    )(h_t, u, v, n, e1, r2, e2)


FR_TM = 256


def _final_kernel(x_ref, f_ref, g_ref, o_ref):
    o_ref[...] = x_ref[...] + g_ref[0, 0] * f_ref[...]


def _final_residual(x, f, modt, layer, g_k):
    off = CTX_ROWS // FR_TM
    lat = pl.BlockSpec((FR_TM, D), lambda i: (i + off, 0))
    return pl.pallas_call(
        _final_kernel,
        grid=(B * S // FR_TM,),
        in_specs=[lat, lat,
                  pl.BlockSpec((1, 1, 1, D), lambda i: (layer, _group_of_row((i + off) * FR_TM), 0, g_k))],
        out_specs=pl.BlockSpec((FR_TM, D), lambda i: (i, 0)),
        out_shape=jax.ShapeDtypeStruct((B * S, D), F32),
        compiler_params=_cparams("arbitrary"),
        name="final_residual",
    )(x, f, modt)


SH1, SC1, G1, SH2, SC2, G2 = range(6)


def kernel(x, c, ctx, c_ctx, w_ada, b_ada, norm1_g, norm2_g, w_in, conv_w, conv_b, conv_ln_g, conv_ln_b,
           w_conv_out, win_qn_g, win_kn_g, win_sink, w_win_out, na_qn_g, na_kn_g, na_rpb, w_na_out, w_out,
           peer_wq, peer_k1, peer_k2, peer_u, peer_v):
    xs = (ctx.reshape(CTX_ROWS, D), x.reshape(B * S, D))
    cvec = jnp.concatenate([c_ctx[None], c, jnp.zeros((N_GROUPS - 1 - B, D), F32)], axis=0)
    modt = _mods(cvec, w_ada, b_ada).reshape(DEPTH, N_GROUPS, 1, 6 * D)
    cosf, sinf = _rope_tables()

    f = None
    for l in range(DEPTH):
        if l == 0:
            h1 = _normmod(xs, modt, l, norm1_g[l], SC1, SH1)
        else:
            xs, h1 = _normmod(xs, modt, l, norm1_g[l], SC1, SH1, f=f, g_k=G2)
        lat = l == DEPTH - 1
        u = _matmul_wstat(h1, w_in, l, F32, 1024)
        hconv = _conv_module(u, conv_w[l], conv_b[l], conv_ln_g[l], conv_ln_b[l], latent_only=lat)
        gains = jnp.concatenate([win_qn_g[l][None], win_kn_g[l][None], na_qn_g[l][None], na_kn_g[l][None],
                                 jnp.zeros((SUBLANES - 4, HD), F32)], axis=0)
        qw, qn, kvw, kn, vn = _headnorm(u, cosf, sinf, gains)
        aw = _win_attention(qw, kvw, win_sink[l], latent_only=lat)
        an = _na_attention(qn, kn, vn, _rpb_tiles(na_rpb[l]), latent_only=lat)
        merged = _merge(hconv, aw, an, _cast_bf16(w_conv_out, l), _cast_bf16(w_win_out, l),
                        _cast_bf16(w_na_out, l), u, latent_only=lat)
        xs = _matmul_resid(merged, _cast_bf16(w_out, l), xs, modt, l, G1, 1024, latent_only=lat)
        h2, h2_t = _normmod(xs, modt, l, norm2_g[l], SC2, SH2, want_t=True, latent_only=lat)
        q = _matmul(h2, _cast_bf16(peer_wq, l), BF16, 1024, latent_only=lat)
        tables = _peer_topk(q, peer_k1[l], peer_k2[l], latent_only=lat)
        f = _experts(h2_t, _cast_bf16(peer_u, l), _cast_bf16(peer_v, l), tables, latent_only=lat)
    out = _final_residual(xs, f, modt, DEPTH - 1, G2)
    return out.reshape(B, S, D)
```

```python
import functools
import math

import jax
import jax.numpy as jnp
import numpy as np
from jax import lax
from jax.experimental import pallas as pl
from jax.experimental.pallas import tpu as pltpu

F32 = jnp.float32
BF16 = jnp.bfloat16

D = 4096
B = 2
S = 4096
L = 256
DEPTH = 2
GRID_W = 64
HD = 128
EPS = 1e-6
NEG = -1e30
ROPE_BASE = 10000.0

CONV_CH = D // 4
CONV_K = 31
WIN_HQ = 16
WIN_HKV = 4
WIN_G = WIN_HQ // WIN_HKV
NA_H = 8
NA_KH = 8
NA_KW = 16
PEER_HEADS = 8
N_KEYS = 128
N_EXPERTS = N_KEYS * N_KEYS
PEER_TOPK = 16

W_CONV_IN = 2 * CONV_CH
W_WIN_Q = WIN_HQ * HD
W_NA_Q = NA_H * HD
W_GATE = 3 * D
W_WIN_KV = WIN_HKV * HD
W_NA_KV = NA_H * HD
Q_WIN_OFF = W_CONV_IN
Q_NA_OFF = Q_WIN_OFF + W_WIN_Q
GATE_OFF = Q_NA_OFF + W_NA_Q
KV_OFF = GATE_OFF + W_GATE
IN_COLS = KV_OFF + 2 * W_WIN_KV + 2 * W_NA_KV

SUBLANES = 8

CTX_ROWS = B * L
T = CTX_ROWS + B * S
N_GROUPS = SUBLANES
ATT_SCALE = HD ** -0.5

VMEM_LIMIT = 56 * 1024 * 1024


def _cparams(*sem):
    return pltpu.CompilerParams(dimension_semantics=sem, vmem_limit_bytes=VMEM_LIMIT)


def _group_of_row(row0):
    return jnp.where(row0 < CTX_ROWS, 0, 1 + (row0 - CTX_ROWS) // S)


def _sigmoid(z):
    return 0.5 * jnp.tanh(0.5 * z) + 0.5


def _dot_nt(a, b):
    return lax.dot_general(a, b, (((1,), (1,)), ((), ())), preferred_element_type=F32)


def _first_tile(tile, latent_only):
    return CTX_ROWS // tile if latent_only else 0


CAST_BLOCK_BYTES = 8 * 1024 * 1024


def _cast_kernel(w_ref, o_ref):
    o_ref[...] = w_ref[0].astype(BF16)


def _cast_bf16(w, layer):
    _, m, n = w.shape
    rb = m
    while rb * n * 4 > CAST_BLOCK_BYTES:
        rb //= 2
    return pl.pallas_call(
        _cast_kernel,
        grid=(m // rb,),
        in_specs=[pl.BlockSpec((1, rb, n), lambda i: (layer, i, 0))],
        out_specs=pl.BlockSpec((rb, n), lambda i: (i, 0)),
        out_shape=jax.ShapeDtypeStruct((m, n), BF16),
        compiler_params=_cparams("arbitrary"),
        name="cast_bf16",
    )(w)


MOD_TN = 512


def _mods_kernel(c_ref, w_ref, b_ref, o_ref):
    cv = c_ref[...]
    a = (cv * _sigmoid(cv)).astype(BF16)
    w = w_ref[0].astype(BF16)
    o_ref[0] = jnp.dot(a, w, preferred_element_type=F32) + b_ref[0]


def _mods(cvec, w_ada, b_ada):
    n = 6 * D
    return pl.pallas_call(
        _mods_kernel,
        grid=(DEPTH, n // MOD_TN),
        in_specs=[
            pl.BlockSpec((N_GROUPS, D), lambda l, j: (0, 0)),
            pl.BlockSpec((1, D, MOD_TN), lambda l, j: (l, 0, j)),
            pl.BlockSpec((1, 1, MOD_TN), lambda l, j: (l, 0, j)),
        ],
        out_specs=pl.BlockSpec((1, N_GROUPS, MOD_TN), lambda l, j: (l, 0, j)),
        out_shape=jax.ShapeDtypeStruct((DEPTH, N_GROUPS, n), F32),
        compiler_params=_cparams("arbitrary", "arbitrary"),
        name="adaln_mods",
    )(cvec, w_ada, b_ada.reshape(DEPTH, 1, n))


NM_TM = 256


def _normmod_kernel(has_f, want_t, ctx_tiles, *refs):
    refs = list(refs)
    ht_ref = refs.pop() if want_t else None
    if ctx_tiles:
        c_ref, x_ref, ng_ref, sc_ref, sh_ref, h_ref = refs
        x = jnp.where(pl.program_id(0) < ctx_tiles, c_ref[...], x_ref[...])
    elif has_f:
        x_ref, f_ref, g_ref, ng_ref, sc_ref, sh_ref, xo_ref, h_ref = refs
        x = x_ref[...] + g_ref[0, 0] * f_ref[...]
        xo_ref[...] = x
    else:
        x_ref, ng_ref, sc_ref, sh_ref, h_ref = refs
        x = x_ref[...]
    ms = jnp.mean(x * x, axis=-1, keepdims=True)
    y = x * lax.rsqrt(ms + EPS) * ng_ref[...]
    h = y * (1.0 + sc_ref[0, 0]) + sh_ref[0, 0]
    h_ref[...] = h.astype(BF16)
    if want_t:
        ht_ref[...] = h.T.astype(BF16)


def _mod_spec(layer, k, tm, t0=0):
    return pl.BlockSpec((1, 1, 1, D), lambda i: (layer, _group_of_row((i + t0) * tm), 0, k))


def _normmod(x, modt, layer, norm_g, sc_k, sh_k, f=None, g_k=None, want_t=False, latent_only=False):
    tm = NM_TM
    t0 = _first_tile(tm, latent_only)
    row = pl.BlockSpec((tm, D), lambda i: (i + t0, 0))
    has_f = f is not None
    ctx_tiles = 0
    if isinstance(x, tuple):
        ctx_tiles = CTX_ROWS // tm
        in_specs = [pl.BlockSpec((tm, D), lambda i: (jnp.minimum(i, ctx_tiles - 1), 0)),
                    pl.BlockSpec((tm, D), lambda i: (jnp.maximum(i - ctx_tiles, 0), 0))]
        args = list(x)
    else:
        in_specs = [row]
        args = [x]
    if has_f:
        in_specs += [row, _mod_spec(layer - 1, g_k, tm, t0)]
        args += [f, modt]
    in_specs += [pl.BlockSpec((1, D), lambda i: (0, 0)), _mod_spec(layer, sc_k, tm, t0),
                 _mod_spec(layer, sh_k, tm, t0)]
    args += [norm_g.reshape(1, D), modt, modt]
    h_shape = jax.ShapeDtypeStruct((T, D), BF16)
    if has_f:
        out_specs = [row, row]
        out_shape = [jax.ShapeDtypeStruct((T, D), F32), h_shape]
    else:
        out_specs = [row]
        out_shape = [h_shape]
    if want_t:
        out_specs = out_specs + [pl.BlockSpec((D, tm), lambda i: (0, i + t0))]
        out_shape = out_shape + [jax.ShapeDtypeStruct((D, T), BF16)]
    outs = pl.pallas_call(
        functools.partial(_normmod_kernel, has_f, want_t, ctx_tiles),
        grid=(T // tm - t0,),
        in_specs=in_specs,
        out_specs=out_specs,
        out_shape=out_shape,
        compiler_params=_cparams("arbitrary"),
        name="normmod_resid" if has_f else "normmod",
    )(*args)
    return outs[0] if len(outs) == 1 else tuple(outs)


MM_TM = 512


def _mm_kernel(a_ref, w_ref, o_ref):
    o_ref[...] = jnp.dot(a_ref[...], w_ref[...], preferred_element_type=F32).astype(o_ref.dtype)


def _mm_resid_kernel(ctx_tiles, a_ref, w_ref, *refs):
    acc = jnp.dot(a_ref[...], w_ref[...], preferred_element_type=F32)
    if ctx_tiles:
        rc_ref, rx_ref, g_ref, o_ref = refs
        is_ctx = pl.program_id(0) < ctx_tiles

        @pl.when(is_ctx)
        def _():
            o_ref[...] = rc_ref[...] + g_ref[0, 0] * acc

        @pl.when(jnp.logical_not(is_ctx))
        def _():
            o_ref[...] = rx_ref[...] + g_ref[0, 0] * acc
    else:
        r_ref, g_ref, o_ref = refs
        o_ref[...] = r_ref[...] + g_ref[0, 0] * acc


def _mm_wstat_kernel(a_ref, w_ref, o_ref, wb_ref):
    @pl.when(pl.program_id(1) == 0)
    def _():
        wb_ref[...] = w_ref[0].astype(BF16)

    o_ref[...] = jnp.dot(a_ref[...], wb_ref[...], preferred_element_type=F32).astype(o_ref.dtype)


def _matmul_wstat(a, w, layer, out_dtype, tn):
    m, k = a.shape
    n = w.shape[2]
    return pl.pallas_call(
        _mm_wstat_kernel,
        grid=(n // tn, m // MM_TM),
        in_specs=[pl.BlockSpec((MM_TM, k), lambda j, i: (i, 0)),
                  pl.BlockSpec((1, k, tn), lambda j, i: (layer, 0, j))],
        out_specs=pl.BlockSpec((MM_TM, tn), lambda j, i: (i, j)),
        out_shape=jax.ShapeDtypeStruct((m, n), out_dtype),
        scratch_shapes=[pltpu.VMEM((k, tn), BF16)],
        compiler_params=_cparams("arbitrary", "arbitrary"),
        name="matmul_wstat",
    )(a, w)


def _matmul(a, w, out_dtype, tn, latent_only=False):
    m, k = a.shape
    n = w.shape[1]
    t0 = _first_tile(MM_TM, latent_only)
    return pl.pallas_call(
        _mm_kernel,
        grid=(m // MM_TM - t0, n // tn),
        in_specs=[pl.BlockSpec((MM_TM, k), lambda i, j: (i + t0, 0)),
                  pl.BlockSpec((k, tn), lambda i, j: (0, j))],
        out_specs=pl.BlockSpec((MM_TM, tn), lambda i, j: (i + t0, j)),
        out_shape=jax.ShapeDtypeStruct((m, n), out_dtype),
        compiler_params=_cparams("arbitrary", "arbitrary"),
        name="matmul",
    )(a, w)


def _matmul_resid(a, w, resid, modt, layer, g_k, tn, latent_only=False):
    m, k = a.shape
    n = w.shape[1]
    nj = n // tn
    t0 = _first_tile(MM_TM, latent_only)
    ctx_tiles = 0
    if isinstance(resid, tuple):
        assert t0 == 0
        ctx_tiles = CTX_ROWS // MM_TM
        r_specs = [pl.BlockSpec((MM_TM, tn), lambda i, j: (jnp.minimum(i, ctx_tiles - 1), j)),
                   pl.BlockSpec((MM_TM, tn), lambda i, j: (jnp.maximum(i - ctx_tiles, 0), j))]
        r_args = list(resid)
    else:
        r_specs = [pl.BlockSpec((MM_TM, tn), lambda i, j: (i + t0, j))]
        r_args = [resid]
    return pl.pallas_call(
        functools.partial(_mm_resid_kernel, ctx_tiles),
        grid=(m // MM_TM - t0, nj),
        in_specs=[pl.BlockSpec((MM_TM, k), lambda i, j: (i + t0, 0)),
                  pl.BlockSpec((k, tn), lambda i, j: (0, j))] + r_specs + [
                  pl.BlockSpec((1, 1, 1, tn),
                               lambda i, j: (layer, _group_of_row((i + t0) * MM_TM), 0, g_k * nj + j))],
        out_specs=pl.BlockSpec((MM_TM, tn), lambda i, j: (i + t0, j)),
        out_shape=jax.ShapeDtypeStruct((m, n), F32),
        compiler_params=_cparams("arbitrary", "arbitrary"),
        name="matmul_resid",
    )(a, w, *r_args, modt)


CV_TM = 256
CV_HALO = 16
CV_RC = 64
CV_CC = 128
CV_SUB = SUBLANES
CV_SHROWS = CV_TM + 2 * CV_HALO - CV_SUB


def _conv_kernel(t0, prev_ref, cur_ref, next_ref, w_ref, cb_ref, lg_ref, lb_ref, o_ref, buf_ref, acc_ref):
    i = pl.program_id(0) + t0
    row0 = i * CV_TM
    lat = row0 - CTX_ROWS
    seq_start = jnp.where(row0 < CTX_ROWS, row0 % L == 0, lat % S == 0)
    seq_end = jnp.where(row0 < CTX_ROWS, (row0 + CV_TM) % L == 0, (lat + CV_TM) % S == 0)

    def glu(u):
        return u[:, :CONV_CH] * _sigmoid(u[:, CONV_CH:])

    buf_ref[0, 0:CV_HALO, :] = glu(prev_ref[...]) * jnp.where(seq_start, 0.0, 1.0)
    buf_ref[0, CV_HALO:CV_HALO + CV_TM, :] = glu(cur_ref[...])
    buf_ref[0, CV_HALO + CV_TM:, :] = glu(next_ref[...]) * jnp.where(seq_end, 0.0, 1.0)
    for s in range(1, CV_SUB):
        buf_ref[s, 0:CV_SHROWS, :] = buf_ref[0, s:s + CV_SHROWS, :]

    tap0 = CV_HALO - CONV_K // 2

    def col_chunk(c, carry):
        c0 = pl.multiple_of(c * CV_CC, CV_CC)
        for r in range(CV_TM // CV_RC):
            acc = jnp.zeros((CV_RC, CV_CC), F32)
            for k in range(CONV_K):
                q, s = divmod(k + tap0, CV_SUB)
                r0 = r * CV_RC + q * CV_SUB
                acc = acc + w_ref[k:k + 1, pl.ds(c0, CV_CC)] * buf_ref[s, r0:r0 + CV_RC, pl.ds(c0, CV_CC)]
            acc_ref[r * CV_RC:(r + 1) * CV_RC, pl.ds(c0, CV_CC)] = acc
        return carry

    lax.fori_loop(0, CONV_CH // CV_CC, col_chunk, 0)

    h = acc_ref[...] + cb_ref[...]
    mu = jnp.mean(h, axis=-1, keepdims=True)
    xc = h - mu
    var = jnp.mean(xc * xc, axis=-1, keepdims=True)
    y = xc * lax.rsqrt(var + EPS) * lg_ref[...] + lb_ref[...]
    o_ref[...] = (y * _sigmoid(y)).astype(BF16)


def _conv_module(u, conv_w, conv_b, ln_g, ln_b, latent_only=False):
    hb = CV_TM // CV_HALO
    last = T // CV_HALO - 1
    t0 = _first_tile(CV_TM, latent_only)
    vec = pl.BlockSpec((1, CONV_CH), lambda i: (0, 0))
    return pl.pallas_call(
        functools.partial(_conv_kernel, t0),
        grid=(T // CV_TM - t0,),
        in_specs=[
            pl.BlockSpec((CV_HALO, W_CONV_IN), lambda i: (jnp.maximum((i + t0) * hb - 1, 0), 0)),
            pl.BlockSpec((CV_TM, W_CONV_IN), lambda i: (i + t0, 0)),
            pl.BlockSpec((CV_HALO, W_CONV_IN), lambda i: (jnp.minimum((i + t0 + 1) * hb, last), 0)),
            pl.BlockSpec((CONV_K, CONV_CH), lambda i: (0, 0)),
            vec, vec, vec,
        ],
        out_specs=pl.BlockSpec((CV_TM, CONV_CH), lambda i: (i + t0, 0)),
        out_shape=jax.ShapeDtypeStruct((T, CONV_CH), BF16),
        scratch_shapes=[pltpu.VMEM((CV_SUB, CV_TM + 2 * CV_HALO, CONV_CH), F32),
                        pltpu.VMEM((CV_TM, CONV_CH), F32)],
        compiler_params=_cparams("arbitrary"),
        name="conv_module",
    )(u, u, u, conv_w, conv_b.reshape(1, CONV_CH), ln_g.reshape(1, CONV_CH), ln_b.reshape(1, CONV_CH))


HN_TM = 256


def _headnorm_kernel(qw_ref, qn_ref, kvw_ref, kn_ref, vn_ref, cos_ref, sin_ref, g_ref,
                     qw_o, qn_o, kvw_o, kn_o, vn_o):
    cosf = cos_ref[...]
    sinf = sin_ref[...]
    lane = lax.broadcasted_iota(jnp.int32, (HN_TM, HD), 1)
    first_half = (lane % (HD // 2)) < (HD // 4)

    def norm(x, gi):
        ms = jnp.mean(x * x, axis=-1, keepdims=True)
        return x * lax.rsqrt(ms + EPS) * g_ref[gi:gi + 1, :]

    def rope(x):
        partner = jnp.where(first_half, pltpu.roll(x, HD - HD // 4, 1), pltpu.roll(x, HD // 4, 1))
        return x * cosf + partner * sinf

    for hd in range(WIN_HQ):
        sl = slice(hd * HD, (hd + 1) * HD)
        qw_o[:, sl] = rope(norm(qw_ref[:, sl], 0)).astype(BF16)
    for hd in range(WIN_HKV):
        sl = slice(hd * HD, (hd + 1) * HD)
        kvw_o[:, sl] = rope(norm(kvw_ref[:, sl], 1)).astype(BF16)
    kvw_o[:, W_WIN_KV:] = kvw_ref[:, W_WIN_KV:].astype(BF16)
    for hd in range(NA_H):
        sl = slice(hd * HD, (hd + 1) * HD)
        qn_o[:, sl] = norm(qn_ref[:, sl], 2).astype(BF16)
        kn_o[:, sl] = norm(kn_ref[:, sl], 3).astype(BF16)
    vn_o[...] = vn_ref[...].astype(BF16)


def _headnorm(u, cosf, sinf, gains):
    tm = HN_TM
    kvb = KV_OFF // W_NA_KV
    row128 = pl.BlockSpec((tm, HD), lambda i: (i, 0))

    def ospec(w):
        return pl.BlockSpec((tm, w), lambda i: (i, 0))

    def oshape(w):
        return jax.ShapeDtypeStruct((T, w), BF16)

    return pl.pallas_call(
        _headnorm_kernel,
        grid=(T // tm,),
        in_specs=[
            pl.BlockSpec((tm, W_WIN_Q), lambda i: (i, Q_WIN_OFF // W_WIN_Q)),
            pl.BlockSpec((tm, W_NA_Q), lambda i: (i, Q_NA_OFF // W_NA_Q)),
            pl.BlockSpec((tm, 2 * W_WIN_KV), lambda i: (i, kvb)),
            pl.BlockSpec((tm, W_NA_KV), lambda i: (i, kvb + 1)),
            pl.BlockSpec((tm, W_NA_KV), lambda i: (i, kvb + 2)),
            row128, row128,
            pl.BlockSpec((SUBLANES, HD), lambda i: (0, 0)),
        ],
        out_specs=[ospec(W_WIN_Q), ospec(W_NA_Q), ospec(2 * W_WIN_KV), ospec(W_NA_KV), ospec(W_NA_KV)],
        out_shape=[oshape(W_WIN_Q), oshape(W_NA_Q), oshape(2 * W_WIN_KV), oshape(W_NA_KV), oshape(W_NA_KV)],
        compiler_params=_cparams("arbitrary"),
        name="headnorm_rope",
    )(u, u, u, u, u, cosf, sinf, gains)


def _rope_tables():
    t = jnp.arange(S)
    row = (t // GRID_W).astype(F32)
    col = (t % GRID_W).astype(F32)
    axis_dim = HD // 2
    inv = ROPE_BASE ** (-jnp.arange(0, axis_dim, 2, dtype=F32) / axis_dim)
    ar = row[:, None] * inv
    ac = col[:, None] * inv
    cosl = jnp.concatenate([jnp.cos(ar), jnp.cos(ar), jnp.cos(ac), jnp.cos(ac)], axis=1)
    sinl = jnp.concatenate([-jnp.sin(ar), jnp.sin(ar), -jnp.sin(ac), jnp.sin(ac)], axis=1)
    cosf = jnp.concatenate([jnp.ones((CTX_ROWS, HD), F32), cosl, cosl], axis=0)
    sinf = jnp.concatenate([jnp.zeros((CTX_ROWS, HD), F32), sinl, sinl], axis=0)
    return cosf, sinf


WA_TQ = 128
WA_GS = WIN_G
WA_CTX_TILES = CTX_ROWS // WA_TQ
WA_LAT_TILES = S // WA_TQ


def _win_attn_kernel(t0, sink_ref, q_ref, kp_ref, kc_ref, kn_ref, kx_ref, o_ref):
    i = pl.program_id(0) + t0
    is_ctx = i < WA_CTX_TILES
    n = (i - WA_CTX_TILES) % WA_LAT_TILES
    far = 4 * WA_TQ
    off_prev = jnp.where(jnp.logical_or(is_ctx, n == 0), far, 0)
    off_cur = jnp.where(is_ctx, far, 0)
    off_next = jnp.where(jnp.logical_or(is_ctx, n == WA_LAT_TILES - 1), far, 0)
    rows = WA_GS * WA_TQ
    r = lax.broadcasted_iota(jnp.int32, (rows, WA_TQ), 0) % WA_TQ
    c = lax.broadcasted_iota(jnp.int32, (rows, WA_TQ), 1)
    mask = jnp.concatenate([c >= r + off_prev, c >= off_cur, c + off_next <= r,
                            jnp.full((rows, L), True)], axis=1)
    grp = lax.broadcasted_iota(jnp.int32, (rows, 1), 0) // WA_TQ

    for h in range(WIN_HKV):
        ks = slice(h * HD, (h + 1) * HD)
        vs = slice(W_WIN_KV + h * HD, W_WIN_KV + (h + 1) * HD)
        k_all = jnp.concatenate([kp_ref[:, ks], kc_ref[:, ks], kn_ref[:, ks], kx_ref[:, ks]], axis=0)
        v_all = jnp.concatenate([kp_ref[:, vs], kc_ref[:, vs], kn_ref[:, vs], kx_ref[:, vs]], axis=0)
        for g0 in range(0, WIN_G, WA_GS):
            heads = [h * WIN_G + g0 + g for g in range(WA_GS)]
            q = jnp.concatenate([q_ref[:, hd * HD:(hd + 1) * HD] for hd in heads], axis=0)
            s = jnp.where(mask, _dot_nt(q, k_all) * ATT_SCALE, NEG)
            snk = jnp.zeros((rows, 1), F32)
            for g, hd in enumerate(heads):
                snk = jnp.where(grp == g, sink_ref[hd], snk)
            m = jnp.maximum(jnp.max(s, axis=-1, keepdims=True), snk)
            p = jnp.exp(s - m)
            den = jnp.sum(p, axis=-1, keepdims=True) + jnp.exp(snk - m)
            o = jnp.dot(p.astype(BF16), v_all, preferred_element_type=F32) / den
            for g, hd in enumerate(heads):
                o_ref[:, hd * HD:(hd + 1) * HD] = o[g * WA_TQ:(g + 1) * WA_TQ].astype(BF16)


def _win_attention(qw, kvw, sink, latent_only=False):
    nt = T // WA_TQ
    t0 = _first_tile(WA_TQ, latent_only)

    def bounds(i):
        is_ctx = i < WA_CTX_TILES
        b = (i - WA_CTX_TILES) // WA_LAT_TILES
        lo = jnp.where(is_ctx, 0, WA_CTX_TILES + b * WA_LAT_TILES)
        hi = jnp.where(is_ctx, nt - 1, WA_CTX_TILES + (b + 1) * WA_LAT_TILES - 1)
        return lo, hi

    def prev_map(g):
        i = g + t0
        lo, _ = bounds(i)
        return (jnp.maximum(i - 1, lo), 0)

    def next_map(g):
        i = g + t0
        _, hi = bounds(i)
        return (jnp.minimum(i + 1, hi), 0)

    def ctx_map(g):
        i = g + t0
        b = jnp.where(i < WA_CTX_TILES, i // (L // WA_TQ), (i - WA_CTX_TILES) // WA_LAT_TILES)
        return (b, 0)

    kvw_w = 2 * W_WIN_KV
    return pl.pallas_call(
        functools.partial(_win_attn_kernel, t0),
        grid=(nt - t0,),
        in_specs=[
            pl.BlockSpec(memory_space=pltpu.SMEM),
            pl.BlockSpec((WA_TQ, W_WIN_Q), lambda g: (g + t0, 0)),
            pl.BlockSpec((WA_TQ, kvw_w), prev_map),
            pl.BlockSpec((WA_TQ, kvw_w), lambda g: (g + t0, 0)),
            pl.BlockSpec((WA_TQ, kvw_w), next_map),
            pl.BlockSpec((L, kvw_w), ctx_map),
        ],
        out_specs=pl.BlockSpec((WA_TQ, W_WIN_Q), lambda g: (g + t0, 0)),
        out_shape=jax.ShapeDtypeStruct((T, W_WIN_Q), BF16),
        compiler_params=_cparams("arbitrary"),
        name="window_attention",
    )(sink, qw, kvw, kvw, kvw, kvw)


NA_TQ = 256
NA_QROWS = NA_TQ // GRID_W
NA_KROWS = 3 * NA_QROWS
NA_NKEY = NA_KROWS * GRID_W
NA_TYPES = 4
RPB_R = 2 * NA_KH - 1
RPB_C = 2 * NA_KW - 1


def _na_row_valid(ty, a, j):
    if ty == 0:
        return NA_QROWS <= j < NA_QROWS + NA_KH
    if ty == 1:
        return a <= j < a + NA_KH
    if ty == 2:
        return j < NA_KH
    return False


def _rpb_kernel(rpb_ref, o_ref):
    h = pl.program_id(0)
    shp = (GRID_W, 2 * GRID_W)
    qc = lax.broadcasted_iota(jnp.int32, shp, 0)
    lane = lax.broadcasted_iota(jnp.int32, shp, 1)
    kc = lane % GRID_W
    second = lane >= GRID_W
    dcol = kc - qc + (NA_KW - 1)
    cs = jnp.clip(qc - NA_KW // 2, 0, GRID_W - NA_KW)
    colmask = jnp.logical_and(kc >= cs, kc < cs + NA_KW)
    neg = jnp.full(shp, NEG, F32)
    base = h * (RPB_R * RPB_C)
    pair = []
    for dr in range(RPB_R - 1):
        acc = jnp.zeros(shp, F32)
        for dd in range(RPB_C):
            v0 = rpb_ref[base + dr * RPB_C + dd]
            v1 = rpb_ref[base + (dr + 1) * RPB_C + dd]
            acc = jnp.where(dcol == dd, jnp.where(second, v1, v0), acc)
        pair.append(jnp.where(colmask, acc, neg))
    for ty in range(NA_TYPES):
        for a in range(NA_QROWS):
            for jp in range(NA_KROWS // 2):
                j = 2 * jp
                ok0 = _na_row_valid(ty, a, j)
                ok1 = _na_row_valid(ty, a, j + 1)
                dr = j - a + NA_QROWS - 1
                if ok0 and ok1:
                    tile = pair[dr]
                elif ok0:
                    tile = jnp.where(second, neg, pair[dr])
                elif ok1:
                    tile = jnp.where(second, pair[dr], neg)
                else:
                    tile = neg
                o_ref[ty, 0, a * GRID_W:(a + 1) * GRID_W, jp * 2 * GRID_W:(jp + 1) * 2 * GRID_W] = tile


def _rpb_tiles(rpb):
    return pl.pallas_call(
        _rpb_kernel,
        grid=(NA_H,),
        in_specs=[pl.BlockSpec(memory_space=pltpu.SMEM)],
        out_specs=pl.BlockSpec((NA_TYPES, 1, NA_TQ, NA_NKEY), lambda h: (0, h, 0, 0)),
        out_shape=jax.ShapeDtypeStruct((NA_TYPES, NA_H, NA_TQ, NA_NKEY), F32),
        compiler_params=_cparams("arbitrary"),
        name="rpb_tiles",
    )(rpb.reshape(-1))


NA_CTX_TILES = CTX_ROWS // NA_TQ
NA_LAT_TILES = S // NA_TQ


def _na_attn_kernel(q_ref, kp_ref, kc_ref, kn_ref, vp_ref, vc_ref, vn_ref, kx_ref, vx_ref, bias_ref, o_ref):
    for h in range(NA_H):
        hs = slice(h * HD, (h + 1) * HD)
        q = q_ref[:, hs]
        s_p = _dot_nt(q, kp_ref[:, hs]) * ATT_SCALE + bias_ref[0, h, :, 0:NA_TQ]
        s_c = _dot_nt(q, kc_ref[:, hs]) * ATT_SCALE + bias_ref[0, h, :, NA_TQ:2 * NA_TQ]
        s_n = _dot_nt(q, kn_ref[:, hs]) * ATT_SCALE + bias_ref[0, h, :, 2 * NA_TQ:3 * NA_TQ]
        s_x = _dot_nt(q, kx_ref[:, hs]) * ATT_SCALE
        m = jnp.maximum(jnp.maximum(jnp.max(s_p, axis=-1, keepdims=True), jnp.max(s_c, axis=-1, keepdims=True)),
                        jnp.maximum(jnp.max(s_n, axis=-1, keepdims=True), jnp.max(s_x, axis=-1, keepdims=True)))
        p_p = jnp.exp(s_p - m)
        p_c = jnp.exp(s_c - m)
        p_n = jnp.exp(s_n - m)
        p_x = jnp.exp(s_x - m)
        den = (jnp.sum(p_p, axis=-1, keepdims=True) + jnp.sum(p_c, axis=-1, keepdims=True)
               + jnp.sum(p_n, axis=-1, keepdims=True) + jnp.sum(p_x, axis=-1, keepdims=True))
        o = (jnp.dot(p_p.astype(BF16), vp_ref[:, hs], preferred_element_type=F32)
             + jnp.dot(p_c.astype(BF16), vc_ref[:, hs], preferred_element_type=F32)
             + jnp.dot(p_n.astype(BF16), vn_ref[:, hs], preferred_element_type=F32)
             + jnp.dot(p_x.astype(BF16), vx_ref[:, hs], preferred_element_type=F32))
        o_ref[:, hs] = (o / den).astype(BF16)


def _na_attention(qn, kn, vn, bias, latent_only=False):
    nt = T // NA_TQ
    t0 = _first_tile(NA_TQ, latent_only)

    def bounds(i):
        is_ctx = i < NA_CTX_TILES
        b = (i - NA_CTX_TILES) // NA_LAT_TILES
        lo = jnp.where(is_ctx, 0, NA_CTX_TILES + b * NA_LAT_TILES)
        hi = jnp.where(is_ctx, nt - 1, NA_CTX_TILES + (b + 1) * NA_LAT_TILES - 1)
        return lo, hi

    def prev_map(g):
        i = g + t0
        lo, _ = bounds(i)
        return (jnp.maximum(i - 1, lo), 0)

    def next_map(g):
        i = g + t0
        _, hi = bounds(i)
        return (jnp.minimum(i + 1, hi), 0)

    def ctx_map(g):
        i = g + t0
        return (jnp.where(i < NA_CTX_TILES, i, (i - NA_CTX_TILES) // NA_LAT_TILES), 0)

    def bias_map(g):
        i = g + t0
        n = (i - NA_CTX_TILES) % NA_LAT_TILES
        ty = jnp.where(i < NA_CTX_TILES, 3, jnp.where(n == 0, 0, jnp.where(n == NA_LAT_TILES - 1, 2, 1)))
        return (ty, 0, 0, 0)

    w = W_NA_KV
    cur = pl.BlockSpec((NA_TQ, w), lambda g: (g + t0, 0))
    prv = pl.BlockSpec((NA_TQ, w), prev_map)
    nxt = pl.BlockSpec((NA_TQ, w), next_map)
    ctx = pl.BlockSpec((L, w), ctx_map)
    return pl.pallas_call(
        _na_attn_kernel,
        grid=(nt - t0,),
        in_specs=[cur, prv, cur, nxt, prv, cur, nxt, ctx, ctx,
                  pl.BlockSpec((1, NA_H, NA_TQ, NA_NKEY), bias_map)],
        out_specs=cur,
        out_shape=jax.ShapeDtypeStruct((T, W_NA_Q), BF16),
        compiler_params=_cparams("arbitrary"),
        name="neighborhood_attention",
    )(qn, kn, kn, kn, vn, vn, vn, kn, vn, bias)


MG_TM = 512
MG_TN = 1024


def _merge_kernel(ca_ref, aw_ref, an_ref, wc_ref, ww_ref, wn_ref, ga_ref, gb_ref, gc_ref, o_ref):
    ya = jnp.dot(ca_ref[...], wc_ref[...], preferred_element_type=F32)
    yb = jnp.dot(aw_ref[...], ww_ref[...], preferred_element_type=F32)
    yc = jnp.dot(an_ref[...], wn_ref[...], preferred_element_type=F32)
    o = _sigmoid(ga_ref[...]) * ya + _sigmoid(gb_ref[...]) * yb + _sigmoid(gc_ref[...]) * yc
    o_ref[...] = o.astype(BF16)


def _merge(hconv, aw, an, w_conv_out, w_win_out, w_na_out, u, latent_only=False):
    gb0 = GATE_OFF // MG_TN
    gstep = D // MG_TN
    t0 = _first_tile(MG_TM, latent_only)

    def a_spec(k):
        return pl.BlockSpec((MG_TM, k), lambda i, j: (i + t0, 0))

    def w_spec(k):
        return pl.BlockSpec((k, MG_TN), lambda i, j: (0, j))

    def g_spec(which):
        return pl.BlockSpec((MG_TM, MG_TN), lambda i, j: (i + t0, gb0 + which * gstep + j))

    return pl.pallas_call(
        _merge_kernel,
        grid=(T // MG_TM - t0, D // MG_TN),
        in_specs=[a_spec(CONV_CH), a_spec(W_WIN_Q), a_spec(W_NA_Q),
                  w_spec(CONV_CH), w_spec(W_WIN_Q), w_spec(W_NA_Q),
                  g_spec(0), g_spec(1), g_spec(2)],
        out_specs=pl.BlockSpec((MG_TM, MG_TN), lambda i, j: (i + t0, j)),
        out_shape=jax.ShapeDtypeStruct((T, D), BF16),
        compiler_params=_cparams("arbitrary", "arbitrary"),
        name="gated_merge",
    )(hconv, aw, an, w_conv_out, w_win_out, w_na_out, u, u, u)


TK_TT = 512


def _topk_rounds(scores, n_rounds):
    nrow = scores.shape[0]
    idx = lax.broadcasted_iota(jnp.int32, scores.shape, 0).astype(F32)
    work = scores
    rank = jnp.full(scores.shape, float(n_rounds), F32)
    vals = []
    for a in range(n_rounds):
        m = jnp.max(work, axis=0, keepdims=True)
        first = jnp.min(jnp.where(work == m, idx, float(nrow)), axis=0, keepdims=True)
        sel = idx == first
        rank = jnp.where(sel, float(a), rank)
        work = jnp.where(sel, -jnp.inf, work)
        vals.append(m)
    return jnp.concatenate(vals, axis=0), rank


def _topk_rounds_no_ties(scores, n_rounds):
    work = scores
    rank = jnp.full(scores.shape, float(n_rounds), F32)
    vals = []
    for a in range(n_rounds):
        m = jnp.max(work, axis=0, keepdims=True)
        sel = work == m
        rank = jnp.where(sel, float(a), rank)
        work = jnp.where(sel, -jnp.inf, work)
        vals.append(m)
    taken = jnp.sum(jnp.where(rank < float(n_rounds), 1.0, 0.0), axis=0, keepdims=True)
    return jnp.concatenate(vals, axis=0), rank, taken


def _peer_topk_kernel(q_ref, k1_ref, k2_ref, n_ref, e1_ref, r2_ref, e2_ref):
    kk = PEER_TOPK

    def head(h, carry):
        c1 = pl.multiple_of(h * 2 * HD, 2 * HD)
        q1 = q_ref[:, pl.ds(c1, HD)]
        q2 = q_ref[:, pl.ds(c1 + HD, HD)]
        s1 = _dot_nt(k1_ref[h].astype(BF16), q1)
        s2 = _dot_nt(k2_ref[h].astype(BF16), q2)
        f1, fr1, t1 = _topk_rounds_no_ties(s1, kk)
        f2, fr2, t2 = _topk_rounds_no_ties(s2, kk)
        tied = jnp.max(jnp.abs(t1 - float(kk)) + jnp.abs(t2 - float(kk))) > 0.0

        def exact(_):
            return _topk_rounds(s1, kk) + _topk_rounds(s2, kk)

        def keep(_):
            return f1, fr1, f2, fr2

        v1all, rank1, v2all, rank2 = lax.cond(tied, exact, keep, None)
        v1 = [v1all[0:1, :]]
        v2 = [v2all[0:1, :]]
        arow = lax.broadcasted_iota(jnp.int32, v2all.shape, 0).astype(F32)
        cnt = jnp.zeros(v2all.shape, F32)
        front = v1all + v2[0]
        top = v1[0] + v2[0]
        z = jnp.zeros_like(top)
        for _ in range(kk):
            m = jnp.max(front, axis=0, keepdims=True)
            first = jnp.min(jnp.where(front == m, arow, float(kk)), axis=0, keepdims=True)
            sel = arow == first
            cnt = cnt + jnp.where(sel, 1.0, 0.0)
            z = z + jnp.exp(m - top)
            taken = jnp.max(jnp.where(sel, cnt, -1.0), axis=0, keepdims=True)
            nxt = jnp.max(jnp.where(arow == taken, v2all, -jnp.inf), axis=0, keepdims=True)
            front = jnp.where(sel, v1all + nxt, front)
        nfull = jnp.zeros(s1.shape, F32)
        for a in range(kk):
            nfull = jnp.where(rank1 == float(a), cnt[a:a + 1, :], nfull)
        n_ref[h] = nfull
        e1_ref[h] = jnp.exp(s1 - v1[0]) / z
        r2_ref[h] = rank2.astype(BF16)
        e2_ref[h] = jnp.exp(s2 - v2[0]).astype(BF16)
        return carry

    lax.fori_loop(0, PEER_HEADS, head, 0)


def _peer_topk(q, k1, k2, latent_only=False):
    tt = TK_TT
    t0 = _first_tile(tt, latent_only)
    kspec = pl.BlockSpec((PEER_HEADS, N_KEYS, HD), lambda i: (0, 0, 0))
    ospec = pl.BlockSpec((PEER_HEADS, N_KEYS, tt), lambda i: (0, 0, i + t0))
    oshape = jax.ShapeDtypeStruct((PEER_HEADS, N_KEYS, T), F32)
    oshape_b = jax.ShapeDtypeStruct((PEER_HEADS, N_KEYS, T), BF16)
    return pl.pallas_call(
        _peer_topk_kernel,
        grid=(T // tt - t0,),
        in_specs=[pl.BlockSpec((tt, 2 * HD * PEER_HEADS), lambda i: (i + t0, 0)), kspec, kspec],
        out_specs=[ospec] * 4,
        out_shape=[oshape, oshape, oshape_b, oshape_b],
        compiler_params=_cparams("arbitrary"),
        name="peer_topk",
    )(q, k1, k2)


EX_TM = 512
EX_TN = 512
EX_SUB = 256
EX_NC = 1024
EX_LC = 128
EX_RC = 16
EX_I1 = EX_TN // N_KEYS
EX_STEPS_PER_GROUP = SUBLANES // EX_I1
assert EX_I1 * EX_STEPS_PER_GROUP == SUBLANES and EX_STEPS_PER_GROUP == 2


def _gelu_tanh(x):
    c = math.sqrt(2.0 / math.pi)
    t = jnp.tanh(x * (c + (c * 0.044715) * (x * x)))
    hx = 0.5 * x
    return hx + hx * t


def _experts_kernel(h_ref, u_ref, v_ref, n_ref, e1_ref, r2_ref, e2_ref, o_ref):
    j = pl.program_id(1)

    @pl.when(j == 0)
    def _():
        o_ref[...] = jnp.zeros_like(o_ref)

    grp = pl.multiple_of((j // EX_STEPS_PER_GROUP) * SUBLANES, SUBLANES)
    upper = (j % EX_STEPS_PER_GROUP) == 1
    key_rows = {}
    for hh in range(PEER_HEADS):
        for c in range(EX_TM // EX_LC):
            cs = slice(c * EX_LC, (c + 1) * EX_LC)
            n8 = n_ref[hh, pl.ds(grp, SUBLANES), cs]
            e8 = e1_ref[hh, pl.ds(grp, SUBLANES), cs]
            for k in range(EX_I1):
                nrow = jnp.where(upper, n8[EX_I1 + k:EX_I1 + k + 1], n8[k:k + 1])
                e1row = jnp.where(upper, e8[EX_I1 + k:EX_I1 + k + 1], e8[k:k + 1])
                key_rows[(k, hh, c)] = (jnp.broadcast_to(nrow, (EX_RC, EX_LC)).astype(BF16),
                                        jnp.broadcast_to(e1row, (EX_RC, EX_LC)).astype(BF16))

    def gate_tile(k, rows, c):
        cs = slice(c * EX_LC, (c + 1) * EX_LC)
        gate = jnp.zeros((EX_RC, EX_LC), BF16)
        for hh in range(PEER_HEADS):
            nrow, e1row = key_rows[(k, hh, c)]
            gate = gate + jnp.where(r2_ref[hh, rows, cs] < nrow, e2_ref[hh, rows, cs], 0.0) * e1row
        return gate

    w_parts = []
    for s in range(EX_TN // EX_SUB):
        es = slice(s * EX_SUB, (s + 1) * EX_SUB)
        act = _gelu_tanh(jnp.dot(u_ref[es, :], h_ref[...], preferred_element_type=F32)).astype(BF16)
        for ii in range(EX_SUB // N_KEYS):
            k = s * (EX_SUB // N_KEYS) + ii
            for rc in range(N_KEYS // EX_RC):
                rows = slice(rc * EX_RC, (rc + 1) * EX_RC)
                arows = slice(ii * N_KEYS + rc * EX_RC, ii * N_KEYS + (rc + 1) * EX_RC)
                row = []
                for c in range(EX_TM // EX_LC):
                    cs = slice(c * EX_LC, (c + 1) * EX_LC)
                    row.append(gate_tile(k, rows, c) * act[arows, cs])
                w_parts.append(jnp.concatenate(row, axis=1))
    w_t = jnp.concatenate(w_parts, axis=0)
    for nc in range(D // EX_NC):
        ns = slice(nc * EX_NC, (nc + 1) * EX_NC)
        o_ref[:, ns] += lax.dot_general(w_t, v_ref[:, ns], (((0,), (0,)), ((), ())),
                                        preferred_element_type=F32)


def _experts(h_t, u, v, tables, latent_only=False):
    n, e1, r2, e2 = tables
    t0 = _first_tile(EX_TM, latent_only)
    tspec = pl.BlockSpec((PEER_HEADS, N_KEYS, EX_TM), lambda i, j: (0, 0, i + t0))
    return pl.pallas_call(
        _experts_kernel,
        grid=(T // EX_TM - t0, N_EXPERTS // EX_TN),
        in_specs=[pl.BlockSpec((D, EX_TM), lambda i, j: (0, i + t0)),
                  pl.BlockSpec((EX_TN, D), lambda i, j: (j, 0)),
                  pl.BlockSpec((EX_TN, D), lambda i, j: (j, 0)),
                  tspec, tspec, tspec, tspec],
        out_specs=pl.BlockSpec((EX_TM, D), lambda i, j: (i + t0, 0), pipeline_mode=pl.Buffered(1)),
        out_shape=jax.ShapeDtypeStruct((T, D), F32),
        compiler_params=_cparams("arbitrary", "arbitrary"),
        name="peer_experts",
    )(h_t, u, v, n, e1, r2, e2)


FR_TM = 256


def _final_kernel(x_ref, f_ref, g_ref, o_ref):
    o_ref[...] = x_ref[...] + g_ref[0, 0] * f_ref[...]


def _final_residual(x, f, modt, layer, g_k):
    off = CTX_ROWS // FR_TM
    lat = pl.BlockSpec((FR_TM, D), lambda i: (i + off, 0))
    return pl.pallas_call(
        _final_kernel,
        grid=(B * S // FR_TM,),
        in_specs=[lat, lat,
                  pl.BlockSpec((1, 1, 1, D), lambda i: (layer, _group_of_row((i + off) * FR_TM), 0, g_k))],
        out_specs=pl.BlockSpec((FR_TM, D), lambda i: (i, 0)),
        out_shape=jax.ShapeDtypeStruct((B * S, D), F32),
        compiler_params=_cparams("arbitrary"),
        name="final_residual",
    )(x, f, modt)


SH1, SC1, G1, SH2, SC2, G2 = range(6)


def kernel(x, c, ctx, c_ctx, w_ada, b_ada, norm1_g, norm2_g, w_in, conv_w, conv_b, conv_ln_g, conv_ln_b,
           w_conv_out, win_qn_g, win_kn_g, win_sink, w_win_out, na_qn_g, na_kn_g, na_rpb, w_na_out, w_out,
           peer_wq, peer_k1, peer_k2, peer_u, peer_v):
    xs = (ctx.reshape(CTX_ROWS, D), x.reshape(B * S, D))
    cvec = jnp.concatenate([c_ctx[None], c, jnp.zeros((N_GROUPS - 1 - B, D), F32)], axis=0)
    modt = _mods(cvec, w_ada, b_ada).reshape(DEPTH, N_GROUPS, 1, 6 * D)
    cosf, sinf = _rope_tables()

    f = None
    for l in range(DEPTH):
        if l == 0:
            h1 = _normmod(xs, modt, l, norm1_g[l], SC1, SH1)
        else:
            xs, h1 = _normmod(xs, modt, l, norm1_g[l], SC1, SH1, f=f, g_k=G2)
        lat = l == DEPTH - 1
        u = _matmul_wstat(h1, w_in, l, F32, 1024)
        hconv = _conv_module(u, conv_w[l], conv_b[l], conv_ln_g[l], conv_ln_b[l], latent_only=lat)
        gains = jnp.concatenate([win_qn_g[l][None], win_kn_g[l][None], na_qn_g[l][None], na_kn_g[l][None],
                                 jnp.zeros((SUBLANES - 4, HD), F32)], axis=0)
        qw, qn, kvw, kn, vn = _headnorm(u, cosf, sinf, gains)
        aw = _win_attention(qw, kvw, win_sink[l], latent_only=lat)
        an = _na_attention(qn, kn, vn, _rpb_tiles(na_rpb[l]), latent_only=lat)
        merged = _merge(hconv, aw, an, _cast_bf16(w_conv_out, l), _cast_bf16(w_win_out, l),
                        _cast_bf16(w_na_out, l), u, latent_only=lat)
        xs = _matmul_resid(merged, _cast_bf16(w_out, l), xs, modt, l, G1, 1024, latent_only=lat)
        h2, h2_t = _normmod(xs, modt, l, norm2_g[l], SC2, SH2, want_t=True, latent_only=lat)
        q = _matmul(h2, _cast_bf16(peer_wq, l), BF16, 1024, latent_only=lat)
        tables = _peer_topk(q, peer_k1[l], peer_k2[l], latent_only=lat)
        f = _experts(h2_t, _cast_bf16(peer_u, l), _cast_bf16(peer_v, l), tables, latent_only=lat)
    out = _final_residual(xs, f, modt, DEPTH - 1, G2)
    return out.reshape(B, S, D)
```
